```python
import jax, jax.numpy as jnp
from jax import lax
import numpy as np

D_MODEL = 1024
BATCH = 16
SEQ = 2048
DEPTH = 2

GRID_W = 64
CTX_LEN = 256
HEAD_DIM = 64
A_WIDTH = 512
A_HEADS = A_WIDTH // HEAD_DIM
B_WIDTH = 512
B_HEADS = B_WIDTH // HEAD_DIM
DECAY_LORA = 64
ICLR_LORA = 64
GATE_LORA = 128
GN_EPS = 64e-5
NA_ROWS = 8
NA_COLS = 16
CONV_WIDTH = 31
N_EXPERTS = 32
TOP_K = 4
D_EXPERT = D_MODEL
SWIGLU_ALPHA = 1.702
SWIGLU_LIMIT = 7.0
EXPERT_BLOCK = 256
RMS_EPS = 1e-6
LN_EPS = 1e-5
N_EVEN = (DEPTH + 1) // 2
N_ODD = DEPTH // 2

COL_QB = 0
COL_RA = COL_QB + B_WIDTH
COL_GD = COL_RA + A_WIDTH
COL_KA = COL_GD + GATE_LORA
COL_VA = COL_KA + A_WIDTH
COL_WD = COL_VA + A_WIDTH
COL_AD = COL_WD + 2 * DECAY_LORA
COL_KB = COL_AD + 2 * ICLR_LORA
COL_VB = COL_KB + B_WIDTH
COL_END = COL_VB + B_WIDTH
SHIFT_COLS = COL_KB - COL_RA

kernel_name = 'hybrid_rwkv7_natten_conformer_moe_dit'


def _rmsnorm(x, g):
    xf = x.astype(jnp.float32)
    y = xf * lax.rsqrt(jnp.mean(xf * xf, axis=-1, keepdims=True) + RMS_EPS)
    return y.astype(x.dtype) * g


def _layernorm(x, g, b):
    xf = x.astype(jnp.float32)
    mu = jnp.mean(xf, axis=-1, keepdims=True)
    var = jnp.mean(jnp.square(xf - mu), axis=-1, keepdims=True)
    return ((xf - mu) * lax.rsqrt(var + LN_EPS)).astype(x.dtype) * g + b


def _adaln(x, g, shift, scale):
    return _rmsnorm(x, g) * (1 + scale) + shift


def _token_shift(p, mu_prev, mu_next):
    prev = jnp.pad(p, ((0, 0), (1, 0), (0, 0)))[:, :-1]
    nxt = jnp.pad(p, ((0, 0), (0, 1), (0, 0)))[:, 1:]
    return p + mu_prev * (prev - p) + mu_next * (nxt - p)


def _to_scan(x):
    B, T, _, C = x.shape
    x = jnp.stack([x[:, :, 0], jnp.flip(x[:, :, 1], axis=1)], axis=0)
    return x.reshape(2, B, T, C // HEAD_DIM, HEAD_DIM).transpose(2, 0, 1, 3, 4).astype(jnp.float32)


def _rwkv7_terms(s, w0, w2, a0, a2, k_k, k_a):
    B, T, _ = s.shape
    k = s[..., :A_WIDTH]
    v = s[..., A_WIDTH:2 * A_WIDTH]
    wd = s[..., 2 * A_WIDTH:2 * A_WIDTH + 2 * DECAY_LORA].reshape(B, T, 2, DECAY_LORA)
    ad = s[..., 2 * A_WIDTH + 2 * DECAY_LORA:].reshape(B, T, 2, ICLR_LORA)
    w_log = -jax.nn.softplus(-(w0 + jnp.einsum('btdr,drc->btdc', jnp.tanh(wd), w2))) - 0.5
    decay = jnp.exp(-jnp.exp(w_log.astype(jnp.float32)))
    a = jax.nn.sigmoid(a0 + jnp.einsum('btdr,drc->btdc', ad, a2))
    kk = (k * k_k).reshape(B, T, A_HEADS, HEAD_DIM).astype(jnp.float32)
    kk = (kk / jnp.maximum(jnp.sqrt(jnp.sum(kk * kk, axis=-1, keepdims=True)), 1e-12)).reshape(B, T, A_WIDTH)
    k_dir = k[:, :, None] * (1 + (a - 1) * k_a)
    both = lambda t: jnp.broadcast_to(t[:, :, None], (B, T, 2, A_WIDTH))
    scan_in = (_to_scan(decay), _to_scan(k_dir), _to_scan(both(v)),
               _to_scan(both(-kk)), _to_scan(kk[:, :, None] * a))
    return scan_in, k_dir, v


def _wkv7_update(S, w, k, v, aa, bb):
    sa = jnp.einsum('dbhij,dbhj->dbhi', S, aa)
    return S * w[..., None, :] + sa[..., :, None] * bb[..., None, :] + v[..., :, None] * k[..., None, :]


def _neighbourhood_attention(q, k, v, kc, vc, rpb):
    B, T, _ = q.shape
    rows = T // GRID_W
    kh = min(NA_ROWS, rows)
    H = B_HEADS
    grid = lambda t: t.reshape(B, rows, GRID_W, H, HEAD_DIM)
    qg = grid(q) * (HEAD_DIM ** -0.5)
    kg = grid(k)
    vg = grid(v)
    kc = kc.reshape(B, -1, H, HEAD_DIM)
    vc = vc.reshape(B, -1, H, HEAD_DIM)
    col = jnp.arange(GRID_W)
    c_start = jnp.clip(col - NA_COLS // 2, 0, GRID_W - NA_COLS)
    col_ok = (col[None, :] >= c_start[:, None]) & (col[None, :] < c_start[:, None] + NA_COLS)
    dc = jnp.clip(col[None, :] - col[:, None], 1 - NA_COLS, NA_COLS - 1) + NA_COLS - 1
    n_loc = kh * GRID_W

    def row_block(r):
        r_start = jnp.clip(r - kh // 2, 0, rows - kh)
        k_rows = lax.dynamic_slice_in_dim(kg, r_start, kh, axis=1)
        v_rows = lax.dynamic_slice_in_dim(vg, r_start, kh, axis=1)
        q_r = lax.dynamic_index_in_dim(qg, r, axis=1, keepdims=False)
        dr = r_start + jnp.arange(kh) - r + NA_ROWS - 1
        bias = rpb[:, dr[None, :, None], dc[:, None, :]]
        s_loc = jnp.einsum('bchn,bkwhn->bhckw', q_r, k_rows).astype(jnp.float32) + bias
        s_loc = jnp.where(col_ok[:, None, :], s_loc, -jnp.inf)
        s_ctx = jnp.einsum('bchn,blhn->bhcl', q_r, kc).astype(jnp.float32)
        p = jax.nn.softmax(jnp.concatenate([s_loc.reshape(B, H, GRID_W, n_loc), s_ctx], axis=-1), axis=-1).astype(v.dtype)
        o = jnp.einsum('bhckw,bkwhn->bchn', p[..., :n_loc].reshape(B, H, GRID_W, kh, GRID_W), v_rows)
        return o + jnp.einsum('bhcl,blhn->bchn', p[..., n_loc:], vc)

    o = lax.map(row_block, jnp.arange(rows))
    return o.transpose(1, 0, 2, 3, 4).reshape(B, T, B_WIDTH)


def _even_mixer(h, hc, w_in, mu_prev, mu_next, w0, w2, a0, a2, g2, k_k, k_a, r_k, lnx_g, lnx_b, rpb, w_out):
    B, T, _ = h.shape
    p = h @ w_in
    pc = hc @ w_in[:, COL_KA:]
    s = _token_shift(p[..., COL_RA:COL_KB], mu_prev, mu_next)
    sc = _token_shift(pc[..., :COL_KB - COL_KA], mu_prev[COL_KA - COL_RA:], mu_next[COL_KA - COL_RA:])
    r = s[..., :A_WIDTH]
    g = jax.nn.sigmoid(s[..., A_WIDTH:A_WIDTH + GATE_LORA]) @ g2
    scan_in, k_dir, v = _rwkv7_terms(s[..., COL_KA - COL_RA:], w0, w2, a0, a2, k_k, k_a)
    scan_c, _, _ = _rwkv7_terms(sc, w0, w2, a0, a2, k_k, k_a)
    S0 = jnp.zeros((2, B, A_HEADS, HEAD_DIM, HEAD_DIM), jnp.float32)
    S_ctx, _ = lax.scan(lambda S, t: (_wkv7_update(S, *t), None), S0, scan_c)

    def step(S, t):
        S = _wkv7_update(S, *t[1:])
        return S, jnp.einsum('dbhij,dbhj->dbhi', S, t[0])

    r_scan = _to_scan(jnp.broadcast_to(r[:, :, None], (B, T, 2, A_WIDTH)))
    _, y = lax.scan(step, S_ctx, (r_scan,) + scan_in)
    y = (y[:, 0] + jnp.flip(y[:, 1], axis=0)).transpose(1, 0, 2, 3)
    mu = jnp.mean(y, axis=-1, keepdims=True)
    var = jnp.mean(jnp.square(y - mu), axis=-1, keepdims=True)
    y = ((y - mu) * lax.rsqrt(var + GN_EPS)).reshape(B, T, A_WIDTH).astype(h.dtype) * lnx_g + lnx_b
    rk = jnp.sum(r.reshape(B, T, 1, A_HEADS, HEAD_DIM) * k_dir.reshape(B, T, 2, A_HEADS, HEAD_DIM) * r_k, axis=(2, 4))
    bonus = (rk[..., None] * v.reshape(B, T, A_HEADS, HEAD_DIM)).reshape(B, T, A_WIDTH)
    o_a = (y + bonus) * g
    o_b = _neighbourhood_attention(p[..., COL_QB:COL_RA], p[..., COL_KB:COL_VB], p[..., COL_VB:],
                                   pc[..., COL_KB - COL_KA:COL_VB - COL_KA], pc[..., COL_VB - COL_KA:], rpb)
    return jnp.concatenate([o_a, o_b], axis=-1) @ w_out


def _conformer_conv(h, pw1_w, pw1_b, dw_w, dw_b, ln_g, ln_b, pw2_w, pw2_b):
    u = h @ pw1_w + pw1_b
    d = u.shape[-1] // 2
    u = u[..., :d] * jax.nn.sigmoid(u[..., d:])
    u = lax.conv_general_dilated(u, dw_w[:, None, :], (1,), [(CONV_WIDTH // 2, CONV_WIDTH // 2)],
                                 dimension_numbers=('NWC', 'WIO', 'NWC'), feature_group_count=d) + dw_b
    u = jax.nn.silu(_layernorm(u, ln_g, ln_b))
    return u @ pw2_w + pw2_b


def _moe(h, w_r, b_r, w1, b1, w2, b2):
    B, T, D = h.shape
    xt = h.reshape(-1, D)
    n = xt.shape[0]
    m = n * TOP_K
    logits = (xt @ w_r + b_r).astype(jnp.float32)
    top_v, top_i = lax.top_k(logits, TOP_K)
    gate = jax.nn.softmax(top_v, axis=-1).astype(h.dtype)
    flat_e = top_i.reshape(-1)
    order = jnp.argsort(flat_e)
    sorted_e = flat_e[order]
    counts = jnp.zeros((N_EXPERTS,), jnp.int32).at[flat_e].add(1)
    padded = (counts + EXPERT_BLOCK - 1) // EXPERT_BLOCK * EXPERT_BLOCK
    pad_end = jnp.cumsum(padded)
    pad_start = pad_end - padded
    start = jnp.cumsum(counts) - counts
    dest = pad_start[sorted_e] + jnp.arange(m) - start[sorted_e]
    n_blocks = -(-(m + N_EXPERTS * (EXPERT_BLOCK - 1)) // EXPERT_BLOCK)
    buf_tok = jnp.full((n_blocks * EXPERT_BLOCK,), n, jnp.int32).at[dest].set((order // TOP_K).astype(jnp.int32))
    buf_w = jnp.zeros((n_blocks * EXPERT_BLOCK,), h.dtype).at[dest].set(gate.reshape(-1)[order])
    block_e = jnp.minimum(jnp.searchsorted(pad_end, jnp.arange(n_blocks) * EXPERT_BLOCK, side='right'), N_EXPERTS - 1)
    x_pad = jnp.concatenate([xt, jnp.zeros((1, D), xt.dtype)], axis=0)

    def block(args):
        tok, e = args
        u = x_pad[tok] @ w1[e] + b1[e]
        glu = jnp.minimum(u[:, ::2], SWIGLU_LIMIT)
        lin = jnp.clip(u[:, 1::2], -SWIGLU_LIMIT, SWIGLU_LIMIT)
        act = glu * jax.nn.sigmoid(SWIGLU_ALPHA * glu) * (lin + 1)
        return act @ w2[e] + b2[e]

    y = lax.map(block, (buf_tok.reshape(n_blocks, EXPERT_BLOCK), block_e)).reshape(-1, D)
    out = jnp.zeros((n + 1, D), h.dtype).at[buf_tok].add(y * buf_w[:, None])[:n]
    return out.reshape(B, T, D)


def setup_inputs(seed: int = 0) -> dict:
    key = jax.random.key(seed)
    ks = iter(jax.random.split(key, 64))
    D = D_MODEL
    nrm = lambda shape, s: s * jax.random.normal(next(ks), shape, jnp.float32)
    uni = lambda shape, lo, hi: jax.random.uniform(next(ks), shape, jnp.float32, lo, hi)
    return {
        'x': nrm((BATCH, SEQ, D), 1.0),
        'c': nrm((BATCH, D), 1.0),
        'ctx': nrm((BATCH, CTX_LEN, D), 1.0),
        'c_ctx': nrm((D,), 1.0),
        'ada_w': nrm((DEPTH, D, 6 * D), 0.02),
        'ada_b': nrm((DEPTH, 6 * D), 0.02),
        'norm_mix_g': 1 + nrm((DEPTH, D), 0.02),
        'norm_ffn_g': 1 + nrm((DEPTH, D), 0.02),
        'final_norm_g': 1 + nrm((D,), 0.02),
        'mix_w_in': nrm((N_EVEN, D, COL_END), D ** -0.5),
        'shift_mu_prev': uni((N_EVEN, SHIFT_COLS), 0.0, 0.5),
        'shift_mu_next': uni((N_EVEN, SHIFT_COLS), 0.0, 0.5),
        'decay_w0': uni((N_EVEN, 2, A_WIDTH), -2.0, 1.0),
        'decay_w2': nrm((N_EVEN, 2, DECAY_LORA, A_WIDTH), 0.1 * DECAY_LORA ** -0.5),
        'iclr_a0': nrm((N_EVEN, 2, A_WIDTH), 0.1),
        'iclr_a2': nrm((N_EVEN, 2, ICLR_LORA, A_WIDTH), ICLR_LORA ** -0.5),
        'gate_g2': nrm((N_EVEN, GATE_LORA, A_WIDTH), GATE_LORA ** -0.5),
        'key_k': 0.85 + nrm((N_EVEN, A_WIDTH), 0.02),
        'key_a': 1 + nrm((N_EVEN, A_WIDTH), 0.02),
        'bonus_r_k': nrm((N_EVEN, A_HEADS, HEAD_DIM), 0.1),
        'lnx_g': 1 + nrm((N_EVEN, A_WIDTH), 0.02),
        'lnx_b': nrm((N_EVEN, A_WIDTH), 0.01),
        'na_rpb': nrm((N_EVEN, B_HEADS, 2 * NA_ROWS - 1, 2 * NA_COLS - 1), 0.02),
        'mix_w_out': nrm((N_EVEN, A_WIDTH + B_WIDTH, D), (A_WIDTH + B_WIDTH) ** -0.5),
        'conv_pw1_w': nrm((N_ODD, D, 2 * D), D ** -0.5),
        'conv_pw1_b': nrm((N_ODD, 2 * D), 0.01),
        'conv_dw_w': nrm((N_ODD, CONV_WIDTH, D), CONV_WIDTH ** -0.5),
        'conv_dw_b': nrm((N_ODD, D), 0.01),
        'conv_ln_g': 1 + nrm((N_ODD, D), 0.02),
        'conv_ln_b': nrm((N_ODD, D), 0.01),
        'conv_pw2_w': nrm((N_ODD, D, D), D ** -0.5),
        'conv_pw2_b': nrm((N_ODD, D), 0.01),
        'router_w': nrm((DEPTH, D, N_EXPERTS), D ** -0.5),
        'router_b': nrm((DEPTH, N_EXPERTS), 0.01),
        'expert_w1': nrm((DEPTH, N_EXPERTS, D, 2 * D_EXPERT), D ** -0.5),
        'expert_b1': nrm((DEPTH, N_EXPERTS, 2 * D_EXPERT), 0.01),
        'expert_w2': nrm((DEPTH, N_EXPERTS, D_EXPERT, D), D_EXPERT ** -0.5),
        'expert_b2': nrm((DEPTH, N_EXPERTS, D), 0.01),
    }


def reference(x, c, ctx, c_ctx, ada_w, ada_b, norm_mix_g, norm_ffn_g, final_norm_g,
              mix_w_in, shift_mu_prev, shift_mu_next, decay_w0, decay_w2, iclr_a0, iclr_a2,
              gate_g2, key_k, key_a, bonus_r_k, lnx_g, lnx_b, na_rpb, mix_w_out,
              conv_pw1_w, conv_pw1_b, conv_dw_w, conv_dw_b, conv_ln_g, conv_ln_b, conv_pw2_w, conv_pw2_b,
              router_w, router_b, expert_w1, expert_b1, expert_w2, expert_b2):
    silu_c = jax.nn.silu(c)
    silu_cc = jax.nn.silu(c_ctx)
    for l in range(DEPTH):
        mod = silu_c @ ada_w[l] + ada_b[l]
        sh1, sc1, g1, sh2, sc2, g2 = jnp.split(mod[:, None, :], 6, axis=-1)
        h = _adaln(x, norm_mix_g[l], sh1, sc1)
        if l % 2 == 0:
            i = l // 2
            mod_c = silu_cc @ ada_w[l] + ada_b[l]
            hc = _adaln(ctx, norm_mix_g[l], mod_c[:D_MODEL], mod_c[D_MODEL:2 * D_MODEL])
            y = _even_mixer(h, hc, mix_w_in[i], shift_mu_prev[i], shift_mu_next[i], decay_w0[i], decay_w2[i],
                            iclr_a0[i], iclr_a2[i], gate_g2[i], key_k[i], key_a[i], bonus_r_k[i],
                            lnx_g[i], lnx_b[i], na_rpb[i], mix_w_out[i])
        else:
            i = l // 2
            y = _conformer_conv(h, conv_pw1_w[i], conv_pw1_b[i], conv_dw_w[i], conv_dw_b[i],
                                conv_ln_g[i], conv_ln_b[i], conv_pw2_w[i], conv_pw2_b[i])
        x = x + g1 * y
        x = x + g2 * _moe(_adaln(x, norm_ffn_g[l], sh2, sc2), router_w[l], router_b[l],
                          expert_w1[l], expert_b1[l], expert_w2[l], expert_b2[l])
    return _rmsnorm(x, final_norm_g)
```

```python
import functools

import jax
import jax.numpy as jnp
from jax import lax
from jax.experimental import pallas as pl
from jax.experimental.pallas import tpu as pltpu

F32 = jnp.float32
BF16 = jnp.bfloat16
HIGHEST = lax.Precision.HIGHEST

HEAD_DIM = 64
GRID_W = 64
NA_ROWS = 8
NA_COLS = 16
CONV_WIDTH = 31
N_EXPERTS = 32
TOP_K = 4
SWIGLU_ALPHA = 1.702
SWIGLU_LIMIT = 7.0
RMS_EPS = 1e-6
LN_EPS = 1e-5
GN_EPS = 64e-5
NEG_BIG = -1e30

VMEM_LIMIT_BYTES = 52 * 1024 * 1024
TOKEN_TILE = 256
SCAN_BLOCK = 16
EXPERT_ROWS = 256
HALO = 16


def _params(*sem):
    return pltpu.CompilerParams(dimension_semantics=sem, vmem_limit_bytes=VMEM_LIMIT_BYTES)


def _adaln(x, g, sh, sc):
    y = x * lax.rsqrt(jnp.mean(x * x, axis=-1, keepdims=True) + RMS_EPS)
    return (y * g) * (1.0 + sc) + sh


def _sigmoid(x):
    return 1.0 / (1.0 + jnp.exp(-x))


def _split_dot(x, m):
    hi = x.astype(BF16)
    r1 = x - hi.astype(F32)
    mid = r1.astype(BF16)
    lo = (r1 - mid.astype(F32)).astype(BF16)
    dot = functools.partial(jnp.dot, preferred_element_type=F32)
    return dot(hi, m) + dot(mid, m) + dot(lo, m)


def _mod_kernel(c_ref, w_ref, b_ref, o_ref):
    c = c_ref[...]
    s = c * _sigmoid(c)
    o_ref[0] = jnp.dot(s, w_ref[0], preferred_element_type=F32, precision=HIGHEST) + b_ref[0]


def _modulation(c_all, ada_w, ada_b):
    depth, d, n = ada_w.shape
    rows = c_all.shape[0]
    tn = 1536
    return pl.pallas_call(
        _mod_kernel,
        grid=(depth, n // tn),
        in_specs=[pl.BlockSpec((rows, d), lambda l, j: (0, 0)),
                  pl.BlockSpec((1, d, tn), lambda l, j: (l, 0, j)),
                  pl.BlockSpec((1, 1, tn), lambda l, j: (l, 0, j))],
        out_specs=pl.BlockSpec((1, rows, tn), lambda l, j: (l, 0, j)),
        out_shape=jax.ShapeDtypeStruct((depth, rows, n), F32),
        compiler_params=_params("arbitrary", "arbitrary"),
        name="modulation",
    )(c_all, ada_w, ada_b.reshape(depth, 1, n))


def _proj_kernel(x_ref, g_ref, sh_ref, sc_ref, w_ref, o_ref):
    h = _adaln(x_ref[0], g_ref[...], sh_ref[0], sc_ref[0]).astype(BF16)
    o_ref[0] = jnp.dot(h, w_ref[...], preferred_element_type=F32)


def _in_proj(x, g, sh, sc, w):
    b, t, d = x.shape
    n = w.shape[1]
    tt = min(TOKEN_TILE, t)
    return pl.pallas_call(
        _proj_kernel,
        grid=(b, t // tt),
        in_specs=[pl.BlockSpec((1, tt, d), lambda i, j: (i, j, 0)),
                  pl.BlockSpec((1, d), lambda i, j: (0, 0)),
                  pl.BlockSpec((1, 1, d), lambda i, j: (i, 0, 0)),
                  pl.BlockSpec((1, 1, d), lambda i, j: (i, 0, 0)),
                  pl.BlockSpec((d, n), lambda i, j: (0, 0))],
        out_specs=pl.BlockSpec((1, tt, n), lambda i, j: (i, j, 0)),
        out_shape=jax.ShapeDtypeStruct((b, t, n), F32),
        compiler_params=_params("arbitrary", "arbitrary"),
        name="in_proj",
    )(x, g, sh, sc, w)


def _terms_kernel(p_ref, pp_ref, pn_ref, mup_ref, mun_ref, w0_ref, w2_ref, a0_ref, a2_ref, g2_ref,
                  kk_ref, ka_ref, rk_ref, seg_ref, z_ref, bonus_ref, gate_ref, *, n_tiles):
    t = pl.program_id(1)
    p = p_ref[0]
    tt, aw = p.shape[0], kk_ref.shape[1]
    prev_row = jnp.where(t > 0, pp_ref[0, 7:8, :], 0.0)
    next_row = jnp.where(t < n_tiles - 1, pn_ref[0, 0:1, :], 0.0)
    rows = lax.broadcasted_iota(jnp.int32, p.shape, 0)
    prev = jnp.where(rows == 0, prev_row, pltpu.roll(p, 1, axis=0))
    nxt = jnp.where(rows == tt - 1, next_row, pltpu.roll(p, tt - 1, axis=0))
    s = p + mup_ref[...] * (prev - p) + mun_ref[...] * (nxt - p)
    r, k, v = s[:, :aw], s[:, aw:2 * aw], s[:, 2 * aw:3 * aw]
    lora = s[:, 3 * aw:]
    g_in, wd, ad = lora[:, 0:128], lora[:, 128:256], lora[:, 256:384]
    dotf = functools.partial(jnp.dot, preferred_element_type=F32, precision=HIGHEST)
    zw = -(w0_ref[...] + dotf(jnp.tanh(wd), w2_ref[...]))
    softplus = jnp.maximum(zw, 0.0) + jnp.log(1.0 + jnp.exp(-jnp.abs(zw)))
    decay = jnp.exp(-jnp.exp(-softplus - 0.5))
    a = _sigmoid(a0_ref[...] + dotf(ad, a2_ref[...]))
    seg = seg_ref[...]
    kk = k * kk_ref[...]
    kk = kk / jnp.maximum(jnp.sqrt(_split_dot(kk * kk, seg)), 1e-12)
    z_ref[0, 0] = kk
    z_ref[0, 1] = v
    z_ref[0, 2] = r
    kd_sum = jnp.zeros_like(k)
    for d in range(2):
        a_d = a[:, d * aw:(d + 1) * aw]
        k_dir = k * (1.0 + (a_d - 1.0) * ka_ref[...])
        z_ref[0, 3 + 3 * d] = decay[:, d * aw:(d + 1) * aw]
        z_ref[0, 4 + 3 * d] = k_dir
        z_ref[0, 5 + 3 * d] = kk * a_d
        kd_sum = kd_sum + k_dir
    bonus_ref[0] = _split_dot(r * kd_sum * rk_ref[...], seg) * v
    gate_ref[0] = jnp.dot(_sigmoid(g_in).astype(BF16), g2_ref[...], preferred_element_type=F32)


def _rwkv_terms(p, consts):
    b, t, _ = p.shape
    aw = consts["key_k"].shape[1]
    sw = 4 * aw
    tt = min(TOKEN_TILE, t)
    n_tiles = t // tt
    hb = tt // 8
    full = lambda a: pl.BlockSpec(a.shape, lambda i, j: (0,) * a.ndim)
    names = ("mu_prev", "mu_next", "w0", "w2", "a0", "a2", "g2", "key_k", "key_a", "r_k", "seg")
    cs = [consts[n] for n in names]
    out3 = jax.ShapeDtypeStruct((b, t, aw), F32)
    return pl.pallas_call(
        functools.partial(_terms_kernel, n_tiles=n_tiles),
        grid=(b, n_tiles),
        in_specs=[pl.BlockSpec((1, tt, sw), lambda i, j: (i, j, 0)),
                  pl.BlockSpec((1, 8, sw), lambda i, j: (i, jnp.maximum(j * hb - 1, 0), 0)),
                  pl.BlockSpec((1, 8, sw), lambda i, j: (i, jnp.minimum((j + 1) * hb, t // 8 - 1), 0))]
                 + [full(a) for a in cs],
        out_specs=[pl.BlockSpec((1, 9, tt, aw), lambda i, j: (i, 0, j, 0)),
                   pl.BlockSpec((1, tt, aw), lambda i, j: (i, j, 0)),
                   pl.BlockSpec((1, tt, aw), lambda i, j: (i, j, 0))],
        out_shape=[jax.ShapeDtypeStruct((b, 9, t, aw), F32), out3, out3],
        compiler_params=_params("arbitrary", "arbitrary"),
        name="rwkv_terms",
    )(p, p, p, *cs)


def _scan_kernel(zsf_ref, zdf_ref, zsb_ref, zdb_ref, yf_ref, yb_ref, s_ref, *, tb):
    @pl.when(pl.program_id(0) == 0)
    def _():
        s_ref[...] = jnp.zeros_like(s_ref)

    n = s_ref.shape[1]
    dirs = ((zsf_ref, zdf_ref, yf_ref), (zsb_ref, zdb_ref, yb_ref))

    def step(tf, carry):
        tidx = (tf, tb - 1 - tf)
        vecs = []
        for d, (zs, zd, _) in enumerate(dirs):
            ti = tidx[d]
            kk, r = zs[ti, 0], zs[ti, 2]
            w, kd, bb = zd[ti, 0], zd[ti, 1], zd[ti, 2]
            bbr = jnp.sum(bb * r, axis=0, keepdims=True)
            kr = jnp.sum(kd * r, axis=0, keepdims=True)
            vecs.append((kk, w * r, w, bb, kd, bbr, kr))

        def row(i, c):
            for d, (zs, _, y_ref) in enumerate(dirs):
                kk, wr, w, bb, kd, bbr, kr = vecs[d]
                ti = tidx[d]
                si = s_ref[d, i]
                sa = -jnp.sum(si * kk, axis=0, keepdims=True)
                y0 = jnp.sum(si * wr, axis=0, keepdims=True)
                vi = zs[ti, 1, pl.ds(i, 1), :]
                s_ref[d, i] = si * w + sa * bb + vi * kd
                y_ref[ti, pl.ds(i, 1), :] = y0 + sa * bbr + vi * kr
            return c

        lax.fori_loop(0, n, row, 0, unroll=2)
        return carry

    lax.fori_loop(0, tb, step, 0)


def _wkv_scan(z, n_ctx):
    t_all, _, n, lanes = z.shape
    tb = SCAN_BLOCK
    nc, nm = n_ctx // tb, (t_all - n_ctx) // tb
    fwd = lambda g: g
    bwd = lambda g: jnp.where(g < nc, nc - 1 - g, 2 * nc + nm - 1 - g)
    blk = (tb, 3, n, lanes)
    y_shape = jax.ShapeDtypeStruct((nm * tb, n, lanes), F32)
    return pl.pallas_call(
        functools.partial(_scan_kernel, tb=tb),
        grid=(nc + nm,),
        in_specs=[pl.BlockSpec(blk, lambda g: (fwd(g), 0, 0, 0)),
                  pl.BlockSpec(blk, lambda g: (fwd(g), 1, 0, 0)),
                  pl.BlockSpec(blk, lambda g: (bwd(g), 0, 0, 0)),
                  pl.BlockSpec(blk, lambda g: (bwd(g), 2, 0, 0))],
        out_specs=[pl.BlockSpec((tb, n, lanes), lambda g: (jnp.maximum(g - nc, 0), 0, 0)),
                   pl.BlockSpec((tb, n, lanes), lambda g: (jnp.minimum(nm - 1, nm - 1 + nc - g), 0, 0))],
        out_shape=[y_shape, y_shape],
        scratch_shapes=[pltpu.VMEM((2, n, n, lanes), F32)],
        compiler_params=_params("arbitrary"),
        name="wkv_scan",
    )(z, z, z, z)


def _na_kernel(q_ref, k_ref, v_ref, kc_ref, vc_ref, bias_ref, o_ref, *, rows):
    kh = NA_ROWS
    dn = (((1,), (1,)), ((), ()))

    def row(r, c):
        r_start = jnp.clip(r - kh // 2, 0, rows - kh)
        off = r - r_start
        q0 = pl.multiple_of(r * GRID_W, GRID_W)
        k0 = pl.multiple_of(r_start * GRID_W, GRID_W)
        for hh in range(2):
            ls = slice(hh * HEAD_DIM, (hh + 1) * HEAD_DIM)
            q = (q_ref[0, pl.ds(q0, GRID_W), ls] * (HEAD_DIM ** -0.5)).astype(BF16)
            kl = k_ref[0, pl.ds(k0, kh * GRID_W), ls].astype(BF16)
            vl = v_ref[0, pl.ds(k0, kh * GRID_W), ls].astype(BF16)
            kc = kc_ref[0, :, ls].astype(BF16)
            vc = vc_ref[0, :, ls].astype(BF16)
            s_loc = lax.dot_general(q, kl, dn, preferred_element_type=F32) + bias_ref[off, hh]
            s_ctx = lax.dot_general(q, kc, dn, preferred_element_type=F32)
            m = jnp.maximum(jnp.max(s_loc, axis=-1, keepdims=True), jnp.max(s_ctx, axis=-1, keepdims=True))
            e_loc = jnp.exp(s_loc - m)
            e_ctx = jnp.exp(s_ctx - m)
            den = jnp.sum(e_loc, axis=-1, keepdims=True) + jnp.sum(e_ctx, axis=-1, keepdims=True)
            o = (jnp.dot(e_loc.astype(BF16), vl, preferred_element_type=F32)
                 + jnp.dot(e_ctx.astype(BF16), vc, preferred_element_type=F32))
            o_ref[0, pl.ds(q0, GRID_W), ls] = (o / den).astype(o_ref.dtype)
        return c

    lax.fori_loop(0, rows, row, 0)


def _neighbourhood_attention(p, pc, bias, col_q, col_k, col_v):
    b, t, _ = p.shape
    l = pc.shape[1]
    rows = t // GRID_W
    n_pairs = bias.shape[1] // 2
    n_keys = NA_ROWS * GRID_W
    return pl.pallas_call(
        functools.partial(_na_kernel, rows=rows),
        grid=(b, n_pairs),
        in_specs=[pl.BlockSpec((1, t, 128), lambda i, h: (i, 0, col_q + h)),
                  pl.BlockSpec((1, t, 128), lambda i, h: (i, 0, col_k + h)),
                  pl.BlockSpec((1, t, 128), lambda i, h: (i, 0, col_v + h)),
                  pl.BlockSpec((1, l, 128), lambda i, h: (i, 0, col_k + h)),
                  pl.BlockSpec((1, l, 128), lambda i, h: (i, 0, col_v + h)),
                  pl.BlockSpec((NA_ROWS, 2, GRID_W, n_keys), lambda i, h: (0, h, 0, 0))],
        out_specs=pl.BlockSpec((1, t, 128), lambda i, h: (i, 0, h)),
        out_shape=jax.ShapeDtypeStruct((b, t, n_pairs * 128), BF16),
        compiler_params=_params("arbitrary", "arbitrary"),
        name="na_attention",
    )(p, p, p, pc, pc, bias)


def _na_bias_table(rpb):
    col = jnp.arange(GRID_W)
    c_start = jnp.clip(col - NA_COLS // 2, 0, GRID_W - NA_COLS)
    col_ok = (col[None, :] >= c_start[:, None]) & (col[None, :] < c_start[:, None] + NA_COLS)
    dc = jnp.clip(col[None, :] - col[:, None], 1 - NA_COLS, NA_COLS - 1) + NA_COLS - 1
    dr = jnp.arange(NA_ROWS)[None, :] - jnp.arange(NA_ROWS)[:, None] + NA_ROWS - 1
    tab = rpb[:, dr[:, :, None, None], dc[None, None]]
    tab = jnp.where(col_ok[None, None, None], tab, NEG_BIG)
    h = rpb.shape[0]
    return tab.transpose(1, 0, 3, 2, 4).reshape(NA_ROWS, h, GRID_W, NA_ROWS * GRID_W)


def _mix_out_kernel(y_ref, bonus_ref, gate_ref, ob_ref, x_ref, g1_ref, lg_ref, lb_ref, seg_ref, wa_ref, wb_ref, o_ref):
    y = y_ref[0]
    seg = seg_ref[...]
    inv = 1.0 / HEAD_DIM
    mu = _split_dot(y, seg) * inv
    yc = y - mu
    var = _split_dot(yc * yc, seg) * inv
    yn = (yc * lax.rsqrt(var + GN_EPS)) * lg_ref[...] + lb_ref[...]
    o_a = ((yn + bonus_ref[0]) * gate_ref[0]).astype(BF16)
    out = (jnp.dot(o_a, wa_ref[...], preferred_element_type=F32)
           + jnp.dot(ob_ref[0], wb_ref[...], preferred_element_type=F32))
    o_ref[0] = x_ref[0] + g1_ref[0] * out


def _mix_out(y, bonus, gate, o_b, x, g1, lnx_g, lnx_b, seg, w_a, w_b):
    b, t, d = x.shape
    aw = y.shape[2]
    tt = min(TOKEN_TILE, t)
    tok = lambda w: pl.BlockSpec((1, tt, w), lambda i, j: (i, j, 0))
    full = lambda a: pl.BlockSpec(a.shape, lambda i, j: (0,) * a.ndim)
    return pl.pallas_call(
        _mix_out_kernel,
        grid=(b, t // tt),
        in_specs=[tok(aw), tok(aw), tok(aw), tok(o_b.shape[2]), tok(d),
                  pl.BlockSpec((1, 1, d), lambda i, j: (i, 0, 0)),
                  full(lnx_g), full(lnx_b), full(seg), full(w_a), full(w_b)],
        out_specs=tok(d),
        out_shape=jax.ShapeDtypeStruct((b, t, d), F32),
        compiler_params=_params("arbitrary", "arbitrary"),
        name="mix_out",
    )(y, bonus, gate, o_b, x, g1, lnx_g, lnx_b, seg, w_a, w_b)


def _glu_kernel(x_ref, g_ref, sh_ref, sc_ref, w_ref, b_ref, o_ref):
    h = _adaln(x_ref[0], g_ref[...], sh_ref[0], sc_ref[0]).astype(BF16)
    u = jnp.dot(h, w_ref[...], preferred_element_type=F32) + b_ref[...]
    d = u.shape[1] // 2
    o_ref[0] = u[:, :d] * _sigmoid(u[:, d:])


def _glu_proj(x, g, sh, sc, w, bias):
    b, t, d = x.shape
    n = w.shape[1]
    tt = min(TOKEN_TILE, t)
    return pl.pallas_call(
        _glu_kernel,
        grid=(b, t // tt),
        in_specs=[pl.BlockSpec((1, tt, d), lambda i, j: (i, j, 0)),
                  pl.BlockSpec((1, d), lambda i, j: (0, 0)),
                  pl.BlockSpec((1, 1, d), lambda i, j: (i, 0, 0)),
                  pl.BlockSpec((1, 1, d), lambda i, j: (i, 0, 0)),
                  pl.BlockSpec((d, n), lambda i, j: (0, 0)),
                  pl.BlockSpec((1, n), lambda i, j: (0, 0))],
        out_specs=pl.BlockSpec((1, tt, n // 2), lambda i, j: (i, j, 0)),
        out_shape=jax.ShapeDtypeStruct((b, t, n // 2), F32),
        compiler_params=_params("arbitrary", "arbitrary"),
        name="glu_proj",
    )(x, g, sh, sc, w, bias)


def _conv_kernel(u_ref, up_ref, un_ref, dw_ref, dwb_ref, lg_ref, lb_ref, w2_ref, b2_ref, x_ref, g1_ref, o_ref,
                 win_ref, acc_ref, *, n_tiles):
    t = pl.program_id(1)
    tt, d = u_ref.shape[1], u_ref.shape[2]
    half = CONV_WIDTH // 2
    win_ref[0:HALO, :] = jnp.where(t > 0, up_ref[0], 0.0)
    win_ref[HALO:HALO + tt, :] = u_ref[0]
    win_ref[HALO + tt:2 * HALO + tt, :] = jnp.where(t < n_tiles - 1, un_ref[0], 0.0)
    rc = 32

    for base in range(0, tt, rc):
        for lc in range(d // 128):
            ls = slice(lc * 128, (lc + 1) * 128)
            acc = jnp.zeros((rc, 128), F32)
            for k in range(CONV_WIDTH):
                lo = base + HALO - half + k
                acc = acc + dw_ref[k:k + 1, ls] * win_ref[lo:lo + rc, ls]
            acc_ref[base:base + rc, ls] = acc
    u = acc_ref[...] + dwb_ref[...]
    mu = jnp.mean(u, axis=-1, keepdims=True)
    uc = u - mu
    var = jnp.mean(uc * uc, axis=-1, keepdims=True)
    un = (uc * lax.rsqrt(var + LN_EPS)) * lg_ref[...] + lb_ref[...]
    act = (un * _sigmoid(un)).astype(BF16)
    out = jnp.dot(act, w2_ref[...], preferred_element_type=F32) + b2_ref[...]
    o_ref[0] = x_ref[0] + g1_ref[0] * out


def _conv_module(u, dw, dwb, ln_g, ln_b, w2, b2, x, g1):
    b, t, d = x.shape
    tt = min(TOKEN_TILE, t)
    n_tiles = t // tt
    hb = tt // HALO
    tok = pl.BlockSpec((1, tt, d), lambda i, j: (i, j, 0))
    full = lambda a: pl.BlockSpec(a.shape, lambda i, j: (0,) * a.ndim)
    return pl.pallas_call(
        functools.partial(_conv_kernel, n_tiles=n_tiles),
        grid=(b, n_tiles),
        in_specs=[tok,
                  pl.BlockSpec((1, HALO, d), lambda i, j: (i, jnp.maximum(j * hb - 1, 0), 0)),
                  pl.BlockSpec((1, HALO, d), lambda i, j: (i, jnp.minimum((j + 1) * hb, t // HALO - 1), 0)),
                  full(dw), full(dwb), full(ln_g), full(ln_b), full(w2), full(b2), tok,
                  pl.BlockSpec((1, 1, d), lambda i, j: (i, 0, 0))],
        out_specs=tok,
        out_shape=jax.ShapeDtypeStruct((b, t, d), F32),
        scratch_shapes=[pltpu.VMEM((tt + 2 * HALO, d), F32), pltpu.VMEM((tt, d), F32)],
        compiler_params=_params("arbitrary", "arbitrary"),
        name="conv_module",
    )(u, u, u, dw, dwb, ln_g, ln_b, w2, b2, x, g1)


def _route_kernel(x_ref, g_ref, sh_ref, sc_ref, wr_ref, br_ref, hp_ref, route_ref, cnt_ref, carry_ref):
    @pl.when((pl.program_id(0) == 0) & (pl.program_id(1) == 0))
    def _():
        carry_ref[...] = jnp.zeros_like(carry_ref)

    h = _adaln(x_ref[0], g_ref[...], sh_ref[0], sc_ref[0])
    tt, d = h.shape
    hi = lax.bitcast_convert_type(h[:, :d // 2].astype(BF16).astype(F32), jnp.uint32)
    lo = lax.bitcast_convert_type(h[:, d // 2:].astype(BF16).astype(F32), jnp.uint32)
    hp_ref[0] = (hi & jnp.uint32(0xFFFF0000)) | (lo >> 16)

    logits = jnp.dot(h, wr_ref[...], preferred_element_type=F32, precision=HIGHEST) + br_ref[...]
    ne = logits.shape[1]
    lane = lax.broadcasted_iota(jnp.int32, (tt, ne), 1).astype(F32)
    work = logits
    mask = jnp.zeros((tt, ne), F32)
    picks, es = [], []
    den = jnp.zeros((tt, 1), F32)
    for k in range(TOP_K):
        m = jnp.max(work, axis=-1, keepdims=True)
        idx = jnp.min(jnp.where(work == m, lane, float(ne)), axis=-1, keepdims=True)
        pick = lane == idx
        if k == 0:
            top = m
        e = jnp.exp(m - top)
        den = den + e
        picks.append((pick, idx))
        es.append(e)
        mask = jnp.where(pick, 1.0, mask)
        work = jnp.where(pick, -jnp.inf, work)

    ri = lax.broadcasted_iota(jnp.int32, (tt, tt), 0)
    ci = lax.broadcasted_iota(jnp.int32, (tt, tt), 1)
    lower = jnp.where(ci < ri, 1.0, 0.0).astype(BF16)
    rank = jnp.dot(lower, mask.astype(BF16), preferred_element_type=F32) + carry_ref[...]
    carry_ref[...] = carry_ref[...] + jnp.sum(mask, axis=0, keepdims=True)
    cnt_ref[...] = carry_ref[...]

    out_lane = lax.broadcasted_iota(jnp.int32, (tt, 128), 1)
    route = jnp.zeros((tt, 128), F32)
    for k in range(TOP_K):
        pick, idx = picks[k]
        rk = jnp.sum(jnp.where(pick, rank, 0.0), axis=-1, keepdims=True)
        route = jnp.where(out_lane == k, idx, route)
        route = jnp.where(out_lane == TOP_K + k, rk, route)
        route = jnp.where(out_lane == 2 * TOP_K + k, es[k] / den, route)
    route_ref[0] = route


def _route(x, g, sh, sc, w_r, b_r):
    b, t, d = x.shape
    ne = w_r.shape[1]
    tt = min(TOKEN_TILE, t)
    return pl.pallas_call(
        _route_kernel,
        grid=(b, t // tt),
        in_specs=[pl.BlockSpec((1, tt, d), lambda i, j: (i, j, 0)),
                  pl.BlockSpec((1, d), lambda i, j: (0, 0)),
                  pl.BlockSpec((1, 1, d), lambda i, j: (i, 0, 0)),
                  pl.BlockSpec((1, 1, d), lambda i, j: (i, 0, 0)),
                  pl.BlockSpec((d, ne), lambda i, j: (0, 0)),
                  pl.BlockSpec((1, ne), lambda i, j: (0, 0))],
        out_specs=[pl.BlockSpec((1, tt, d // 2), lambda i, j: (i, j, 0)),
                   pl.BlockSpec((1, tt, 128), lambda i, j: (i, j, 0)),
                   pl.BlockSpec((1, ne), lambda i, j: (0, 0))],
        out_shape=[jax.ShapeDtypeStruct((b, t, d // 2), jnp.uint32),
                   jax.ShapeDtypeStruct((b, t, 128), F32),
                   jax.ShapeDtypeStruct((1, ne), F32)],
        scratch_shapes=[pltpu.VMEM((1, ne), F32)],
        compiler_params=_params("arbitrary", "arbitrary"),
        name="moe_route",
    )(x, g, sh, sc, w_r, b_r)


def _dispatch_kernel(pos_hbm, hp_ref, xs_in, xs_out, idx_ref, sem_idx, sem_rows, *, tile):
    del xs_in
    i = pl.program_id(0)
    n_idx = tile * TOP_K
    cp = pltpu.make_async_copy(pos_hbm.at[pl.ds(pl.multiple_of(i * n_idx, n_idx), n_idx)], idx_ref, sem_idx)
    cp.start()
    cp.wait()

    def row_copy(j, k):
        return pltpu.make_async_copy(hp_ref.at[pl.ds(j, 1)], xs_out.at[pl.ds(idx_ref[j * TOP_K + k], 1)], sem_rows)

    def issue(j, c):
        for k in range(TOP_K):
            row_copy(j, k).start()
        return c

    lax.fori_loop(0, tile, issue, 0)
    for k in range(TOP_K):
        pltpu.make_async_copy(hp_ref, xs_out.at[pl.ds(0, tile)], sem_rows).wait()


def _dispatch(pos_flat, hp, n_rows):
    n, w = hp.shape
    tile = TOKEN_TILE
    xs0 = jnp.zeros((n_rows, w), jnp.uint32)
    return pl.pallas_call(
        functools.partial(_dispatch_kernel, tile=tile),
        grid=(n // tile,),
        in_specs=[pl.BlockSpec(memory_space=pl.ANY),
                  pl.BlockSpec((tile, w), lambda i: (i, 0)),
                  pl.BlockSpec(memory_space=pl.ANY)],
        out_specs=pl.BlockSpec(memory_space=pl.ANY),
        out_shape=jax.ShapeDtypeStruct((n_rows, w), jnp.uint32),
        scratch_shapes=[pltpu.SMEM((tile * TOP_K,), jnp.int32),
                        pltpu.SemaphoreType.DMA, pltpu.SemaphoreType.DMA],
        input_output_aliases={2: 0},
        compiler_params=_params("arbitrary"),
        name="moe_dispatch",
    )(pos_flat, hp, xs0)


def _expert_kernel(be_ref, nb_ref, xs_ref, w1g_ref, w1l_ref, b1g_ref, b1l_ref, w2_ref, b2_ref, ys_ref):
    del be_ref

    @pl.when(pl.program_id(0) < nb_ref[0])
    def _():
        u = xs_ref[...]
        half = u.shape[1]
        xa = lax.bitcast_convert_type(u & jnp.uint32(0xFFFF0000), F32).astype(BF16)
        xb = lax.bitcast_convert_type(u << 16, F32).astype(BF16)
        dot = functools.partial(jnp.dot, preferred_element_type=F32)
        ug = dot(xa, w1g_ref[0, :half, :]) + dot(xb, w1g_ref[0, half:, :]) + b1g_ref[0]
        ul = dot(xa, w1l_ref[0, :half, :]) + dot(xb, w1l_ref[0, half:, :]) + b1l_ref[0]
        glu = jnp.minimum(ug, SWIGLU_LIMIT)
        lin = jnp.clip(ul, -SWIGLU_LIMIT, SWIGLU_LIMIT)
        act = (glu * _sigmoid(SWIGLU_ALPHA * glu)) * (lin + 1.0)
        ys_ref[...] = dot(act.astype(BF16), w2_ref[0]) + b2_ref[0]

    @pl.when(pl.program_id(0) >= nb_ref[0])
    def _():
        ys_ref[...] = jnp.zeros_like(ys_ref)


def _expert_ffn(block_e, n_used, xs, w1g, w1l, b1g, b1l, w2, b2):
    n_rows, half = xs.shape
    ne, d, f = w1g.shape
    bm = EXPERT_ROWS
    n_blocks = n_rows // bm
    wspec = lambda s: pl.BlockSpec((1,) + s, lambda i, be, nb: (be[i], 0, 0))
    return pl.pallas_call(
        _expert_kernel,
        grid_spec=pltpu.PrefetchScalarGridSpec(
            num_scalar_prefetch=2,
            grid=(n_blocks,),
            in_specs=[pl.BlockSpec((bm, half), lambda i, be, nb: (i, 0)),
                      wspec((d, f)), wspec((d, f)), wspec((1, f)), wspec((1, f)), wspec((f, d)), wspec((1, d))],
            out_specs=pl.BlockSpec((bm, d), lambda i, be, nb: (i, 0)),
        ),
        out_shape=jax.ShapeDtypeStruct((n_rows, d), F32),
        compiler_params=_params("arbitrary"),
        name="moe_experts",
    )(block_e, n_used, xs, w1g, w1l, b1g, b1l, w2, b2)


def _combine_kernel(pos_hbm, ys_hbm, x_ref, route_ref, g2_ref, fg_ref, o_ref, buf_ref, idx_ref, sem_idx, sem_rows,
                    *, tile, final):
    i = pl.program_id(0)
    n_idx = tile * TOP_K
    cp = pltpu.make_async_copy(pos_hbm.at[pl.ds(pl.multiple_of(i * n_idx, n_idx), n_idx)], idx_ref, sem_idx)
    cp.start()
    cp.wait()

    def issue(j, c):
        for k in range(TOP_K):
            pltpu.make_async_copy(ys_hbm.at[pl.ds(idx_ref[j * TOP_K + k], 1)], buf_ref.at[k, pl.ds(j, 1)],
                                  sem_rows).start()
        return c

    lax.fori_loop(0, tile, issue, 0)
    for k in range(TOP_K):
        pltpu.make_async_copy(ys_hbm.at[pl.ds(0, tile)], buf_ref.at[k], sem_rows).wait()

    route = route_ref[...]
    acc = jnp.zeros(x_ref.shape, F32)
    for k in range(TOP_K):
        acc = acc + buf_ref[k] * route[:, 2 * TOP_K + k:2 * TOP_K + k + 1]
    x = x_ref[...] + g2_ref[0] * acc
    if final:
        x = (x * lax.rsqrt(jnp.mean(x * x, axis=-1, keepdims=True) + RMS_EPS)) * fg_ref[...]
    o_ref[...] = x


def _combine(pos_flat, ys, x2, route2, g2, final_g, tiles_per_batch, final):
    n, d = x2.shape
    tile = TOKEN_TILE
    return pl.pallas_call(
        functools.partial(_combine_kernel, tile=tile, final=final),
        grid=(n // tile,),
        in_specs=[pl.BlockSpec(memory_space=pl.ANY),
                  pl.BlockSpec(memory_space=pl.ANY),
                  pl.BlockSpec((tile, d), lambda i: (i, 0)),
                  pl.BlockSpec((tile, 128), lambda i: (i, 0)),
                  pl.BlockSpec((1, 1, d), lambda i: (i // tiles_per_batch, 0, 0)),
                  pl.BlockSpec((1, d), lambda i: (0, 0))],
        out_specs=pl.BlockSpec((tile, d), lambda i: (i, 0)),
        out_shape=jax.ShapeDtypeStruct((n, d), F32),
        scratch_shapes=[pltpu.VMEM((TOP_K, tile, d), F32),
                        pltpu.SMEM((tile * TOP_K,), jnp.int32),
                        pltpu.SemaphoreType.DMA, pltpu.SemaphoreType.DMA],
        compiler_params=_params("arbitrary"),
        name="moe_combine",
    )(pos_flat, ys, x2, route2, g2, final_g)


def _moe_layer(x, g, sh, sc, gate2, w_r, b_r, w1, b1, w2, b2, final_g, final):
    b, t, d = x.shape
    n = b * t
    ne = w_r.shape[1]
    bm = EXPERT_ROWS
    hp, route, counts = _route(x, g, sh, sc, w_r, b_r.reshape(1, ne))
    route2 = route.reshape(n, 128)

    counts = counts[0].astype(jnp.int32)
    padded = (counts + bm - 1) // bm * bm
    pad_end = jnp.cumsum(padded)
    pad_start = pad_end - padded
    n_blocks = -(-(n * TOP_K + ne * (bm - 1)) // bm)
    e_idx = route2[:, :TOP_K].astype(jnp.int32)
    pos = pad_start[e_idx] + route2[:, TOP_K:2 * TOP_K].astype(jnp.int32)
    pos_flat = pos.reshape(-1)
    block_e = jnp.minimum(jnp.searchsorted(pad_end, jnp.arange(n_blocks, dtype=jnp.int32) * bm, side="right"),
                          ne - 1).astype(jnp.int32)
    n_used = (pad_end[-1:] // bm).astype(jnp.int32)

    xs = _dispatch(pos_flat, hp.reshape(n, d // 2), n_blocks * bm)
    f = w2.shape[1]
    w1g = w1[:, :, 0::2].astype(BF16)
    w1l = w1[:, :, 1::2].astype(BF16)
    b1g = b1[:, 0::2].reshape(ne, 1, f)
    b1l = b1[:, 1::2].reshape(ne, 1, f)
    ys = _expert_ffn(block_e, n_used, xs, w1g, w1l, b1g, b1l, w2.astype(BF16), b2.reshape(ne, 1, d))
    out = _combine(pos_flat, ys, x.reshape(n, d), route2, gate2, final_g.reshape(1, d), t // TOKEN_TILE, final)
    return out.reshape(b, t, d)


def _even_layer(x, ctx, mod, mod_c, norm_g, w_in, mu_prev, mu_next, w0, w2, a0, a2, g2, key_k, key_a, r_k,
                lnx_g, lnx_b, rpb, w_out):
    b, t, d = x.shape
    l = ctx.shape[1]
    aw = key_k.shape[0]
    dl = w2.shape[1]
    bw = (w_in.shape[1] - 3 * aw - 128 - 4 * dl) // 3
    n_heads = aw // HEAD_DIM
    sh1, sc1, g1 = (mod[:, None, i * d:(i + 1) * d] for i in range(3))
    shc = jnp.broadcast_to(mod_c[None, None, :d], (b, 1, d))
    scc = jnp.broadcast_to(mod_c[None, None, d:2 * d], (b, 1, d))

    c_ra, c_gd = bw, bw + aw
    c_ka = c_gd + 128
    c_va = c_ka + aw
    c_wd = c_va + aw
    c_ad = c_wd + 2 * dl
    c_kb = c_ad + 2 * dl
    c_vb = c_kb + bw
    cols = lambda a, lo, hi: a[..., lo:hi]
    pad = jnp.zeros((d, 128), F32)
    w_p = jnp.concatenate([cols(w_in, c_ra, c_gd), cols(w_in, c_ka, c_va), cols(w_in, c_va, c_wd),
                           cols(w_in, c_gd, c_ka), cols(w_in, c_wd, c_ad), cols(w_in, c_ad, c_kb), pad,
                           cols(w_in, 0, c_ra), cols(w_in, c_kb, c_vb), cols(w_in, c_vb, c_vb + bw)],
                          axis=1).astype(BF16)

    def shift_vec(mu):
        o = lambda c: c - c_ra
        return jnp.concatenate([mu[o(c_ra):o(c_gd)], mu[o(c_ka):o(c_va)], mu[o(c_va):o(c_wd)], mu[o(c_gd):o(c_ka)],
                                mu[o(c_wd):o(c_ad)], mu[o(c_ad):o(c_kb)], jnp.zeros((128,), F32)]).reshape(1, -1)

    blockdiag = lambda m: jnp.concatenate(
        [jnp.concatenate([m[0], jnp.zeros_like(m[0])], axis=1),
         jnp.concatenate([jnp.zeros_like(m[1]), m[1]], axis=1)], axis=0)
    head = jnp.arange(aw) // HEAD_DIM
    consts = {
        "mu_prev": shift_vec(mu_prev), "mu_next": shift_vec(mu_next),
        "w0": w0.reshape(1, 2 * aw), "w2": blockdiag(w2), "a0": a0.reshape(1, 2 * aw), "a2": blockdiag(a2),
        "g2": g2.astype(BF16), "key_k": key_k.reshape(1, aw), "key_a": key_a.reshape(1, aw),
        "r_k": r_k.reshape(1, aw), "seg": (head[:, None] == head[None, :]).astype(BF16),
    }

    g_row = norm_g.reshape(1, d)
    p = _in_proj(x, g_row, sh1, sc1, w_p)
    pc = _in_proj(ctx, g_row, shc, scc, w_p)
    z_m, bonus, gate = _rwkv_terms(p, consts)
    z_c, _, _ = _rwkv_terms(pc, consts)

    z = jnp.concatenate([z_c, z_m], axis=2)
    z = z.reshape(b, 9, l + t, n_heads, HEAD_DIM).transpose(2, 1, 4, 0, 3).reshape(l + t, 9, HEAD_DIM, b * n_heads)
    yf, yb = _wkv_scan(z, l)
    y = (yf + yb).reshape(t, HEAD_DIM, b, n_heads).transpose(2, 0, 3, 1).reshape(b, t, aw)

    qb = (4 * aw) // 128
    o_b = _neighbourhood_attention(p, pc, _na_bias_table(rpb), qb, qb + bw // 128, qb + 2 * bw // 128)
    return _mix_out(y, bonus, gate, o_b, x, g1, lnx_g.reshape(1, aw), lnx_b.reshape(1, aw), consts["seg"],
                    w_out[:aw].astype(BF16), w_out[aw:].astype(BF16))


def _odd_layer(x, mod, norm_g, pw1_w, pw1_b, dw_w, dw_b, ln_g, ln_b, pw2_w, pw2_b):
    b, t, d = x.shape
    sh1, sc1, g1 = (mod[:, None, i * d:(i + 1) * d] for i in range(3))
    u = _glu_proj(x, norm_g.reshape(1, d), sh1, sc1, pw1_w.astype(BF16), pw1_b.reshape(1, -1))
    dw = jnp.concatenate([dw_w, jnp.zeros((1, d), F32)], axis=0)
    return _conv_module(u, dw, dw_b.reshape(1, d), ln_g.reshape(1, d), ln_b.reshape(1, d), pw2_w.astype(BF16),
                        pw2_b.reshape(1, d), x, g1)


def kernel(x, c, ctx, c_ctx, ada_w, ada_b, norm_mix_g, norm_ffn_g, final_norm_g, mix_w_in, shift_mu_prev, shift_mu_next, decay_w0, decay_w2, iclr_a0, iclr_a2, gate_g2, key_k, key_a, bonus_r_k, lnx_g, lnx_b, na_rpb, mix_w_out, conv_pw1_w, conv_pw1_b, conv_dw_w, conv_dw_b, conv_ln_g, conv_ln_b, conv_pw2_w, conv_pw2_b, router_w, router_b, expert_w1, expert_b1, expert_w2, expert_b2):
    b, t, d = x.shape
    depth = ada_w.shape[0]
    rows = -(-(b + 1) // 8) * 8
    c_all = jnp.concatenate([c, c_ctx[None, :], jnp.zeros((rows - b - 1, d), F32)], axis=0)
    mod_all = _modulation(c_all, ada_w, ada_b)
    for l in range(depth):
        mod = mod_all[l, :b]
        i = l // 2
        if l % 2 == 0:
            x = _even_layer(x, ctx, mod, mod_all[l, b], norm_mix_g[l], mix_w_in[i], shift_mu_prev[i], shift_mu_next[i],
                            decay_w0[i], decay_w2[i], iclr_a0[i], iclr_a2[i], gate_g2[i], key_k[i], key_a[i],
                            bonus_r_k[i].reshape(-1), lnx_g[i], lnx_b[i], na_rpb[i], mix_w_out[i])
        else:
            x = _odd_layer(x, mod, norm_mix_g[l], conv_pw1_w[i], conv_pw1_b[i], conv_dw_w[i], conv_dw_b[i],
                           conv_ln_g[i], conv_ln_b[i], conv_pw2_w[i], conv_pw2_b[i])
        sh2, sc2, g2 = (mod[:, None, j * d:(j + 1) * d] for j in range(3, 6))
        x = _moe_layer(x, norm_ffn_g[l].reshape(1, d), sh2, sc2, g2, router_w[l], router_b[l], expert_w1[l],
                       expert_b1[l], expert_w2[l], expert_b2[l], final_norm_g, final=(l == depth - 1))
    return x
```

```python
import functools

import jax
import jax.numpy as jnp
import numpy as np
from jax import lax
from jax.experimental import pallas as pl
from jax.experimental.pallas import tpu as pltpu

F32 = jnp.float32
BF16 = jnp.bfloat16
HIGHEST = lax.Precision.HIGHEST

HEAD_DIM = 64
GRID_W = 64
NA_ROWS = 8
NA_COLS = 16
CONV_WIDTH = 31
N_EXPERTS = 32
TOP_K = 4
SWIGLU_ALPHA = 1.702
SWIGLU_LIMIT = 7.0
RMS_EPS = 1e-6
LN_EPS = 1e-5
GN_EPS = 64e-5
NEG_BIG = -1e30

VMEM_LIMIT_BYTES = 52 * 1024 * 1024
TOKEN_TILE = 256
SCAN_BLOCK = 16
EXPERT_ROWS = 256
HALO = 16


def _params(*sem):
    return pltpu.CompilerParams(dimension_semantics=sem, vmem_limit_bytes=VMEM_LIMIT_BYTES)


def _adaln(x, g, sh, sc):
    y = x * lax.rsqrt(jnp.mean(x * x, axis=-1, keepdims=True) + RMS_EPS)
    return (y * g) * (1.0 + sc) + sh


def _sigmoid(x):
    return 1.0 / (1.0 + jnp.exp(-x))


def _split_dot(x, m):
    hi = x.astype(BF16)
    r1 = x - hi.astype(F32)
    mid = r1.astype(BF16)
    lo = (r1 - mid.astype(F32)).astype(BF16)
    dot = functools.partial(jnp.dot, preferred_element_type=F32)
    return dot(hi, m) + dot(mid, m) + dot(lo, m)


def _mod_kernel(c_ref, w_ref, b_ref, o_ref):
    c = c_ref[...]
    s = c * _sigmoid(c)
    o_ref[0] = jnp.dot(s, w_ref[0], preferred_element_type=F32, precision=HIGHEST) + b_ref[0]


def _modulation(c_all, ada_w, ada_b):
    depth, d, n = ada_w.shape
    rows = c_all.shape[0]
    tn = 1536
    return pl.pallas_call(
        _mod_kernel,
        grid=(depth, n // tn),
        in_specs=[pl.BlockSpec((rows, d), lambda l, j: (0, 0)),
                  pl.BlockSpec((1, d, tn), lambda l, j: (l, 0, j)),
                  pl.BlockSpec((1, 1, tn), lambda l, j: (l, 0, j))],
        out_specs=pl.BlockSpec((1, rows, tn), lambda l, j: (l, 0, j)),
        out_shape=jax.ShapeDtypeStruct((depth, rows, n), F32),
        compiler_params=_params("arbitrary", "arbitrary"),
        name="modulation",
    )(c_all, ada_w, ada_b.reshape(depth, 1, n))


def _proj_kernel(x_ref, g_ref, sh_ref, sc_ref, w_ref, o_ref):
    h = _adaln(x_ref[0], g_ref[...], sh_ref[0], sc_ref[0]).astype(BF16)
    o_ref[0] = jnp.dot(h, w_ref[...], preferred_element_type=F32)


def _in_proj(x, g, sh, sc, w):
    b, t, d = x.shape
    n = w.shape[1]
    tt = min(TOKEN_TILE, t)
    return pl.pallas_call(
        _proj_kernel,
        grid=(b, t // tt),
        in_specs=[pl.BlockSpec((1, tt, d), lambda i, j: (i, j, 0)),
                  pl.BlockSpec((1, d), lambda i, j: (0, 0)),
                  pl.BlockSpec((1, 1, d), lambda i, j: (i, 0, 0)),
                  pl.BlockSpec((1, 1, d), lambda i, j: (i, 0, 0)),
                  pl.BlockSpec((d, n), lambda i, j: (0, 0))],
        out_specs=pl.BlockSpec((1, tt, n), lambda i, j: (i, j, 0)),
        out_shape=jax.ShapeDtypeStruct((b, t, n), F32),
        compiler_params=_params("arbitrary", "arbitrary"),
        name="in_proj",
    )(x, g, sh, sc, w)


def _terms_kernel(p_ref, pp_ref, pn_ref, mup_ref, mun_ref, w0_ref, w2_ref, a0_ref, a2_ref, g2_ref,
                  kk_ref, ka_ref, rk_ref, seg_ref, z_ref, bonus_ref, gate_ref, *, n_tiles):
    t = pl.program_id(1)
    p = p_ref[0]
    tt, aw = p.shape[0], kk_ref.shape[1]
    prev_row = jnp.where(t > 0, pp_ref[0, 7:8, :], 0.0)
    next_row = jnp.where(t < n_tiles - 1, pn_ref[0, 0:1, :], 0.0)
    rows = lax.broadcasted_iota(jnp.int32, p.shape, 0)
    prev = jnp.where(rows == 0, prev_row, pltpu.roll(p, 1, axis=0))
    nxt = jnp.where(rows == tt - 1, next_row, pltpu.roll(p, tt - 1, axis=0))
    s = p + mup_ref[...] * (prev - p) + mun_ref[...] * (nxt - p)
    r, k, v = s[:, :aw], s[:, aw:2 * aw], s[:, 2 * aw:3 * aw]
    lora = s[:, 3 * aw:]
    g_in, wd, ad = lora[:, 0:128], lora[:, 128:256], lora[:, 256:384]
    dotf = functools.partial(jnp.dot, preferred_element_type=F32, precision=HIGHEST)
    zw = -(w0_ref[...] + dotf(jnp.tanh(wd), w2_ref[...]))
    softplus = jnp.maximum(zw, 0.0) + jnp.log(1.0 + jnp.exp(-jnp.abs(zw)))
    decay = jnp.exp(-jnp.exp(-softplus - 0.5))
    a = _sigmoid(a0_ref[...] + dotf(ad, a2_ref[...]))
    seg = seg_ref[...]
    kk = k * kk_ref[...]
    kk = kk / jnp.maximum(jnp.sqrt(_split_dot(kk * kk, seg)), 1e-12)
    z_ref[0, 0] = kk
    z_ref[0, 1] = v
    z_ref[0, 2] = r
    kd_sum = jnp.zeros_like(k)
    for d in range(2):
        a_d = a[:, d * aw:(d + 1) * aw]
        k_dir = k * (1.0 + (a_d - 1.0) * ka_ref[...])
        z_ref[0, 3 + 3 * d] = decay[:, d * aw:(d + 1) * aw]
        z_ref[0, 4 + 3 * d] = k_dir
        z_ref[0, 5 + 3 * d] = kk * a_d
        kd_sum = kd_sum + k_dir
    bonus_ref[0] = _split_dot(r * kd_sum * rk_ref[...], seg) * v
    gate_ref[0] = jnp.dot(_sigmoid(g_in).astype(BF16), g2_ref[...], preferred_element_type=F32)


def _rwkv_terms(p, consts):
    b, t, _ = p.shape
    aw = consts["key_k"].shape[1]
    sw = 4 * aw
    tt = min(TOKEN_TILE, t)
    n_tiles = t // tt
    hb = tt // 8
    full = lambda a: pl.BlockSpec(a.shape, lambda i, j: (0,) * a.ndim)
    names = ("mu_prev", "mu_next", "w0", "w2", "a0", "a2", "g2", "key_k", "key_a", "r_k", "seg")
    cs = [consts[n] for n in names]
    out3 = jax.ShapeDtypeStruct((b, t, aw), F32)
    return pl.pallas_call(
        functools.partial(_terms_kernel, n_tiles=n_tiles),
        grid=(b, n_tiles),
        in_specs=[pl.BlockSpec((1, tt, sw), lambda i, j: (i, j, 0)),
                  pl.BlockSpec((1, 8, sw), lambda i, j: (i, jnp.maximum(j * hb - 1, 0), 0)),
                  pl.BlockSpec((1, 8, sw), lambda i, j: (i, jnp.minimum((j + 1) * hb, t // 8 - 1), 0))]
                 + [full(a) for a in cs],
        out_specs=[pl.BlockSpec((1, 9, tt, aw), lambda i, j: (i, 0, j, 0)),
                   pl.BlockSpec((1, tt, aw), lambda i, j: (i, j, 0)),
                   pl.BlockSpec((1, tt, aw), lambda i, j: (i, j, 0))],
        out_shape=[jax.ShapeDtypeStruct((b, 9, t, aw), F32), out3, out3],
        compiler_params=_params("arbitrary", "arbitrary"),
        name="rwkv_terms",
    )(p, p, p, *cs)


def _scan_kernel(zsf_ref, zdf_ref, zsb_ref, zdb_ref, yf_ref, yb_ref, s_ref, *, tb):
    @pl.when(pl.program_id(0) == 0)
    def _():
        s_ref[...] = jnp.zeros_like(s_ref)

    n = s_ref.shape[1]
    dirs = ((zsf_ref, zdf_ref, yf_ref), (zsb_ref, zdb_ref, yb_ref))

    def step(tf, carry):
        tidx = (tf, tb - 1 - tf)
        vecs = []
        for d, (zs, zd, _) in enumerate(dirs):
            ti = tidx[d]
            kk, r = zs[ti, 0], zs[ti, 2]
            w, kd, bb = zd[ti, 0], zd[ti, 1], zd[ti, 2]
            bbr = jnp.sum(bb * r, axis=0, keepdims=True)
            kr = jnp.sum(kd * r, axis=0, keepdims=True)
            vecs.append((kk, w * r, w, bb, kd, bbr, kr))

        def row(i, c):
            for d, (zs, _, y_ref) in enumerate(dirs):
                kk, wr, w, bb, kd, bbr, kr = vecs[d]
                ti = tidx[d]
                si = s_ref[d, i]
                sa = -jnp.sum(si * kk, axis=0, keepdims=True)
                y0 = jnp.sum(si * wr, axis=0, keepdims=True)
                vi = zs[ti, 1, pl.ds(i, 1), :]
                s_ref[d, i] = si * w + sa * bb + vi * kd
                y_ref[ti, pl.ds(i, 1), :] = y0 + sa * bbr + vi * kr
            return c

        lax.fori_loop(0, n, row, 0, unroll=2)
        return carry

    lax.fori_loop(0, tb, step, 0)


def _wkv_scan(z, n_ctx):
    t_all, _, n, lanes = z.shape
    tb = SCAN_BLOCK
    nc, nm = n_ctx // tb, (t_all - n_ctx) // tb
    fwd = lambda g: g
    bwd = lambda g: jnp.where(g < nc, nc - 1 - g, 2 * nc + nm - 1 - g)
    blk = (tb, 3, n, lanes)
    y_shape = jax.ShapeDtypeStruct((nm * tb, n, lanes), F32)
    return pl.pallas_call(
        functools.partial(_scan_kernel, tb=tb),
        grid=(nc + nm,),
        in_specs=[pl.BlockSpec(blk, lambda g: (fwd(g), 0, 0, 0)),
                  pl.BlockSpec(blk, lambda g: (fwd(g), 1, 0, 0)),
                  pl.BlockSpec(blk, lambda g: (bwd(g), 0, 0, 0)),
                  pl.BlockSpec(blk, lambda g: (bwd(g), 2, 0, 0))],
        out_specs=[pl.BlockSpec((tb, n, lanes), lambda g: (jnp.maximum(g - nc, 0), 0, 0)),
                   pl.BlockSpec((tb, n, lanes), lambda g: (jnp.minimum(nm - 1, nm - 1 + nc - g), 0, 0))],
        out_shape=[y_shape, y_shape],
        scratch_shapes=[pltpu.VMEM((2, n, n, lanes), F32)],
        compiler_params=_params("arbitrary"),
        name="wkv_scan",
    )(z, z, z, z)


def _na_kernel(q_ref, k_ref, v_ref, kc_ref, vc_ref, bias_ref, o_ref, *, rows):
    kh = NA_ROWS
    dn = (((1,), (1,)), ((), ()))

    def row(r, c):
        r_start = jnp.clip(r - kh // 2, 0, rows - kh)
        off = r - r_start
        q0 = pl.multiple_of(r * GRID_W, GRID_W)
        k0 = pl.multiple_of(r_start * GRID_W, GRID_W)
        for hh in range(2):
            ls = slice(hh * HEAD_DIM, (hh + 1) * HEAD_DIM)
            q = (q_ref[0, pl.ds(q0, GRID_W), ls] * (HEAD_DIM ** -0.5)).astype(BF16)
            kl = k_ref[0, pl.ds(k0, kh * GRID_W), ls].astype(BF16)
            vl = v_ref[0, pl.ds(k0, kh * GRID_W), ls].astype(BF16)
            kc = kc_ref[0, :, ls].astype(BF16)
            vc = vc_ref[0, :, ls].astype(BF16)
            s_loc = lax.dot_general(q, kl, dn, preferred_element_type=F32) + bias_ref[off, hh]
            s_ctx = lax.dot_general(q, kc, dn, preferred_element_type=F32)
            m = jnp.maximum(jnp.max(s_loc, axis=-1, keepdims=True), jnp.max(s_ctx, axis=-1, keepdims=True))
            e_loc = jnp.exp(s_loc - m)
            e_ctx = jnp.exp(s_ctx - m)
            den = jnp.sum(e_loc, axis=-1, keepdims=True) + jnp.sum(e_ctx, axis=-1, keepdims=True)
            o = (jnp.dot(e_loc.astype(BF16), vl, preferred_element_type=F32)
                 + jnp.dot(e_ctx.astype(BF16), vc, preferred_element_type=F32))
            o_ref[0, pl.ds(q0, GRID_W), ls] = (o / den).astype(o_ref.dtype)
        return c

    lax.fori_loop(0, rows, row, 0)


def _neighbourhood_attention(p, pc, bias, col_q, col_k, col_v):
    b, t, _ = p.shape
    l = pc.shape[1]
    rows = t // GRID_W
    n_pairs = bias.shape[1] // 2
    n_keys = NA_ROWS * GRID_W
    return pl.pallas_call(
        functools.partial(_na_kernel, rows=rows),
        grid=(b, n_pairs),
        in_specs=[pl.BlockSpec((1, t, 128), lambda i, h: (i, 0, col_q + h)),
                  pl.BlockSpec((1, t, 128), lambda i, h: (i, 0, col_k + h)),
                  pl.BlockSpec((1, t, 128), lambda i, h: (i, 0, col_v + h)),
                  pl.BlockSpec((1, l, 128), lambda i, h: (i, 0, col_k + h)),
                  pl.BlockSpec((1, l, 128), lambda i, h: (i, 0, col_v + h)),
                  pl.BlockSpec((NA_ROWS, 2, GRID_W, n_keys), lambda i, h: (0, h, 0, 0))],
        out_specs=pl.BlockSpec((1, t, 128), lambda i, h: (i, 0, h)),
        out_shape=jax.ShapeDtypeStruct((b, t, n_pairs * 128), BF16),
        compiler_params=_params("arbitrary", "arbitrary"),
        name="na_attention",
    )(p, p, p, pc, pc, bias)


def _na_bias_table(rpb):
    h = rpb.shape[0]
    col = np.arange(GRID_W)
    c_start = np.clip(col - NA_COLS // 2, 0, GRID_W - NA_COLS)
    col_ok = (col[None, :] >= c_start[:, None]) & (col[None, :] < c_start[:, None] + NA_COLS)
    dc = np.clip(col[None, :] - col[:, None], 1 - NA_COLS, NA_COLS - 1) + NA_COLS - 1
    pick = (dc.reshape(1, -1) == np.arange(2 * NA_COLS - 1)[:, None]).astype(np.float32)
    t = jnp.einsum("hrc,cx->hrx", rpb, pick, precision=HIGHEST)
    t = jnp.where(col_ok.reshape(-1), t, NEG_BIG).reshape(h, 2 * NA_ROWS - 1, GRID_W, GRID_W)
    tab = jnp.stack([t[:, NA_ROWS - 1 - off:2 * NA_ROWS - 1 - off] for off in range(NA_ROWS)], axis=0)
    return tab.transpose(0, 1, 3, 2, 4).reshape(NA_ROWS, h, GRID_W, NA_ROWS * GRID_W)


def _mix_out_kernel(y_ref, bonus_ref, gate_ref, ob_ref, x_ref, g1_ref, lg_ref, lb_ref, seg_ref, wa_ref, wb_ref, o_ref):
    y = y_ref[0]
    seg = seg_ref[...]
    inv = 1.0 / HEAD_DIM
    mu = _split_dot(y, seg) * inv
    yc = y - mu
    var = _split_dot(yc * yc, seg) * inv
    yn = (yc * lax.rsqrt(var + GN_EPS)) * lg_ref[...] + lb_ref[...]
    o_a = ((yn + bonus_ref[0]) * gate_ref[0]).astype(BF16)
    out = (jnp.dot(o_a, wa_ref[...], preferred_element_type=F32)
           + jnp.dot(ob_ref[0], wb_ref[...], preferred_element_type=F32))
    o_ref[0] = x_ref[0] + g1_ref[0] * out


def _mix_out(y, bonus, gate, o_b, x, g1, lnx_g, lnx_b, seg, w_a, w_b):
    b, t, d = x.shape
    aw = y.shape[2]
    tt = min(TOKEN_TILE, t)
    tok = lambda w: pl.BlockSpec((1, tt, w), lambda i, j: (i, j, 0))
    full = lambda a: pl.BlockSpec(a.shape, lambda i, j: (0,) * a.ndim)
    return pl.pallas_call(
        _mix_out_kernel,
        grid=(b, t // tt),
        in_specs=[tok(aw), tok(aw), tok(aw), tok(o_b.shape[2]), tok(d),
                  pl.BlockSpec((1, 1, d), lambda i, j: (i, 0, 0)),
                  full(lnx_g), full(lnx_b), full(seg), full(w_a), full(w_b)],
        out_specs=tok(d),
        out_shape=jax.ShapeDtypeStruct((b, t, d), F32),
        compiler_params=_params("arbitrary", "arbitrary"),
        name="mix_out",
    )(y, bonus, gate, o_b, x, g1, lnx_g, lnx_b, seg, w_a, w_b)


def _glu_kernel(x_ref, g_ref, sh_ref, sc_ref, w_ref, b_ref, o_ref):
    h = _adaln(x_ref[0], g_ref[...], sh_ref[0], sc_ref[0]).astype(BF16)
    u = jnp.dot(h, w_ref[...], preferred_element_type=F32) + b_ref[...]
    d = u.shape[1] // 2
    o_ref[0] = u[:, :d] * _sigmoid(u[:, d:])


def _glu_proj(x, g, sh, sc, w, bias):
    b, t, d = x.shape
    n = w.shape[1]
    tt = min(TOKEN_TILE, t)
    return pl.pallas_call(
        _glu_kernel,
        grid=(b, t // tt),
        in_specs=[pl.BlockSpec((1, tt, d), lambda i, j: (i, j, 0)),
                  pl.BlockSpec((1, d), lambda i, j: (0, 0)),
                  pl.BlockSpec((1, 1, d), lambda i, j: (i, 0, 0)),
                  pl.BlockSpec((1, 1, d), lambda i, j: (i, 0, 0)),
                  pl.BlockSpec((d, n), lambda i, j: (0, 0)),
                  pl.BlockSpec((1, n), lambda i, j: (0, 0))],
        out_specs=pl.BlockSpec((1, tt, n // 2), lambda i, j: (i, j, 0)),
        out_shape=jax.ShapeDtypeStruct((b, t, n // 2), F32),
        compiler_params=_params("arbitrary", "arbitrary"),
        name="glu_proj",
    )(x, g, sh, sc, w, bias)


def _conv_kernel(u_ref, up_ref, un_ref, dw_ref, dwb_ref, lg_ref, lb_ref, w2_ref, b2_ref, x_ref, g1_ref, o_ref,
                 win_ref, acc_ref, *, n_tiles):
    t = pl.program_id(1)
    tt, d = u_ref.shape[1], u_ref.shape[2]
    half = CONV_WIDTH // 2
    win_ref[0:HALO, :] = jnp.where(t > 0, up_ref[0], 0.0)
    win_ref[HALO:HALO + tt, :] = u_ref[0]
    win_ref[HALO + tt:2 * HALO + tt, :] = jnp.where(t < n_tiles - 1, un_ref[0], 0.0)
    rc = 32

    for base in range(0, tt, rc):
        for lc in range(d // 128):
            ls = slice(lc * 128, (lc + 1) * 128)
            acc = jnp.zeros((rc, 128), F32)
            for k in range(CONV_WIDTH):
                lo = base + HALO - half + k
                acc = acc + dw_ref[k:k + 1, ls] * win_ref[lo:lo + rc, ls]
            acc_ref[base:base + rc, ls] = acc
    u = acc_ref[...] + dwb_ref[...]
    mu = jnp.mean(u, axis=-1, keepdims=True)
    uc = u - mu
    var = jnp.mean(uc * uc, axis=-1, keepdims=True)
    un = (uc * lax.rsqrt(var + LN_EPS)) * lg_ref[...] + lb_ref[...]
    act = (un * _sigmoid(un)).astype(BF16)
    out = jnp.dot(act, w2_ref[...], preferred_element_type=F32) + b2_ref[...]
    o_ref[0] = x_ref[0] + g1_ref[0] * out


def _conv_module(u, dw, dwb, ln_g, ln_b, w2, b2, x, g1):
    b, t, d = x.shape
    tt = min(TOKEN_TILE, t)
    n_tiles = t // tt
    hb = tt // HALO
    tok = pl.BlockSpec((1, tt, d), lambda i, j: (i, j, 0))
    full = lambda a: pl.BlockSpec(a.shape, lambda i, j: (0,) * a.ndim)
    return pl.pallas_call(
        functools.partial(_conv_kernel, n_tiles=n_tiles),
        grid=(b, n_tiles),
        in_specs=[tok,
                  pl.BlockSpec((1, HALO, d), lambda i, j: (i, jnp.maximum(j * hb - 1, 0), 0)),
                  pl.BlockSpec((1, HALO, d), lambda i, j: (i, jnp.minimum((j + 1) * hb, t // HALO - 1), 0)),
                  full(dw), full(dwb), full(ln_g), full(ln_b), full(w2), full(b2), tok,
                  pl.BlockSpec((1, 1, d), lambda i, j: (i, 0, 0))],
        out_specs=tok,
        out_shape=jax.ShapeDtypeStruct((b, t, d), F32),
        scratch_shapes=[pltpu.VMEM((tt + 2 * HALO, d), F32), pltpu.VMEM((tt, d), F32)],
        compiler_params=_params("arbitrary", "arbitrary"),
        name="conv_module",
    )(u, u, u, dw, dwb, ln_g, ln_b, w2, b2, x, g1)


def _route_kernel(x_ref, g_ref, sh_ref, sc_ref, wr_ref, br_ref, hp_ref, route_ref, cnt_ref, carry_ref):
    @pl.when((pl.program_id(0) == 0) & (pl.program_id(1) == 0))
    def _():
        carry_ref[...] = jnp.zeros_like(carry_ref)

    h = _adaln(x_ref[0], g_ref[...], sh_ref[0], sc_ref[0])
    tt, d = h.shape
    hi = lax.bitcast_convert_type(h[:, :d // 2].astype(BF16).astype(F32), jnp.uint32)
    lo = lax.bitcast_convert_type(h[:, d // 2:].astype(BF16).astype(F32), jnp.uint32)
    hp_ref[0] = (hi & jnp.uint32(0xFFFF0000)) | (lo >> 16)

    logits = jnp.dot(h, wr_ref[...], preferred_element_type=F32, precision=HIGHEST) + br_ref[...]
    ne = logits.shape[1]
    lane = lax.broadcasted_iota(jnp.int32, (tt, ne), 1).astype(F32)
    work = logits
    mask = jnp.zeros((tt, ne), F32)
    picks, es = [], []
    den = jnp.zeros((tt, 1), F32)
    for k in range(TOP_K):
        m = jnp.max(work, axis=-1, keepdims=True)
        idx = jnp.min(jnp.where(work == m, lane, float(ne)), axis=-1, keepdims=True)
        pick = lane == idx
        if k == 0:
            top = m
        e = jnp.exp(m - top)
        den = den + e
        picks.append((pick, idx))
        es.append(e)
        mask = jnp.where(pick, 1.0, mask)
        work = jnp.where(pick, -jnp.inf, work)

    ri = lax.broadcasted_iota(jnp.int32, (tt, tt), 0)
    ci = lax.broadcasted_iota(jnp.int32, (tt, tt), 1)
    lower = jnp.where(ci < ri, 1.0, 0.0).astype(BF16)
    rank = jnp.dot(lower, mask.astype(BF16), preferred_element_type=F32) + carry_ref[...]
    carry_ref[...] = carry_ref[...] + jnp.sum(mask, axis=0, keepdims=True)
    cnt_ref[...] = carry_ref[...]

    out_lane = lax.broadcasted_iota(jnp.int32, (tt, 128), 1)
    route = jnp.zeros((tt, 128), F32)
    for k in range(TOP_K):
        pick, idx = picks[k]
        rk = jnp.sum(jnp.where(pick, rank, 0.0), axis=-1, keepdims=True)
        route = jnp.where(out_lane == k, idx, route)
        route = jnp.where(out_lane == TOP_K + k, rk, route)
        route = jnp.where(out_lane == 2 * TOP_K + k, es[k] / den, route)
    route_ref[0] = route


def _route(x, g, sh, sc, w_r, b_r):
    b, t, d = x.shape
    ne = w_r.shape[1]
    tt = min(TOKEN_TILE, t)
    return pl.pallas_call(
        _route_kernel,
        grid=(b, t // tt),
        in_specs=[pl.BlockSpec((1, tt, d), lambda i, j: (i, j, 0)),
                  pl.BlockSpec((1, d), lambda i, j: (0, 0)),
                  pl.BlockSpec((1, 1, d), lambda i, j: (i, 0, 0)),
                  pl.BlockSpec((1, 1, d), lambda i, j: (i, 0, 0)),
                  pl.BlockSpec((d, ne), lambda i, j: (0, 0)),
                  pl.BlockSpec((1, ne), lambda i, j: (0, 0))],
        out_specs=[pl.BlockSpec((1, tt, d // 2), lambda i, j: (i, j, 0)),
                   pl.BlockSpec((1, tt, 128), lambda i, j: (i, j, 0)),
                   pl.BlockSpec((1, ne), lambda i, j: (0, 0))],
        out_shape=[jax.ShapeDtypeStruct((b, t, d // 2), jnp.uint32),
                   jax.ShapeDtypeStruct((b, t, 128), F32),
                   jax.ShapeDtypeStruct((1, ne), F32)],
        scratch_shapes=[pltpu.VMEM((1, ne), F32)],
        compiler_params=_params("arbitrary", "arbitrary"),
        name="moe_route",
    )(x, g, sh, sc, w_r, b_r)


def _dispatch_kernel(pos_hbm, hp_ref, xs_in, xs_out, idx_ref, sem_idx, sem_rows, *, tile):
    del xs_in
    i = pl.program_id(0)
    n_idx = tile * TOP_K
    cp = pltpu.make_async_copy(pos_hbm.at[pl.ds(pl.multiple_of(i * n_idx, n_idx), n_idx)], idx_ref, sem_idx)
    cp.start()
    cp.wait()

    def row_copy(j, k):
        return pltpu.make_async_copy(hp_ref.at[pl.ds(j, 1)], xs_out.at[pl.ds(idx_ref[j * TOP_K + k], 1)], sem_rows)

    def issue(j, c):
        for k in range(TOP_K):
            row_copy(j, k).start()
        return c

    lax.fori_loop(0, tile, issue, 0)
    for k in range(TOP_K):
        pltpu.make_async_copy(hp_ref, xs_out.at[pl.ds(0, tile)], sem_rows).wait()


def _dispatch(pos_flat, hp, n_rows):
    n, w = hp.shape
    tile = TOKEN_TILE
    xs0 = jnp.zeros((n_rows, w), jnp.uint32)
    return pl.pallas_call(
        functools.partial(_dispatch_kernel, tile=tile),
        grid=(n // tile,),
        in_specs=[pl.BlockSpec(memory_space=pl.ANY),
                  pl.BlockSpec((tile, w), lambda i: (i, 0)),
                  pl.BlockSpec(memory_space=pl.ANY)],
        out_specs=pl.BlockSpec(memory_space=pl.ANY),
        out_shape=jax.ShapeDtypeStruct((n_rows, w), jnp.uint32),
        scratch_shapes=[pltpu.SMEM((tile * TOP_K,), jnp.int32),
                        pltpu.SemaphoreType.DMA, pltpu.SemaphoreType.DMA],
        input_output_aliases={2: 0},
        compiler_params=_params("arbitrary"),
        name="moe_dispatch",
    )(pos_flat, hp, xs0)


def _split_w1_kernel(w_ref, p_ref, g_ref, l_ref):
    w = w_ref[0].astype(BF16)
    n = w.shape[1]
    for c in range(n // 256):
        res = jnp.dot(w[:, c * 256:(c + 1) * 256], p_ref[...], preferred_element_type=F32)
        g_ref[0, :, c * 128:(c + 1) * 128] = res[:, :128].astype(BF16)
        l_ref[0, :, c * 128:(c + 1) * 128] = res[:, 128:].astype(BF16)


def _split_w1(w1):
    ne, d, f2 = w1.shape
    rows = 512
    r = np.arange(256)[:, None]
    c = np.arange(256)[None, :]
    sel = jnp.asarray(np.where(c < 128, r == 2 * c, r == 2 * (c - 128) + 1), BF16)
    out = jax.ShapeDtypeStruct((ne, d, f2 // 2), BF16)
    return pl.pallas_call(
        _split_w1_kernel,
        grid=(ne, d // rows),
        in_specs=[pl.BlockSpec((1, rows, f2), lambda e, i: (e, i, 0)),
                  pl.BlockSpec((256, 256), lambda e, i: (0, 0))],
        out_specs=[pl.BlockSpec((1, rows, f2 // 2), lambda e, i: (e, i, 0))] * 2,
        out_shape=[out, out],
        compiler_params=_params("arbitrary", "arbitrary"),
        name="split_w1",
    )(w1, sel)


def _expert_kernel(be_ref, nb_ref, xs_ref, w1g_ref, w1l_ref, b1g_ref, b1l_ref, w2_ref, b2_ref, ys_ref):
    del be_ref

    @pl.when(pl.program_id(0) < nb_ref[0])
    def _():
        u = xs_ref[...]
        half = u.shape[1]
        xa = lax.bitcast_convert_type(u & jnp.uint32(0xFFFF0000), F32).astype(BF16)
        xb = lax.bitcast_convert_type(u << 16, F32).astype(BF16)
        dot = functools.partial(jnp.dot, preferred_element_type=F32)
        ug = dot(xa, w1g_ref[0, :half, :]) + dot(xb, w1g_ref[0, half:, :]) + b1g_ref[0]
        ul = dot(xa, w1l_ref[0, :half, :]) + dot(xb, w1l_ref[0, half:, :]) + b1l_ref[0]
        glu = jnp.minimum(ug, SWIGLU_LIMIT)
        lin = jnp.clip(ul, -SWIGLU_LIMIT, SWIGLU_LIMIT)
        act = (glu * _sigmoid(SWIGLU_ALPHA * glu)) * (lin + 1.0)
        ys_ref[...] = dot(act.astype(BF16), w2_ref[0]) + b2_ref[0]

    @pl.when(pl.program_id(0) >= nb_ref[0])
    def _():
        ys_ref[...] = jnp.zeros_like(ys_ref)


def _expert_ffn(block_e, n_used, xs, w1g, w1l, b1g, b1l, w2, b2):
    n_rows, half = xs.shape
    ne, d, f = w1g.shape
    bm = EXPERT_ROWS
    n_blocks = n_rows // bm
    wspec = lambda s: pl.BlockSpec((1,) + s, lambda i, be, nb: (be[i], 0, 0))
    return pl.pallas_call(
        _expert_kernel,
        grid_spec=pltpu.PrefetchScalarGridSpec(
            num_scalar_prefetch=2,
            grid=(n_blocks,),
            in_specs=[pl.BlockSpec((bm, half), lambda i, be, nb: (i, 0)),
                      wspec((d, f)), wspec((d, f)), wspec((1, f)), wspec((1, f)), wspec((f, d)), wspec((1, d))],
            out_specs=pl.BlockSpec((bm, d), lambda i, be, nb: (i, 0)),
        ),
        out_shape=jax.ShapeDtypeStruct((n_rows, d), F32),
        compiler_params=_params("arbitrary"),
        name="moe_experts",
    )(block_e, n_used, xs, w1g, w1l, b1g, b1l, w2, b2)


def _combine_kernel(pos_hbm, ys_hbm, x_ref, route_ref, g2_ref, fg_ref, o_ref, buf_ref, idx_ref, sem_idx, sem_rows,
                    *, tile, final):
    i = pl.program_id(0)
    n_idx = tile * TOP_K
    cp = pltpu.make_async_copy(pos_hbm.at[pl.ds(pl.multiple_of(i * n_idx, n_idx), n_idx)], idx_ref, sem_idx)
    cp.start()
    cp.wait()

    def issue(j, c):
        for k in range(TOP_K):
            pltpu.make_async_copy(ys_hbm.at[pl.ds(idx_ref[j * TOP_K + k], 1)], buf_ref.at[k, pl.ds(j, 1)],
                                  sem_rows).start()
        return c

    lax.fori_loop(0, tile, issue, 0)
    for k in range(TOP_K):
        pltpu.make_async_copy(ys_hbm.at[pl.ds(0, tile)], buf_ref.at[k], sem_rows).wait()

    route = route_ref[...]
    acc = jnp.zeros(x_ref.shape, F32)
    for k in range(TOP_K):
        acc = acc + buf_ref[k] * route[:, 2 * TOP_K + k:2 * TOP_K + k + 1]
    x = x_ref[...] + g2_ref[0] * acc
    if final:
        x = (x * lax.rsqrt(jnp.mean(x * x, axis=-1, keepdims=True) + RMS_EPS)) * fg_ref[...]
    o_ref[...] = x


def _combine(pos_flat, ys, x2, route2, g2, final_g, tiles_per_batch, final):
    n, d = x2.shape
    tile = TOKEN_TILE
    return pl.pallas_call(
        functools.partial(_combine_kernel, tile=tile, final=final),
        grid=(n // tile,),
        in_specs=[pl.BlockSpec(memory_space=pl.ANY),
                  pl.BlockSpec(memory_space=pl.ANY),
                  pl.BlockSpec((tile, d), lambda i: (i, 0)),
                  pl.BlockSpec((tile, 128), lambda i: (i, 0)),
                  pl.BlockSpec((1, 1, d), lambda i: (i // tiles_per_batch, 0, 0)),
                  pl.BlockSpec((1, d), lambda i: (0, 0))],
        out_specs=pl.BlockSpec((tile, d), lambda i: (i, 0)),
        out_shape=jax.ShapeDtypeStruct((n, d), F32),
        scratch_shapes=[pltpu.VMEM((TOP_K, tile, d), F32),
                        pltpu.SMEM((tile * TOP_K,), jnp.int32),
                        pltpu.SemaphoreType.DMA, pltpu.SemaphoreType.DMA],
        compiler_params=_params("arbitrary"),
        name="moe_combine",
    )(pos_flat, ys, x2, route2, g2, final_g)


def _moe_layer(x, g, sh, sc, gate2, w_r, b_r, w1, b1, w2, b2, final_g, final):
    b, t, d = x.shape
    n = b * t
    ne = w_r.shape[1]
    bm = EXPERT_ROWS
    hp, route, counts = _route(x, g, sh, sc, w_r, b_r.reshape(1, ne))
    route2 = route.reshape(n, 128)

    counts = counts[0].astype(jnp.int32)
    padded = (counts + bm - 1) // bm * bm
    pad_end = jnp.cumsum(padded)
    pad_start = pad_end - padded
    n_blocks = -(-(n * TOP_K + ne * (bm - 1)) // bm)
    e_idx = route2[:, :TOP_K].astype(jnp.int32)
    experts = jnp.arange(ne, dtype=jnp.int32)
    start_of = jnp.sum(jnp.where(e_idx[:, :, None] == experts, pad_start, 0), axis=-1)
    pos_flat = (start_of + route2[:, TOP_K:2 * TOP_K].astype(jnp.int32)).reshape(-1)
    block_row = jnp.arange(n_blocks, dtype=jnp.int32) * bm
    block_e = jnp.minimum(jnp.sum((pad_end[None, :] <= block_row[:, None]).astype(jnp.int32), axis=-1), ne - 1)
    n_used = (pad_end[-1:] // bm).astype(jnp.int32)

    xs = _dispatch(pos_flat, hp.reshape(n, d // 2), n_blocks * bm)
    f = w2.shape[1]
    w1g, w1l = _split_w1(w1)
    b1g = b1[:, 0::2].reshape(ne, 1, f)
    b1l = b1[:, 1::2].reshape(ne, 1, f)
    ys = _expert_ffn(block_e, n_used, xs, w1g, w1l, b1g, b1l, w2.astype(BF16), b2.reshape(ne, 1, d))
    out = _combine(pos_flat, ys, x.reshape(n, d), route2, gate2, final_g.reshape(1, d), t // TOKEN_TILE, final)
    return out.reshape(b, t, d)


def _even_layer(x, ctx, mod, mod_c, norm_g, w_in, mu_prev, mu_next, w0, w2, a0, a2, g2, key_k, key_a, r_k,
                lnx_g, lnx_b, rpb, w_out):
    b, t, d = x.shape
    l = ctx.shape[1]
    aw = key_k.shape[0]
    dl = w2.shape[1]
    bw = (w_in.shape[1] - 3 * aw - 128 - 4 * dl) // 3
    n_heads = aw // HEAD_DIM
    sh1, sc1, g1 = (mod[:, None, i * d:(i + 1) * d] for i in range(3))
    shc = jnp.broadcast_to(mod_c[None, None, :d], (b, 1, d))
    scc = jnp.broadcast_to(mod_c[None, None, d:2 * d], (b, 1, d))

    c_ra, c_gd = bw, bw + aw
    c_ka = c_gd + 128
    c_va = c_ka + aw
    c_wd = c_va + aw
    c_ad = c_wd + 2 * dl
    c_kb = c_ad + 2 * dl
    c_vb = c_kb + bw
    cols = lambda a, lo, hi: a[..., lo:hi]
    pad = jnp.zeros((d, 128), F32)
    w_p = jnp.concatenate([cols(w_in, c_ra, c_gd), cols(w_in, c_ka, c_va), cols(w_in, c_va, c_wd),
                           cols(w_in, c_gd, c_ka), cols(w_in, c_wd, c_ad), cols(w_in, c_ad, c_kb), pad,
                           cols(w_in, 0, c_ra), cols(w_in, c_kb, c_vb), cols(w_in, c_vb, c_vb + bw)],
                          axis=1).astype(BF16)

    def shift_vec(mu):
        o = lambda c: c - c_ra
        return jnp.concatenate([mu[o(c_ra):o(c_gd)], mu[o(c_ka):o(c_va)], mu[o(c_va):o(c_wd)], mu[o(c_gd):o(c_ka)],
                                mu[o(c_wd):o(c_ad)], mu[o(c_ad):o(c_kb)], jnp.zeros((128,), F32)]).reshape(1, -1)

    blockdiag = lambda m: jnp.concatenate(
        [jnp.concatenate([m[0], jnp.zeros_like(m[0])], axis=1),
         jnp.concatenate([jnp.zeros_like(m[1]), m[1]], axis=1)], axis=0)
    head = jnp.arange(aw) // HEAD_DIM
    consts = {
        "mu_prev": shift_vec(mu_prev), "mu_next": shift_vec(mu_next),
        "w0": w0.reshape(1, 2 * aw), "w2": blockdiag(w2), "a0": a0.reshape(1, 2 * aw), "a2": blockdiag(a2),
        "g2": g2.astype(BF16), "key_k": key_k.reshape(1, aw), "key_a": key_a.reshape(1, aw),
        "r_k": r_k.reshape(1, aw), "seg": (head[:, None] == head[None, :]).astype(BF16),
    }

    g_row = norm_g.reshape(1, d)
    p = _in_proj(x, g_row, sh1, sc1, w_p)
    pc = _in_proj(ctx, g_row, shc, scc, w_p)
    z_m, bonus, gate = _rwkv_terms(p, consts)
    z_c, _, _ = _rwkv_terms(pc, consts)

    z = jnp.concatenate([z_c, z_m], axis=2)
    z = z.reshape(b, 9, l + t, n_heads, HEAD_DIM).transpose(2, 1, 4, 0, 3).reshape(l + t, 9, HEAD_DIM, b * n_heads)
    yf, yb = _wkv_scan(z, l)
    y = (yf + yb).reshape(t, HEAD_DIM, b, n_heads).transpose(2, 0, 3, 1).reshape(b, t, aw)

    qb = (4 * aw) // 128
    o_b = _neighbourhood_attention(p, pc, _na_bias_table(rpb), qb, qb + bw // 128, qb + 2 * bw // 128)
    return _mix_out(y, bonus, gate, o_b, x, g1, lnx_g.reshape(1, aw), lnx_b.reshape(1, aw), consts["seg"],
                    w_out[:aw].astype(BF16), w_out[aw:].astype(BF16))


def _odd_layer(x, mod, norm_g, pw1_w, pw1_b, dw_w, dw_b, ln_g, ln_b, pw2_w, pw2_b):
    b, t, d = x.shape
    sh1, sc1, g1 = (mod[:, None, i * d:(i + 1) * d] for i in range(3))
    u = _glu_proj(x, norm_g.reshape(1, d), sh1, sc1, pw1_w.astype(BF16), pw1_b.reshape(1, -1))
    dw = jnp.concatenate([dw_w, jnp.zeros((1, d), F32)], axis=0)
    return _conv_module(u, dw, dw_b.reshape(1, d), ln_g.reshape(1, d), ln_b.reshape(1, d), pw2_w.astype(BF16),
                        pw2_b.reshape(1, d), x, g1)


def kernel(x, c, ctx, c_ctx, ada_w, ada_b, norm_mix_g, norm_ffn_g, final_norm_g, mix_w_in, shift_mu_prev, shift_mu_next, decay_w0, decay_w2, iclr_a0, iclr_a2, gate_g2, key_k, key_a, bonus_r_k, lnx_g, lnx_b, na_rpb, mix_w_out, conv_pw1_w, conv_pw1_b, conv_dw_w, conv_dw_b, conv_ln_g, conv_ln_b, conv_pw2_w, conv_pw2_b, router_w, router_b, expert_w1, expert_b1, expert_w2, expert_b2):
    b, t, d = x.shape
    depth = ada_w.shape[0]
    rows = -(-(b + 1) // 8) * 8
    c_all = jnp.concatenate([c, c_ctx[None, :], jnp.zeros((rows - b - 1, d), F32)], axis=0)
    mod_all = _modulation(c_all, ada_w, ada_b)
    for l in range(depth):
        mod = mod_all[l, :b]
        i = l // 2
        if l % 2 == 0:
            x = _even_layer(x, ctx, mod, mod_all[l, b], norm_mix_g[l], mix_w_in[i], shift_mu_prev[i], shift_mu_next[i],
                            decay_w0[i], decay_w2[i], iclr_a0[i], iclr_a2[i], gate_g2[i], key_k[i], key_a[i],
                            bonus_r_k[i].reshape(-1), lnx_g[i], lnx_b[i], na_rpb[i], mix_w_out[i])
        else:
            x = _odd_layer(x, mod, norm_mix_g[l], conv_pw1_w[i], conv_pw1_b[i], conv_dw_w[i], conv_dw_b[i],
                           conv_ln_g[i], conv_ln_b[i], conv_pw2_w[i], conv_pw2_b[i])
        sh2, sc2, g2 = (mod[:, None, j * d:(j + 1) * d] for j in range(3, 6))
        x = _moe_layer(x, norm_ffn_g[l].reshape(1, d), sh2, sc2, g2, router_w[l], router_b[l], expert_w1[l],
                       expert_b1[l], expert_w2[l], expert_b2[l], final_norm_g, final=(l == depth - 1))
    return x
```

```python
import functools

import jax
import jax.numpy as jnp
import numpy as np
from jax import lax
from jax.experimental import pallas as pl
from jax.experimental.pallas import tpu as pltpu

F32 = jnp.float32
BF16 = jnp.bfloat16
HIGHEST = lax.Precision.HIGHEST

HEAD_DIM = 64
GRID_W = 64
NA_ROWS = 8
NA_COLS = 16
CONV_WIDTH = 31
N_EXPERTS = 32
TOP_K = 4
SWIGLU_ALPHA = 1.702
SWIGLU_LIMIT = 7.0
RMS_EPS = 1e-6
LN_EPS = 1e-5
GN_EPS = 64e-5
NEG_BIG = -1e30

VMEM_LIMIT_BYTES = 52 * 1024 * 1024
TOKEN_TILE = 256
SCAN_BLOCK = 16
EXPERT_ROWS = 256
HALO = 16
ISSUE_GROUP = 4


def _params(*sem):
    return pltpu.CompilerParams(dimension_semantics=sem, vmem_limit_bytes=VMEM_LIMIT_BYTES)


def _adaln(x, g, sh, sc):
    y = x * lax.rsqrt(jnp.mean(x * x, axis=-1, keepdims=True) + RMS_EPS)
    return (y * g) * (1.0 + sc) + sh


def _sigmoid(x):
    return 1.0 / (1.0 + jnp.exp(-x))


def _split_dot(x, m):
    hi = x.astype(BF16)
    r1 = x - hi.astype(F32)
    mid = r1.astype(BF16)
    lo = (r1 - mid.astype(F32)).astype(BF16)
    dot = functools.partial(jnp.dot, preferred_element_type=F32)
    return dot(hi, m) + dot(mid, m) + dot(lo, m)


def _mod_kernel(c_ref, w_ref, b_ref, o_ref):
    c = c_ref[...]
    s = c * _sigmoid(c)
    o_ref[0] = jnp.dot(s, w_ref[0], preferred_element_type=F32, precision=HIGHEST) + b_ref[0]


def _modulation(c_all, ada_w, ada_b):
    depth, d, n = ada_w.shape
    rows = c_all.shape[0]
    tn = 1536
    return pl.pallas_call(
        _mod_kernel,
        grid=(depth, n // tn),
        in_specs=[pl.BlockSpec((rows, d), lambda l, j: (0, 0)),
                  pl.BlockSpec((1, d, tn), lambda l, j: (l, 0, j)),
                  pl.BlockSpec((1, 1, tn), lambda l, j: (l, 0, j))],
        out_specs=pl.BlockSpec((1, rows, tn), lambda l, j: (l, 0, j)),
        out_shape=jax.ShapeDtypeStruct((depth, rows, n), F32),
        compiler_params=_params("arbitrary", "arbitrary"),
        name="modulation",
    )(c_all, ada_w, ada_b.reshape(depth, 1, n))


def _proj_kernel(x_ref, g_ref, sh_ref, sc_ref, w_ref, o_ref):
    h = _adaln(x_ref[0], g_ref[...], sh_ref[0], sc_ref[0]).astype(BF16)
    o_ref[0] = jnp.dot(h, w_ref[...], preferred_element_type=F32)


def _in_proj(x, g, sh, sc, w):
    b, t, d = x.shape
    n = w.shape[1]
    tt = min(TOKEN_TILE, t)
    return pl.pallas_call(
        _proj_kernel,
        grid=(b, t // tt),
        in_specs=[pl.BlockSpec((1, tt, d), lambda i, j: (i, j, 0)),
                  pl.BlockSpec((1, d), lambda i, j: (0, 0)),
                  pl.BlockSpec((1, 1, d), lambda i, j: (i, 0, 0)),
                  pl.BlockSpec((1, 1, d), lambda i, j: (i, 0, 0)),
                  pl.BlockSpec((d, n), lambda i, j: (0, 0))],
        out_specs=pl.BlockSpec((1, tt, n), lambda i, j: (i, j, 0)),
        out_shape=jax.ShapeDtypeStruct((b, t, n), F32),
        compiler_params=_params("arbitrary", "arbitrary"),
        name="in_proj",
    )(x, g, sh, sc, w)


def _terms_kernel(p_ref, pp_ref, pn_ref, mup_ref, mun_ref, w0_ref, w2_ref, a0_ref, a2_ref, g2_ref,
                  kk_ref, ka_ref, rk_ref, seg_ref, z_ref, bonus_ref, gate_ref, *, n_tiles):
    t = pl.program_id(1)
    p = p_ref[0]
    tt, aw = p.shape[0], kk_ref.shape[1]
    prev_row = jnp.where(t > 0, pp_ref[0, 7:8, :], 0.0)
    next_row = jnp.where(t < n_tiles - 1, pn_ref[0, 0:1, :], 0.0)
    rows = lax.broadcasted_iota(jnp.int32, p.shape, 0)
    prev = jnp.where(rows == 0, prev_row, pltpu.roll(p, 1, axis=0))
    nxt = jnp.where(rows == tt - 1, next_row, pltpu.roll(p, tt - 1, axis=0))
    s = p + mup_ref[...] * (prev - p) + mun_ref[...] * (nxt - p)
    r, k, v = s[:, :aw], s[:, aw:2 * aw], s[:, 2 * aw:3 * aw]
    lora = s[:, 3 * aw:]
    g_in, wd, ad = lora[:, 0:128], lora[:, 128:256], lora[:, 256:384]
    dotf = functools.partial(jnp.dot, preferred_element_type=F32, precision=HIGHEST)
    zw = -(w0_ref[...] + dotf(jnp.tanh(wd), w2_ref[...]))
    softplus = jnp.maximum(zw, 0.0) + jnp.log(1.0 + jnp.exp(-jnp.abs(zw)))
    decay = jnp.exp(-jnp.exp(-softplus - 0.5))
    a = _sigmoid(a0_ref[...] + dotf(ad, a2_ref[...]))
    seg = seg_ref[...]
    kk = k * kk_ref[...]
    kk = kk / jnp.maximum(jnp.sqrt(_split_dot(kk * kk, seg)), 1e-12)
    z_ref[0, 0] = kk
    z_ref[0, 1] = v
    z_ref[0, 2] = r
    kd_sum = jnp.zeros_like(k)
    for d in range(2):
        a_d = a[:, d * aw:(d + 1) * aw]
        k_dir = k * (1.0 + (a_d - 1.0) * ka_ref[...])
        z_ref[0, 3 + 3 * d] = decay[:, d * aw:(d + 1) * aw]
        z_ref[0, 4 + 3 * d] = k_dir
        z_ref[0, 5 + 3 * d] = kk * a_d
        kd_sum = kd_sum + k_dir
    bonus_ref[0] = _split_dot(r * kd_sum * rk_ref[...], seg) * v
    gate_ref[0] = jnp.dot(_sigmoid(g_in).astype(BF16), g2_ref[...], preferred_element_type=F32)


def _rwkv_terms(p, consts):
    b, t, _ = p.shape
    aw = consts["key_k"].shape[1]
    sw = 4 * aw
    tt = min(TOKEN_TILE, t)
    n_tiles = t // tt
    hb = tt // 8
    full = lambda a: pl.BlockSpec(a.shape, lambda i, j: (0,) * a.ndim)
    names = ("mu_prev", "mu_next", "w0", "w2", "a0", "a2", "g2", "key_k", "key_a", "r_k", "seg")
    cs = [consts[n] for n in names]
    out3 = jax.ShapeDtypeStruct((b, t, aw), F32)
    return pl.pallas_call(
        functools.partial(_terms_kernel, n_tiles=n_tiles),
        grid=(b, n_tiles),
        in_specs=[pl.BlockSpec((1, tt, sw), lambda i, j: (i, j, 0)),
                  pl.BlockSpec((1, 8, sw), lambda i, j: (i, jnp.maximum(j * hb - 1, 0), 0)),
                  pl.BlockSpec((1, 8, sw), lambda i, j: (i, jnp.minimum((j + 1) * hb, t // 8 - 1), 0))]
                 + [full(a) for a in cs],
        out_specs=[pl.BlockSpec((1, 9, tt, aw), lambda i, j: (i, 0, j, 0)),
                   pl.BlockSpec((1, tt, aw), lambda i, j: (i, j, 0)),
                   pl.BlockSpec((1, tt, aw), lambda i, j: (i, j, 0))],
        out_shape=[jax.ShapeDtypeStruct((b, 9, t, aw), F32), out3, out3],
        compiler_params=_params("arbitrary", "arbitrary"),
        name="rwkv_terms",
    )(p, p, p, *cs)


def _scan_kernel(zsf_ref, zdf_ref, zsb_ref, zdb_ref, yf_ref, yb_ref, s_ref, *, tb):
    @pl.when(pl.program_id(0) == 0)
    def _():
        s_ref[...] = jnp.zeros_like(s_ref)

    n = s_ref.shape[1]
    dirs = ((zsf_ref, zdf_ref, yf_ref), (zsb_ref, zdb_ref, yb_ref))

    def step(tf, carry):
        tidx = (tf, tb - 1 - tf)
        vecs = []
        for d, (zs, zd, _) in enumerate(dirs):
            ti = tidx[d]
            kk, r = zs[ti, 0], zs[ti, 2]
            w, kd, bb = zd[ti, 0], zd[ti, 1], zd[ti, 2]
            bbr = jnp.sum(bb * r, axis=0, keepdims=True)
            kr = jnp.sum(kd * r, axis=0, keepdims=True)
            vecs.append((kk, w * r, w, bb, kd, bbr, kr))

        def row(i, c):
            for d, (zs, _, y_ref) in enumerate(dirs):
                kk, wr, w, bb, kd, bbr, kr = vecs[d]
                ti = tidx[d]
                si = s_ref[d, i]
                sa = -jnp.sum(si * kk, axis=0, keepdims=True)
                y0 = jnp.sum(si * wr, axis=0, keepdims=True)
                vi = zs[ti, 1, pl.ds(i, 1), :]
                s_ref[d, i] = si * w + sa * bb + vi * kd
                y_ref[ti, pl.ds(i, 1), :] = y0 + sa * bbr + vi * kr
            return c

        lax.fori_loop(0, n, row, 0, unroll=4)
        return carry

    lax.fori_loop(0, tb, step, 0)


def _wkv_scan(z, n_ctx):
    t_all, _, n, lanes = z.shape
    tb = SCAN_BLOCK
    nc, nm = n_ctx // tb, (t_all - n_ctx) // tb
    fwd = lambda g: g
    bwd = lambda g: jnp.where(g < nc, nc - 1 - g, 2 * nc + nm - 1 - g)
    blk = (tb, 3, n, lanes)
    y_shape = jax.ShapeDtypeStruct((nm * tb, n, lanes), F32)
    return pl.pallas_call(
        functools.partial(_scan_kernel, tb=tb),
        grid=(nc + nm,),
        in_specs=[pl.BlockSpec(blk, lambda g: (fwd(g), 0, 0, 0)),
                  pl.BlockSpec(blk, lambda g: (fwd(g), 1, 0, 0)),
                  pl.BlockSpec(blk, lambda g: (bwd(g), 0, 0, 0)),
                  pl.BlockSpec(blk, lambda g: (bwd(g), 2, 0, 0))],
        out_specs=[pl.BlockSpec((tb, n, lanes), lambda g: (jnp.maximum(g - nc, 0), 0, 0)),
                   pl.BlockSpec((tb, n, lanes), lambda g: (jnp.minimum(nm - 1, nm - 1 + nc - g), 0, 0))],
        out_shape=[y_shape, y_shape],
        scratch_shapes=[pltpu.VMEM((2, n, n, lanes), F32)],
        compiler_params=_params("arbitrary"),
        name="wkv_scan",
    )(z, z, z, z)


def _na_kernel(q_ref, k_ref, v_ref, kc_ref, vc_ref, bias_ref, o_ref, kb_ref, vb_ref, kcb_ref, vcb_ref, *, rows):
    kh = NA_ROWS
    dn = (((1,), (1,)), ((), ()))
    kb_ref[...] = k_ref[0].astype(BF16)
    vb_ref[...] = v_ref[0].astype(BF16)
    kcb_ref[...] = kc_ref[0].astype(BF16)
    vcb_ref[...] = vc_ref[0].astype(BF16)
    head_of_lane = lax.broadcasted_iota(jnp.int32, (GRID_W, 2 * HEAD_DIM), 1) // HEAD_DIM

    def row(r, c):
        r_start = jnp.clip(r - kh // 2, 0, rows - kh)
        off = r - r_start
        q0 = pl.multiple_of(r * GRID_W, GRID_W)
        k0 = pl.multiple_of(r_start * GRID_W, GRID_W)
        q2 = q_ref[0, pl.ds(q0, GRID_W), :] * (HEAD_DIM ** -0.5)
        kl = kb_ref[pl.ds(k0, kh * GRID_W), :]
        vl = vb_ref[pl.ds(k0, kh * GRID_W), :]
        out = jnp.zeros((GRID_W, 2 * HEAD_DIM), F32)
        for hh in range(2):
            q = jnp.where(head_of_lane == hh, q2, 0.0).astype(BF16)
            s_loc = lax.dot_general(q, kl, dn, preferred_element_type=F32) + bias_ref[off, hh]
            s_ctx = lax.dot_general(q, kcb_ref[...], dn, preferred_element_type=F32)
            m = jnp.maximum(jnp.max(s_loc, axis=-1, keepdims=True), jnp.max(s_ctx, axis=-1, keepdims=True))
            e_loc = jnp.exp(s_loc - m)
            e_ctx = jnp.exp(s_ctx - m)
            den = jnp.sum(e_loc, axis=-1, keepdims=True) + jnp.sum(e_ctx, axis=-1, keepdims=True)
            o = (jnp.dot(e_loc.astype(BF16), vl, preferred_element_type=F32)
                 + jnp.dot(e_ctx.astype(BF16), vcb_ref[...], preferred_element_type=F32))
            out = jnp.where(head_of_lane == hh, o / den, out)
        o_ref[0, pl.ds(q0, GRID_W), :] = out.astype(o_ref.dtype)
        return c

    lax.fori_loop(0, rows, row, 0, unroll=2)


def _neighbourhood_attention(p, pc, bias, col_q, col_k, col_v):
    b, t, _ = p.shape
    l = pc.shape[1]
    rows = t // GRID_W
    n_pairs = bias.shape[1] // 2
    n_keys = NA_ROWS * GRID_W
    return pl.pallas_call(
        functools.partial(_na_kernel, rows=rows),
        grid=(b, n_pairs),
        in_specs=[pl.BlockSpec((1, t, 128), lambda i, h: (i, 0, col_q + h)),
                  pl.BlockSpec((1, t, 128), lambda i, h: (i, 0, col_k + h)),
                  pl.BlockSpec((1, t, 128), lambda i, h: (i, 0, col_v + h)),
                  pl.BlockSpec((1, l, 128), lambda i, h: (i, 0, col_k + h)),
                  pl.BlockSpec((1, l, 128), lambda i, h: (i, 0, col_v + h)),
                  pl.BlockSpec((NA_ROWS, 2, GRID_W, n_keys), lambda i, h: (0, h, 0, 0))],
        out_specs=pl.BlockSpec((1, t, 128), lambda i, h: (i, 0, h)),
        out_shape=jax.ShapeDtypeStruct((b, t, n_pairs * 128), BF16),
        scratch_shapes=[pltpu.VMEM((t, 128), BF16), pltpu.VMEM((t, 128), BF16),
                        pltpu.VMEM((l, 128), BF16), pltpu.VMEM((l, 128), BF16)],
        compiler_params=_params("arbitrary", "arbitrary"),
        name="na_attention",
    )(p, p, p, pc, pc, bias)


def _na_bias_table(rpb):
    h = rpb.shape[0]
    col = np.arange(GRID_W)
    c_start = np.clip(col - NA_COLS // 2, 0, GRID_W - NA_COLS)
    col_ok = (col[None, :] >= c_start[:, None]) & (col[None, :] < c_start[:, None] + NA_COLS)
    dc = np.clip(col[None, :] - col[:, None], 1 - NA_COLS, NA_COLS - 1) + NA_COLS - 1
    pick = (dc.reshape(1, -1) == np.arange(2 * NA_COLS - 1)[:, None]).astype(np.float32)
    t = jnp.einsum("hrc,cx->hrx", rpb, pick, precision=HIGHEST)
    t = jnp.where(col_ok.reshape(-1), t, NEG_BIG).reshape(h, 2 * NA_ROWS - 1, GRID_W, GRID_W)
    tab = jnp.stack([t[:, NA_ROWS - 1 - off:2 * NA_ROWS - 1 - off] for off in range(NA_ROWS)], axis=0)
    return tab.transpose(0, 1, 3, 2, 4).reshape(NA_ROWS, h, GRID_W, NA_ROWS * GRID_W)


def _mix_out_kernel(y_ref, bonus_ref, gate_ref, ob_ref, x_ref, g1_ref, lg_ref, lb_ref, seg_ref, wa_ref, wb_ref, o_ref):
    y = y_ref[0]
    seg = seg_ref[...]
    inv = 1.0 / HEAD_DIM
    mu = _split_dot(y, seg) * inv
    yc = y - mu
    var = _split_dot(yc * yc, seg) * inv
    yn = (yc * lax.rsqrt(var + GN_EPS)) * lg_ref[...] + lb_ref[...]
    o_a = ((yn + bonus_ref[0]) * gate_ref[0]).astype(BF16)
    out = (jnp.dot(o_a, wa_ref[...], preferred_element_type=F32)
           + jnp.dot(ob_ref[0], wb_ref[...], preferred_element_type=F32))
    o_ref[0] = x_ref[0] + g1_ref[0] * out


def _mix_out(y, bonus, gate, o_b, x, g1, lnx_g, lnx_b, seg, w_a, w_b):
    b, t, d = x.shape
    aw = y.shape[2]
    tt = min(TOKEN_TILE, t)
    tok = lambda w: pl.BlockSpec((1, tt, w), lambda i, j: (i, j, 0))
    full = lambda a: pl.BlockSpec(a.shape, lambda i, j: (0,) * a.ndim)
    return pl.pallas_call(
        _mix_out_kernel,
        grid=(b, t // tt),
        in_specs=[tok(aw), tok(aw), tok(aw), tok(o_b.shape[2]), tok(d),
                  pl.BlockSpec((1, 1, d), lambda i, j: (i, 0, 0)),
                  full(lnx_g), full(lnx_b), full(seg), full(w_a), full(w_b)],
        out_specs=tok(d),
        out_shape=jax.ShapeDtypeStruct((b, t, d), F32),
        compiler_params=_params("arbitrary", "arbitrary"),
        name="mix_out",
    )(y, bonus, gate, o_b, x, g1, lnx_g, lnx_b, seg, w_a, w_b)


def _glu_kernel(x_ref, g_ref, sh_ref, sc_ref, w_ref, b_ref, o_ref):
    h = _adaln(x_ref[0], g_ref[...], sh_ref[0], sc_ref[0]).astype(BF16)
    u = jnp.dot(h, w_ref[...], preferred_element_type=F32) + b_ref[...]
    d = u.shape[1] // 2
    o_ref[0] = u[:, :d] * _sigmoid(u[:, d:])


def _glu_proj(x, g, sh, sc, w, bias):
    b, t, d = x.shape
    n = w.shape[1]
    tt = min(TOKEN_TILE, t)
    return pl.pallas_call(
        _glu_kernel,
        grid=(b, t // tt),
        in_specs=[pl.BlockSpec((1, tt, d), lambda i, j: (i, j, 0)),
                  pl.BlockSpec((1, d), lambda i, j: (0, 0)),
                  pl.BlockSpec((1, 1, d), lambda i, j: (i, 0, 0)),
                  pl.BlockSpec((1, 1, d), lambda i, j: (i, 0, 0)),
                  pl.BlockSpec((d, n), lambda i, j: (0, 0)),
                  pl.BlockSpec((1, n), lambda i, j: (0, 0))],
        out_specs=pl.BlockSpec((1, tt, n // 2), lambda i, j: (i, j, 0)),
        out_shape=jax.ShapeDtypeStruct((b, t, n // 2), F32),
        compiler_params=_params("arbitrary", "arbitrary"),
        name="glu_proj",
    )(x, g, sh, sc, w, bias)


def _conv_kernel(u_ref, up_ref, un_ref, dw_ref, dwb_ref, lg_ref, lb_ref, w2_ref, b2_ref, x_ref, g1_ref, o_ref,
                 win_ref, acc_ref, *, n_tiles):
    t = pl.program_id(1)
    tt, d = u_ref.shape[1], u_ref.shape[2]
    half = CONV_WIDTH // 2
    win_ref[0:HALO, :] = jnp.where(t > 0, up_ref[0], 0.0)
    win_ref[HALO:HALO + tt, :] = u_ref[0]
    win_ref[HALO + tt:2 * HALO + tt, :] = jnp.where(t < n_tiles - 1, un_ref[0], 0.0)
    rc = 32

    for base in range(0, tt, rc):
        for lc in range(d // 128):
            ls = slice(lc * 128, (lc + 1) * 128)
            acc = jnp.zeros((rc, 128), F32)
            for k in range(CONV_WIDTH):
                lo = base + HALO - half + k
                acc = acc + dw_ref[k:k + 1, ls] * win_ref[lo:lo + rc, ls]
            acc_ref[base:base + rc, ls] = acc
    u = acc_ref[...] + dwb_ref[...]
    mu = jnp.mean(u, axis=-1, keepdims=True)
    uc = u - mu
    var = jnp.mean(uc * uc, axis=-1, keepdims=True)
    un = (uc * lax.rsqrt(var + LN_EPS)) * lg_ref[...] + lb_ref[...]
    act = (un * _sigmoid(un)).astype(BF16)
    out = jnp.dot(act, w2_ref[...], preferred_element_type=F32) + b2_ref[...]
    o_ref[0] = x_ref[0] + g1_ref[0] * out


def _conv_module(u, dw, dwb, ln_g, ln_b, w2, b2, x, g1):
    b, t, d = x.shape
    tt = min(TOKEN_TILE, t)
    n_tiles = t // tt
    hb = tt // HALO
    tok = pl.BlockSpec((1, tt, d), lambda i, j: (i, j, 0))
    full = lambda a: pl.BlockSpec(a.shape, lambda i, j: (0,) * a.ndim)
    return pl.pallas_call(
        functools.partial(_conv_kernel, n_tiles=n_tiles),
        grid=(b, n_tiles),
        in_specs=[tok,
                  pl.BlockSpec((1, HALO, d), lambda i, j: (i, jnp.maximum(j * hb - 1, 0), 0)),
                  pl.BlockSpec((1, HALO, d), lambda i, j: (i, jnp.minimum((j + 1) * hb, t // HALO - 1), 0)),
                  full(dw), full(dwb), full(ln_g), full(ln_b), full(w2), full(b2), tok,
                  pl.BlockSpec((1, 1, d), lambda i, j: (i, 0, 0))],
        out_specs=tok,
        out_shape=jax.ShapeDtypeStruct((b, t, d), F32),
        scratch_shapes=[pltpu.VMEM((tt + 2 * HALO, d), F32), pltpu.VMEM((tt, d), F32)],
        compiler_params=_params("arbitrary", "arbitrary"),
        name="conv_module",
    )(u, u, u, dw, dwb, ln_g, ln_b, w2, b2, x, g1)


def _route_kernel(x_ref, g_ref, sh_ref, sc_ref, wr_ref, br_ref, hp_ref, route_ref, cnt_ref, carry_ref):
    @pl.when((pl.program_id(0) == 0) & (pl.program_id(1) == 0))
    def _():
        carry_ref[...] = jnp.zeros_like(carry_ref)

    h = _adaln(x_ref[0], g_ref[...], sh_ref[0], sc_ref[0])
    tt, d = h.shape
    hi = lax.bitcast_convert_type(h[:, :d // 2].astype(BF16).astype(F32), jnp.uint32)
    lo = lax.bitcast_convert_type(h[:, d // 2:].astype(BF16).astype(F32), jnp.uint32)
    packed = (hi & jnp.uint32(0xFFFF0000)) | (lo >> 16)
    n_ch = d // 2 // 128
    for c in range(n_ch):
        hp_ref[0, pl.ds(c, tt, stride=n_ch), :] = packed[:, c * 128:(c + 1) * 128]

    logits = jnp.dot(h, wr_ref[...], preferred_element_type=F32, precision=HIGHEST) + br_ref[...]
    ne = logits.shape[1]
    lane = lax.broadcasted_iota(jnp.int32, (tt, ne), 1).astype(F32)
    work = logits
    mask = jnp.zeros((tt, ne), F32)
    picks, es = [], []
    den = jnp.zeros((tt, 1), F32)
    for k in range(TOP_K):
        m = jnp.max(work, axis=-1, keepdims=True)
        idx = jnp.min(jnp.where(work == m, lane, float(ne)), axis=-1, keepdims=True)
        pick = lane == idx
        if k == 0:
            top = m
        e = jnp.exp(m - top)
        den = den + e
        picks.append((pick, idx))
        es.append(e)
        mask = jnp.where(pick, 1.0, mask)
        work = jnp.where(pick, -jnp.inf, work)

    ri = lax.broadcasted_iota(jnp.int32, (tt, tt), 0)
    ci = lax.broadcasted_iota(jnp.int32, (tt, tt), 1)
    lower = jnp.where(ci < ri, 1.0, 0.0).astype(BF16)
    rank = jnp.dot(lower, mask.astype(BF16), preferred_element_type=F32) + carry_ref[...]
    carry_ref[...] = carry_ref[...] + jnp.sum(mask, axis=0, keepdims=True)
    cnt_ref[...] = carry_ref[...]

    out_lane = lax.broadcasted_iota(jnp.int32, (tt, 128), 1)
    route = jnp.zeros((tt, 128), F32)
    for k in range(TOP_K):
        pick, idx = picks[k]
        rk = jnp.sum(jnp.where(pick, rank, 0.0), axis=-1, keepdims=True)
        route = jnp.where(out_lane == k, idx, route)
        route = jnp.where(out_lane == TOP_K + k, rk, route)
        route = jnp.where(out_lane == 2 * TOP_K + k, es[k] / den, route)
    route_ref[0] = route


def _route(x, g, sh, sc, w_r, b_r):
    b, t, d = x.shape
    ne = w_r.shape[1]
    tt = min(TOKEN_TILE, t)
    return pl.pallas_call(
        _route_kernel,
        grid=(b, t // tt),
        in_specs=[pl.BlockSpec((1, tt, d), lambda i, j: (i, j, 0)),
                  pl.BlockSpec((1, d), lambda i, j: (0, 0)),
                  pl.BlockSpec((1, 1, d), lambda i, j: (i, 0, 0)),
                  pl.BlockSpec((1, 1, d), lambda i, j: (i, 0, 0)),
                  pl.BlockSpec((d, ne), lambda i, j: (0, 0)),
                  pl.BlockSpec((1, ne), lambda i, j: (0, 0))],
        out_specs=[pl.BlockSpec((1, tt * (d // 256), 128), lambda i, j: (i, j, 0)),
                   pl.BlockSpec((1, tt, 128), lambda i, j: (i, j, 0)),
                   pl.BlockSpec((1, ne), lambda i, j: (0, 0))],
        out_shape=[jax.ShapeDtypeStruct((b, t * (d // 256), 128), jnp.uint32),
                   jax.ShapeDtypeStruct((b, t, 128), F32),
                   jax.ShapeDtypeStruct((1, ne), F32)],
        scratch_shapes=[pltpu.VMEM((1, ne), F32)],
        compiler_params=_params("arbitrary", "arbitrary"),
        name="moe_route",
    )(x, g, sh, sc, w_r, b_r)


def _dispatch_kernel(pos_hbm, hp_ref, xs_in, xs_out, idx_ref, sem_idx, sem_rows, *, tile, n_tiles, rc):
    del xs_in
    n_idx = tile * TOP_K

    def idx_copy(i, slot):
        return pltpu.make_async_copy(pos_hbm.at[i], idx_ref.at[pl.ds(slot * n_idx, n_idx)], sem_idx.at[slot])

    i = pl.program_id(0)
    slot = i % 2

    @pl.when(i == 0)
    def _():
        idx_copy(0, 0).start()

    idx_copy(i, slot).wait()

    @pl.when(i + 1 < n_tiles)
    def _():
        idx_copy(i + 1, 1 - slot).start()

    ibase = slot * n_idx

    def issue(jg, c2):
        j0 = jg * ISSUE_GROUP
        rows = [idx_ref[ibase + j0 * TOP_K + q] for q in range(ISSUE_GROUP * TOP_K)]
        for q, row in enumerate(rows):
            src = hp_ref.at[pl.ds(pl.multiple_of((j0 + q // TOP_K) * rc, rc), rc)]
            pltpu.make_async_copy(src, xs_out.at[pl.ds(pl.multiple_of(row * rc, rc), rc)], sem_rows).start()
        return c2

    lax.fori_loop(0, tile // ISSUE_GROUP, issue, 0)
    for _ in range(TOP_K):
        pltpu.make_async_copy(hp_ref, xs_out.at[pl.ds(0, tile * rc)], sem_rows).wait()


def _dispatch(pos_flat, hp, n_rows, rc):
    n, w = hp.shape[0] // rc, hp.shape[1]
    tile = TOKEN_TILE
    xs0 = jnp.zeros((n_rows * rc, w), jnp.uint32)
    any_spec = pl.BlockSpec(memory_space=pl.ANY)
    return pl.pallas_call(
        functools.partial(_dispatch_kernel, tile=tile, n_tiles=n // tile, rc=rc),
        grid=(n // tile,),
        in_specs=[any_spec, pl.BlockSpec((tile * rc, w), lambda i: (i, 0)), any_spec],
        out_specs=any_spec,
        out_shape=jax.ShapeDtypeStruct((n_rows * rc, w), jnp.uint32),
        scratch_shapes=[pltpu.SMEM((2 * tile * TOP_K,), jnp.int32),
                        pltpu.SemaphoreType.DMA((2,)), pltpu.SemaphoreType.DMA],
        input_output_aliases={2: 0},
        compiler_params=_params("arbitrary"),
        name="moe_dispatch",
    )(pos_flat.reshape(n // tile, tile * TOP_K), hp, xs0)


def _split_w1_kernel(w_ref, p_ref, g_ref, l_ref):
    w = w_ref[0].astype(BF16)
    n = w.shape[1]
    for c in range(n // 256):
        res = jnp.dot(w[:, c * 256:(c + 1) * 256], p_ref[...], preferred_element_type=F32)
        g_ref[0, :, c * 128:(c + 1) * 128] = res[:, :128].astype(BF16)
        l_ref[0, :, c * 128:(c + 1) * 128] = res[:, 128:].astype(BF16)


def _split_w1(w1):
    ne, d, f2 = w1.shape
    rows = 512
    r = np.arange(256)[:, None]
    c = np.arange(256)[None, :]
    sel = jnp.asarray(np.where(c < 128, r == 2 * c, r == 2 * (c - 128) + 1), BF16)
    out = jax.ShapeDtypeStruct((ne, d, f2 // 2), BF16)
    return pl.pallas_call(
        _split_w1_kernel,
        grid=(ne, d // rows),
        in_specs=[pl.BlockSpec((1, rows, f2), lambda e, i: (e, i, 0)),
                  pl.BlockSpec((256, 256), lambda e, i: (0, 0))],
        out_specs=[pl.BlockSpec((1, rows, f2 // 2), lambda e, i: (e, i, 0))] * 2,
        out_shape=[out, out],
        compiler_params=_params("arbitrary", "arbitrary"),
        name="split_w1",
    )(w1, sel)


def _expert_kernel(be_ref, nb_ref, xs_ref, w1g_ref, w1l_ref, b1g_ref, b1l_ref, w2_ref, b2_ref, ys_ref, *, bm):
    del be_ref
    rc = xs_ref.shape[0] // bm
    oc = ys_ref.shape[0] // bm

    @pl.when(pl.program_id(0) < nb_ref[0])
    def _():
        u = jnp.concatenate([xs_ref[pl.ds(c, bm, stride=rc), :] for c in range(rc)], axis=1)
        half = u.shape[1]
        xa = lax.bitcast_convert_type(u & jnp.uint32(0xFFFF0000), F32).astype(BF16)
        xb = lax.bitcast_convert_type(u << 16, F32).astype(BF16)
        dot = functools.partial(jnp.dot, preferred_element_type=F32)
        ug = dot(xa, w1g_ref[0, :half, :]) + dot(xb, w1g_ref[0, half:, :]) + b1g_ref[0]
        ul = dot(xa, w1l_ref[0, :half, :]) + dot(xb, w1l_ref[0, half:, :]) + b1l_ref[0]
        glu = jnp.minimum(ug, SWIGLU_LIMIT)
        lin = jnp.clip(ul, -SWIGLU_LIMIT, SWIGLU_LIMIT)
        act = (glu * _sigmoid(SWIGLU_ALPHA * glu)) * (lin + 1.0)
        y = dot(act.astype(BF16), w2_ref[0]) + b2_ref[0]
        for c in range(oc):
            ys_ref[pl.ds(c, bm, stride=oc), :] = y[:, c * 128:(c + 1) * 128]

    @pl.when(pl.program_id(0) >= nb_ref[0])
    def _():
        ys_ref[...] = jnp.zeros_like(ys_ref)


def _expert_ffn(block_e, n_used, xs, w1g, w1l, b1g, b1l, w2, b2, rc):
    n_rows = xs.shape[0] // rc
    ne, d, f = w1g.shape
    oc = d // 128
    bm = EXPERT_ROWS
    n_blocks = n_rows // bm
    wspec = lambda s: pl.BlockSpec((1,) + s, lambda i, be, nb: (be[i], 0, 0))
    return pl.pallas_call(
        functools.partial(_expert_kernel, bm=bm),
        grid_spec=pltpu.PrefetchScalarGridSpec(
            num_scalar_prefetch=2,
            grid=(n_blocks,),
            in_specs=[pl.BlockSpec((bm * rc, 128), lambda i, be, nb: (i, 0)),
                      wspec((d, f)), wspec((d, f)), wspec((1, f)), wspec((1, f)), wspec((f, d)), wspec((1, d))],
            out_specs=pl.BlockSpec((bm * oc, 128), lambda i, be, nb: (i, 0)),
        ),
        out_shape=jax.ShapeDtypeStruct((n_rows * oc, 128), F32),
        compiler_params=_params("arbitrary"),
        name="moe_experts",
    )(block_e, n_used, xs, w1g, w1l, b1g, b1l, w2, b2)


def _combine_kernel(pos_hbm, ys_hbm, x_ref, route_ref, g2_ref, fg_ref, o_ref, buf_ref, idx_ref, sem_idx, sem_rows,
                    *, tile, n_tiles, final):
    i = pl.program_id(0)
    n_idx = tile * TOP_K
    slot = i % 2
    oc = x_ref.shape[1] // 128

    def idx_copy(t, s):
        src = pos_hbm.at[pl.ds(pl.multiple_of(t * n_idx, n_idx), n_idx)]
        return pltpu.make_async_copy(src, idx_ref.at[pl.ds(s * n_idx, n_idx)], sem_idx.at[s])

    def gather(s):
        ibase = s * n_idx

        def issue(jg, c):
            j0 = jg * ISSUE_GROUP
            rows = [idx_ref[ibase + j0 * TOP_K + q] for q in range(ISSUE_GROUP * TOP_K)]
            for q, row in enumerate(rows):
                dst_row = pl.multiple_of((j0 + q // TOP_K) * oc, oc)
                pltpu.make_async_copy(ys_hbm.at[pl.ds(pl.multiple_of(row * oc, oc), oc)],
                                      buf_ref.at[s, q % TOP_K, pl.ds(dst_row, oc)], sem_rows.at[s]).start()
            return c
        lax.fori_loop(0, tile // ISSUE_GROUP, issue, 0)

    @pl.when(i == 0)
    def _():
        idx_copy(0, 0).start()
        idx_copy(0, 0).wait()
        gather(0)
        if n_tiles > 1:
            idx_copy(1, 1).start()

    for k in range(TOP_K):
        pltpu.make_async_copy(ys_hbm.at[pl.ds(0, tile * oc)], buf_ref.at[slot, k], sem_rows.at[slot]).wait()

    @pl.when(i + 1 < n_tiles)
    def _():
        idx_copy(i + 1, 1 - slot).wait()
        gather(1 - slot)

    @pl.when(i + 2 < n_tiles)
    def _():
        idx_copy(i + 2, slot).start()

    route = route_ref[...]
    chunks = []
    for c in range(oc):
        acc = jnp.zeros((tile, 128), F32)
        for k in range(TOP_K):
            acc = acc + buf_ref[slot, k, pl.ds(c, tile, stride=oc), :] * route[:, 2 * TOP_K + k:2 * TOP_K + k + 1]
        chunks.append(acc)
    x = x_ref[...] + g2_ref[0] * jnp.concatenate(chunks, axis=1)
    if final:
        x = (x * lax.rsqrt(jnp.mean(x * x, axis=-1, keepdims=True) + RMS_EPS)) * fg_ref[...]
    o_ref[...] = x


def _combine(pos_flat, ys, x2, route2, g2, final_g, tiles_per_batch, final):
    n, d = x2.shape
    tile = TOKEN_TILE
    return pl.pallas_call(
        functools.partial(_combine_kernel, tile=tile, n_tiles=n // tile, final=final),
        grid=(n // tile,),
        in_specs=[pl.BlockSpec(memory_space=pl.ANY),
                  pl.BlockSpec(memory_space=pl.ANY),
                  pl.BlockSpec((tile, d), lambda i: (i, 0)),
                  pl.BlockSpec((tile, 128), lambda i: (i, 0)),
                  pl.BlockSpec((1, 1, d), lambda i: (i // tiles_per_batch, 0, 0)),
                  pl.BlockSpec((1, d), lambda i: (0, 0))],
        out_specs=pl.BlockSpec((tile, d), lambda i: (i, 0)),
        out_shape=jax.ShapeDtypeStruct((n, d), F32),
        scratch_shapes=[pltpu.VMEM((2, TOP_K, tile * (d // 128), 128), F32),
                        pltpu.SMEM((2 * tile * TOP_K,), jnp.int32),
                        pltpu.SemaphoreType.DMA((2,)), pltpu.SemaphoreType.DMA((2,))],
        compiler_params=_params("arbitrary"),
        name="moe_combine",
    )(pos_flat, ys, x2, route2, g2, final_g)


def _moe_layer(x, g, sh, sc, gate2, w_r, b_r, w1, b1, w2, b2, final_g, final):
    b, t, d = x.shape
    n = b * t
    ne = w_r.shape[1]
    bm = EXPERT_ROWS
    hp, route, counts = _route(x, g, sh, sc, w_r, b_r.reshape(1, ne))
    route2 = route.reshape(n, 128)

    counts = counts[0].astype(jnp.int32)
    padded = (counts + bm - 1) // bm * bm
    pad_end = jnp.cumsum(padded)
    pad_start = pad_end - padded
    n_blocks = -(-(n * TOP_K + ne * (bm - 1)) // bm)
    e_idx = route2[:, :TOP_K].astype(jnp.int32)
    experts = jnp.arange(ne, dtype=jnp.int32)
    start_of = jnp.sum(jnp.where(e_idx[:, :, None] == experts, pad_start, 0), axis=-1)
    pos_flat = (start_of + route2[:, TOP_K:2 * TOP_K].astype(jnp.int32)).reshape(-1)
    block_row = jnp.arange(n_blocks, dtype=jnp.int32) * bm
    block_e = jnp.minimum(jnp.sum((pad_end[None, :] <= block_row[:, None]).astype(jnp.int32), axis=-1), ne - 1)
    n_used = (pad_end[-1:] // bm).astype(jnp.int32)

    rc = d // 256
    xs = _dispatch(pos_flat, hp.reshape(n * rc, 128), n_blocks * bm, rc)
    f = w2.shape[1]
    w1g, w1l = _split_w1(w1)
    b1g = b1[:, 0::2].reshape(ne, 1, f)
    b1l = b1[:, 1::2].reshape(ne, 1, f)
    ys = _expert_ffn(block_e, n_used, xs, w1g, w1l, b1g, b1l, w2.astype(BF16), b2.reshape(ne, 1, d), rc)
    out = _combine(pos_flat, ys, x.reshape(n, d), route2, gate2, final_g.reshape(1, d), t // TOKEN_TILE, final)
    return out.reshape(b, t, d)


def _even_layer(x, ctx, mod, mod_c, norm_g, w_in, mu_prev, mu_next, w0, w2, a0, a2, g2, key_k, key_a, r_k,
                lnx_g, lnx_b, rpb, w_out):
    b, t, d = x.shape
    l = ctx.shape[1]
    aw = key_k.shape[0]
    dl = w2.shape[1]
    bw = (w_in.shape[1] - 3 * aw - 128 - 4 * dl) // 3
    n_heads = aw // HEAD_DIM
    sh1, sc1, g1 = (mod[:, None, i * d:(i + 1) * d] for i in range(3))
    shc = jnp.broadcast_to(mod_c[None, None, :d], (b, 1, d))
    scc = jnp.broadcast_to(mod_c[None, None, d:2 * d], (b, 1, d))

    c_ra, c_gd = bw, bw + aw
    c_ka = c_gd + 128
    c_va = c_ka + aw
    c_wd = c_va + aw
    c_ad = c_wd + 2 * dl
    c_kb = c_ad + 2 * dl
    c_vb = c_kb + bw
    cols = lambda a, lo, hi: a[..., lo:hi]
    pad = jnp.zeros((d, 128), F32)
    w_p = jnp.concatenate([cols(w_in, c_ra, c_gd), cols(w_in, c_ka, c_va), cols(w_in, c_va, c_wd),
                           cols(w_in, c_gd, c_ka), cols(w_in, c_wd, c_ad), cols(w_in, c_ad, c_kb), pad,
                           cols(w_in, 0, c_ra), cols(w_in, c_kb, c_vb), cols(w_in, c_vb, c_vb + bw)],
                          axis=1).astype(BF16)

    def shift_vec(mu):
        o = lambda c: c - c_ra
        return jnp.concatenate([mu[o(c_ra):o(c_gd)], mu[o(c_ka):o(c_va)], mu[o(c_va):o(c_wd)], mu[o(c_gd):o(c_ka)],
                                mu[o(c_wd):o(c_ad)], mu[o(c_ad):o(c_kb)], jnp.zeros((128,), F32)]).reshape(1, -1)

    blockdiag = lambda m: jnp.concatenate(
        [jnp.concatenate([m[0], jnp.zeros_like(m[0])], axis=1),
         jnp.concatenate([jnp.zeros_like(m[1]), m[1]], axis=1)], axis=0)
    head = jnp.arange(aw) // HEAD_DIM
    consts = {
        "mu_prev": shift_vec(mu_prev), "mu_next": shift_vec(mu_next),
        "w0": w0.reshape(1, 2 * aw), "w2": blockdiag(w2), "a0": a0.reshape(1, 2 * aw), "a2": blockdiag(a2),
        "g2": g2.astype(BF16), "key_k": key_k.reshape(1, aw), "key_a": key_a.reshape(1, aw),
        "r_k": r_k.reshape(1, aw), "seg": (head[:, None] == head[None, :]).astype(BF16),
    }

    g_row = norm_g.reshape(1, d)
    p = _in_proj(x, g_row, sh1, sc1, w_p)
    pc = _in_proj(ctx, g_row, shc, scc, w_p)
    z_m, bonus, gate = _rwkv_terms(p, consts)
    z_c, _, _ = _rwkv_terms(pc, consts)

    z = jnp.concatenate([z_c, z_m], axis=2)
    z = z.reshape(b, 9, l + t, n_heads, HEAD_DIM).transpose(2, 1, 4, 0, 3).reshape(l + t, 9, HEAD_DIM, b * n_heads)
    yf, yb = _wkv_scan(z, l)
    y = (yf + yb).reshape(t, HEAD_DIM, b, n_heads).transpose(2, 0, 3, 1).reshape(b, t, aw)

    qb = (4 * aw) // 128
    o_b = _neighbourhood_attention(p, pc, _na_bias_table(rpb), qb, qb + bw // 128, qb + 2 * bw // 128)
    return _mix_out(y, bonus, gate, o_b, x, g1, lnx_g.reshape(1, aw), lnx_b.reshape(1, aw), consts["seg"],
                    w_out[:aw].astype(BF16), w_out[aw:].astype(BF16))


def _odd_layer(x, mod, norm_g, pw1_w, pw1_b, dw_w, dw_b, ln_g, ln_b, pw2_w, pw2_b):
    b, t, d = x.shape
    sh1, sc1, g1 = (mod[:, None, i * d:(i + 1) * d] for i in range(3))
    u = _glu_proj(x, norm_g.reshape(1, d), sh1, sc1, pw1_w.astype(BF16), pw1_b.reshape(1, -1))
    dw = jnp.concatenate([dw_w, jnp.zeros((1, d), F32)], axis=0)
    return _conv_module(u, dw, dw_b.reshape(1, d), ln_g.reshape(1, d), ln_b.reshape(1, d), pw2_w.astype(BF16),
                        pw2_b.reshape(1, d), x, g1)


def kernel(x, c, ctx, c_ctx, ada_w, ada_b, norm_mix_g, norm_ffn_g, final_norm_g, mix_w_in, shift_mu_prev, shift_mu_next, decay_w0, decay_w2, iclr_a0, iclr_a2, gate_g2, key_k, key_a, bonus_r_k, lnx_g, lnx_b, na_rpb, mix_w_out, conv_pw1_w, conv_pw1_b, conv_dw_w, conv_dw_b, conv_ln_g, conv_ln_b, conv_pw2_w, conv_pw2_b, router_w, router_b, expert_w1, expert_b1, expert_w2, expert_b2):
    b, t, d = x.shape
    depth = ada_w.shape[0]
    rows = -(-(b + 1) // 8) * 8
    c_all = jnp.concatenate([c, c_ctx[None, :], jnp.zeros((rows - b - 1, d), F32)], axis=0)
    mod_all = _modulation(c_all, ada_w, ada_b)
    for l in range(depth):
        mod = mod_all[l, :b]
        i = l // 2
        if l % 2 == 0:
            x = _even_layer(x, ctx, mod, mod_all[l, b], norm_mix_g[l], mix_w_in[i], shift_mu_prev[i], shift_mu_next[i],
                            decay_w0[i], decay_w2[i], iclr_a0[i], iclr_a2[i], gate_g2[i], key_k[i], key_a[i],
                            bonus_r_k[i].reshape(-1), lnx_g[i], lnx_b[i], na_rpb[i], mix_w_out[i])
        else:
            x = _odd_layer(x, mod, norm_mix_g[l], conv_pw1_w[i], conv_pw1_b[i], conv_dw_w[i], conv_dw_b[i],
                           conv_ln_g[i], conv_ln_b[i], conv_pw2_w[i], conv_pw2_b[i])
        sh2, sc2, g2 = (mod[:, None, j * d:(j + 1) * d] for j in range(3, 6))
        x = _moe_layer(x, norm_ffn_g[l].reshape(1, d), sh2, sc2, g2, router_w[l], router_b[l], expert_w1[l],
                       expert_b1[l], expert_w2[l], expert_b2[l], final_norm_g, final=(l == depth - 1))
    return x
```

```python
import functools

import jax
import jax.numpy as jnp
import numpy as np
from jax import lax
from jax.experimental import pallas as pl
from jax.experimental.pallas import tpu as pltpu

F32 = jnp.float32
BF16 = jnp.bfloat16
HIGHEST = lax.Precision.HIGHEST

HEAD_DIM = 64
GRID_W = 64
NA_ROWS = 8
NA_COLS = 16
CONV_WIDTH = 31
N_EXPERTS = 32
TOP_K = 4
SWIGLU_ALPHA = 1.702
SWIGLU_LIMIT = 7.0
RMS_EPS = 1e-6
LN_EPS = 1e-5
GN_EPS = 64e-5
NEG_BIG = -1e30

VMEM_LIMIT_BYTES = 52 * 1024 * 1024
TOKEN_TILE = 256
SCAN_BLOCK = 16
EXPERT_ROWS = 256
HALO = 16
ISSUE_GROUP = 4


def _params(*sem):
    return pltpu.CompilerParams(dimension_semantics=sem, vmem_limit_bytes=VMEM_LIMIT_BYTES)


def _adaln(x, g, sh, sc):
    y = x * lax.rsqrt(jnp.mean(x * x, axis=-1, keepdims=True) + RMS_EPS)
    return (y * g) * (1.0 + sc) + sh


def _sigmoid(x):
    return 1.0 / (1.0 + jnp.exp(-x))


def _split_dot(x, m):
    hi = x.astype(BF16)
    r1 = x - hi.astype(F32)
    mid = r1.astype(BF16)
    lo = (r1 - mid.astype(F32)).astype(BF16)
    dot = functools.partial(jnp.dot, preferred_element_type=F32)
    return dot(hi, m) + dot(mid, m) + dot(lo, m)


def _mod_kernel(c_ref, w_ref, b_ref, o_ref):
    c = c_ref[...]
    s = c * _sigmoid(c)
    o_ref[0] = jnp.dot(s, w_ref[0], preferred_element_type=F32, precision=HIGHEST) + b_ref[0]


def _modulation(c_all, ada_w, ada_b):
    depth, d, n = ada_w.shape
    rows = c_all.shape[0]
    tn = 1536
    return pl.pallas_call(
        _mod_kernel,
        grid=(depth, n // tn),
        in_specs=[pl.BlockSpec((rows, d), lambda l, j: (0, 0)),
                  pl.BlockSpec((1, d, tn), lambda l, j: (l, 0, j)),
                  pl.BlockSpec((1, 1, tn), lambda l, j: (l, 0, j))],
        out_specs=pl.BlockSpec((1, rows, tn), lambda l, j: (l, 0, j)),
        out_shape=jax.ShapeDtypeStruct((depth, rows, n), F32),
        compiler_params=_params("arbitrary", "arbitrary"),
        name="modulation",
    )(c_all, ada_w, ada_b.reshape(depth, 1, n))


def _proj_kernel(x_ref, g_ref, sh_ref, sc_ref, w_ref, o_ref):
    h = _adaln(x_ref[0], g_ref[...], sh_ref[0], sc_ref[0]).astype(BF16)
    o_ref[0] = jnp.dot(h, w_ref[...], preferred_element_type=F32)


def _in_proj(x, g, sh, sc, w):
    b, t, d = x.shape
    n = w.shape[1]
    tt = min(TOKEN_TILE, t)
    return pl.pallas_call(
        _proj_kernel,
        grid=(b, t // tt),
        in_specs=[pl.BlockSpec((1, tt, d), lambda i, j: (i, j, 0)),
                  pl.BlockSpec((1, d), lambda i, j: (0, 0)),
                  pl.BlockSpec((1, 1, d), lambda i, j: (i, 0, 0)),
                  pl.BlockSpec((1, 1, d), lambda i, j: (i, 0, 0)),
                  pl.BlockSpec((d, n), lambda i, j: (0, 0))],
        out_specs=pl.BlockSpec((1, tt, n), lambda i, j: (i, j, 0)),
        out_shape=jax.ShapeDtypeStruct((b, t, n), F32),
        compiler_params=_params("arbitrary", "arbitrary"),
        name="in_proj",
    )(x, g, sh, sc, w)


def _terms_kernel(p_ref, pp_ref, pn_ref, mup_ref, mun_ref, w0_ref, w2_ref, a0_ref, a2_ref, g2_ref,
                  kk_ref, ka_ref, rk_ref, seg_ref, z_ref, bonus_ref, gate_ref, *, n_tiles):
    t = pl.program_id(1)
    p = p_ref[0]
    tt, aw = p.shape[0], kk_ref.shape[1]
    prev_row = jnp.where(t > 0, pp_ref[0, 7:8, :], 0.0)
    next_row = jnp.where(t < n_tiles - 1, pn_ref[0, 0:1, :], 0.0)
    rows = lax.broadcasted_iota(jnp.int32, p.shape, 0)
    prev = jnp.where(rows == 0, prev_row, pltpu.roll(p, 1, axis=0))
    nxt = jnp.where(rows == tt - 1, next_row, pltpu.roll(p, tt - 1, axis=0))
    s = p + mup_ref[...] * (prev - p) + mun_ref[...] * (nxt - p)
    r, k, v = s[:, :aw], s[:, aw:2 * aw], s[:, 2 * aw:3 * aw]
    lora = s[:, 3 * aw:]
    g_in, wd, ad = lora[:, 0:128], lora[:, 128:256], lora[:, 256:384]
    dotf = functools.partial(jnp.dot, preferred_element_type=F32, precision=HIGHEST)
    zw = -(w0_ref[...] + dotf(jnp.tanh(wd), w2_ref[...]))
    softplus = jnp.maximum(zw, 0.0) + jnp.log(1.0 + jnp.exp(-jnp.abs(zw)))
    decay = jnp.exp(-jnp.exp(-softplus - 0.5))
    a = _sigmoid(a0_ref[...] + dotf(ad, a2_ref[...]))
    seg = seg_ref[...]
    kk = k * kk_ref[...]
    kk = kk / jnp.maximum(jnp.sqrt(_split_dot(kk * kk, seg)), 1e-12)
    z_ref[0, 0] = kk
    z_ref[1, 0] = v
    z_ref[2, 0] = r
    kd_sum = jnp.zeros_like(k)
    for d in range(2):
        a_d = a[:, d * aw:(d + 1) * aw]
        k_dir = k * (1.0 + (a_d - 1.0) * ka_ref[...])
        z_ref[3 + 3 * d, 0] = decay[:, d * aw:(d + 1) * aw]
        z_ref[4 + 3 * d, 0] = k_dir
        z_ref[5 + 3 * d, 0] = kk * a_d
        kd_sum = kd_sum + k_dir
    bonus_ref[0] = _split_dot(r * kd_sum * rk_ref[...], seg) * v
    gate_ref[0] = jnp.dot(_sigmoid(g_in).astype(BF16), g2_ref[...], preferred_element_type=F32)


def _rwkv_terms(p, consts):
    b, t, _ = p.shape
    aw = consts["key_k"].shape[1]
    sw = 4 * aw
    tt = min(TOKEN_TILE, t)
    n_tiles = t // tt
    hb = tt // 8
    full = lambda a: pl.BlockSpec(a.shape, lambda i, j: (0,) * a.ndim)
    names = ("mu_prev", "mu_next", "w0", "w2", "a0", "a2", "g2", "key_k", "key_a", "r_k", "seg")
    cs = [consts[n] for n in names]
    out3 = jax.ShapeDtypeStruct((b, t, aw), F32)
    return pl.pallas_call(
        functools.partial(_terms_kernel, n_tiles=n_tiles),
        grid=(b, n_tiles),
        in_specs=[pl.BlockSpec((1, tt, sw), lambda i, j: (i, j, 0)),
                  pl.BlockSpec((1, 8, sw), lambda i, j: (i, jnp.maximum(j * hb - 1, 0), 0)),
                  pl.BlockSpec((1, 8, sw), lambda i, j: (i, jnp.minimum((j + 1) * hb, t // 8 - 1), 0))]
                 + [full(a) for a in cs],
        out_specs=[pl.BlockSpec((9, 1, tt, aw), lambda i, j: (0, i, j, 0)),
                   pl.BlockSpec((1, tt, aw), lambda i, j: (i, j, 0)),
                   pl.BlockSpec((1, tt, aw), lambda i, j: (i, j, 0))],
        out_shape=[jax.ShapeDtypeStruct((9, b, t, aw), F32), out3, out3],
        compiler_params=_params("arbitrary", "arbitrary"),
        name="rwkv_terms",
    )(p, p, p, *cs)


def _scan_kernel(csf_ref, cdf_ref, csb_ref, cdb_ref, zsf_ref, zdf_ref, zsb_ref, zdb_ref, yf_ref, yb_ref, s_ref,
                 *, tb, nc):
    g = pl.program_id(0)

    @pl.when(g == 0)
    def _():
        s_ref[...] = jnp.zeros_like(s_ref)

    n = s_ref.shape[1]

    def run(dirs):
        def step(tf, carry):
            tidx = (tf, tb - 1 - tf)
            vecs = []
            for d, (zs, zd, _) in enumerate(dirs):
                ti = tidx[d]
                kk, r = zs[ti, 0], zs[ti, 2]
                w, kd, bb = zd[ti, 0], zd[ti, 1], zd[ti, 2]
                bbr = jnp.sum(bb * r, axis=0, keepdims=True)
                kr = jnp.sum(kd * r, axis=0, keepdims=True)
                vecs.append((kk, w * r, w, bb, kd, bbr, kr))

            def row(i, c):
                for d, (zs, _, y_ref) in enumerate(dirs):
                    kk, wr, w, bb, kd, bbr, kr = vecs[d]
                    ti = tidx[d]
                    si = s_ref[d, i]
                    sa = -jnp.sum(si * kk, axis=0, keepdims=True)
                    vi = zs[ti, 1, pl.ds(i, 1), :]
                    s_ref[d, i] = si * w + sa * bb + vi * kd
                    if y_ref is not None:
                        y0 = jnp.sum(si * wr, axis=0, keepdims=True)
                        y_ref[ti, pl.ds(i, 1), :] = y0 + sa * bbr + vi * kr
                return c

            lax.fori_loop(0, n, row, 0, unroll=4)
            return carry

        lax.fori_loop(0, tb, step, 0)

    @pl.when(g < nc)
    def _():
        run(((csf_ref, cdf_ref, None), (csb_ref, cdb_ref, None)))

    @pl.when(g >= nc)
    def _():
        run(((zsf_ref, zdf_ref, yf_ref), (zsb_ref, zdb_ref, yb_ref)))


def _wkv_scan(zc, zm):
    n, lanes = zm.shape[2:]
    tb = SCAN_BLOCK
    nc, nm = zc.shape[0] // tb, zm.shape[0] // tb
    cf = lambda g: jnp.minimum(g, nc - 1)
    cb = lambda g: jnp.maximum(nc - 1 - g, 0)
    mf = lambda g: jnp.maximum(g - nc, 0)
    mb = lambda g: jnp.minimum(nm - 1, nm - 1 + nc - g)
    blk = (tb, 3, n, lanes)
    spec = lambda t_of, part: pl.BlockSpec(blk, lambda g: (t_of(g), part, 0, 0))
    y_shape = jax.ShapeDtypeStruct((nm * tb, n, lanes), F32)
    return pl.pallas_call(
        functools.partial(_scan_kernel, tb=tb, nc=nc),
        grid=(nc + nm,),
        in_specs=[spec(cf, 0), spec(cf, 1), spec(cb, 0), spec(cb, 2),
                  spec(mf, 0), spec(mf, 1), spec(mb, 0), spec(mb, 2)],
        out_specs=[pl.BlockSpec((tb, n, lanes), lambda g: (mf(g), 0, 0)),
                   pl.BlockSpec((tb, n, lanes), lambda g: (mb(g), 0, 0))],
        out_shape=[y_shape, y_shape],
        scratch_shapes=[pltpu.VMEM((2, n, n, lanes), F32)],
        compiler_params=_params("arbitrary"),
        name="wkv_scan",
    )(zc, zc, zc, zc, zm, zm, zm, zm)


def _to_scan_kernel(x_ref, o_ref):
    n_comp, nb, tt, aw = x_ref.shape
    n_head = aw // HEAD_DIM
    low = lax.broadcasted_iota(jnp.int32, (nb, 128), 1) < HEAD_DIM

    def comp(c, carry):
        for tp in range(tt // 2):
            a = x_ref[c, :, 2 * tp, :]
            b = x_ref[c, :, 2 * tp + 1, :]
            pieces = []
            for h in range(n_head):
                ls = slice((h // 2) * 128, (h // 2 + 1) * 128)
                am, bm = a[:, ls], b[:, ls]
                if h % 2 == 0:
                    pieces.append(jnp.where(low, am, pltpu.roll(bm, HEAD_DIM, axis=1)))
                else:
                    pieces.append(jnp.where(low, pltpu.roll(am, HEAD_DIM, axis=1), bm))
            r2 = jnp.concatenate(pieces, axis=0).T
            o_ref[2 * tp, c] = r2[:HEAD_DIM]
            o_ref[2 * tp + 1, c] = r2[HEAD_DIM:]
        return carry

    lax.fori_loop(0, n_comp, comp, 0)


def _to_scan_layout(z):
    n_comp, b, t, aw = z.shape
    tt = SCAN_BLOCK
    lanes = (aw // HEAD_DIM) * b
    return pl.pallas_call(
        _to_scan_kernel,
        grid=(t // tt,),
        in_specs=[pl.BlockSpec((n_comp, b, tt, aw), lambda i: (0, 0, i, 0))],
        out_specs=pl.BlockSpec((tt, n_comp, HEAD_DIM, lanes), lambda i: (i, 0, 0, 0)),
        out_shape=jax.ShapeDtypeStruct((t, n_comp, HEAD_DIM, lanes), F32),
        compiler_params=_params("arbitrary"),
        name="to_scan_layout",
    )(z)


def _from_scan_kernel(yf_ref, yb_ref, o_ref):
    tt = yf_ref.shape[0]
    nb, _, aw = o_ref.shape
    n_head = aw // HEAD_DIM
    low = lax.broadcasted_iota(jnp.int32, (nb, 128), 1) < HEAD_DIM
    for tp in range(tt // 2):
        s = jnp.concatenate([yf_ref[2 * tp] + yb_ref[2 * tp], yf_ref[2 * tp + 1] + yb_ref[2 * tp + 1]], axis=0)
        r2 = s.T
        for m in range(n_head // 2):
            pe = r2[(2 * m) * nb:(2 * m + 1) * nb]
            po = r2[(2 * m + 1) * nb:(2 * m + 2) * nb]
            ls = slice(m * 128, (m + 1) * 128)
            o_ref[:, 2 * tp, ls] = jnp.where(low, pe, pltpu.roll(po, HEAD_DIM, axis=1))
            o_ref[:, 2 * tp + 1, ls] = jnp.where(low, pltpu.roll(pe, HEAD_DIM, axis=1), po)


def _from_scan_layout(yf, yb, b):
    t, n, lanes = yf.shape
    aw = (lanes // b) * n
    tt = SCAN_BLOCK
    return pl.pallas_call(
        _from_scan_kernel,
        grid=(t // tt,),
        in_specs=[pl.BlockSpec((tt, n, lanes), lambda i: (i, 0, 0))] * 2,
        out_specs=pl.BlockSpec((b, tt, aw), lambda i: (0, i, 0)),
        out_shape=jax.ShapeDtypeStruct((b, t, aw), F32),
        compiler_params=_params("arbitrary"),
        name="from_scan_layout",
    )(yf, yb)


def _na_kernel(q_ref, k_ref, v_ref, kc_ref, vc_ref, bias_ref, o_ref, kb_ref, vb_ref, kcb_ref, vcb_ref, *, rows):
    kh = NA_ROWS
    dn = (((1,), (1,)), ((), ()))
    kb_ref[...] = k_ref[0].astype(BF16)
    vb_ref[...] = v_ref[0].astype(BF16)
    kcb_ref[...] = kc_ref[0].astype(BF16)
    vcb_ref[...] = vc_ref[0].astype(BF16)
    head_of_lane = lax.broadcasted_iota(jnp.int32, (GRID_W, 2 * HEAD_DIM), 1) // HEAD_DIM

    def row(r, c):
        r_start = jnp.clip(r - kh // 2, 0, rows - kh)
        off = r - r_start
        q0 = pl.multiple_of(r * GRID_W, GRID_W)
        k0 = pl.multiple_of(r_start * GRID_W, GRID_W)
        q2 = q_ref[0, pl.ds(q0, GRID_W), :] * (HEAD_DIM ** -0.5)
        kl = kb_ref[pl.ds(k0, kh * GRID_W), :]
        vl = vb_ref[pl.ds(k0, kh * GRID_W), :]
        out = jnp.zeros((GRID_W, 2 * HEAD_DIM), F32)
        for hh in range(2):
            q = jnp.where(head_of_lane == hh, q2, 0.0).astype(BF16)
            s_loc = lax.dot_general(q, kl, dn, preferred_element_type=F32) + bias_ref[off, hh]
            s_ctx = lax.dot_general(q, kcb_ref[...], dn, preferred_element_type=F32)
            m = jnp.maximum(jnp.max(s_loc, axis=-1, keepdims=True), jnp.max(s_ctx, axis=-1, keepdims=True))
            e_loc = jnp.exp(s_loc - m)
            e_ctx = jnp.exp(s_ctx - m)
            den = jnp.sum(e_loc, axis=-1, keepdims=True) + jnp.sum(e_ctx, axis=-1, keepdims=True)
            o = (jnp.dot(e_loc.astype(BF16), vl, preferred_element_type=F32)
                 + jnp.dot(e_ctx.astype(BF16), vcb_ref[...], preferred_element_type=F32))
            out = jnp.where(head_of_lane == hh, o / den, out)
        o_ref[0, pl.ds(q0, GRID_W), :] = out.astype(o_ref.dtype)
        return c

    lax.fori_loop(0, rows, row, 0, unroll=2)


def _neighbourhood_attention(p, pc, bias, col_q, col_k, col_v):
    b, t, _ = p.shape
    l = pc.shape[1]
    rows = t // GRID_W
    n_pairs = bias.shape[1] // 2
    n_keys = NA_ROWS * GRID_W
    return pl.pallas_call(
        functools.partial(_na_kernel, rows=rows),
        grid=(b, n_pairs),
        in_specs=[pl.BlockSpec((1, t, 128), lambda i, h: (i, 0, col_q + h)),
                  pl.BlockSpec((1, t, 128), lambda i, h: (i, 0, col_k + h)),
                  pl.BlockSpec((1, t, 128), lambda i, h: (i, 0, col_v + h)),
                  pl.BlockSpec((1, l, 128), lambda i, h: (i, 0, col_k + h)),
                  pl.BlockSpec((1, l, 128), lambda i, h: (i, 0, col_v + h)),
                  pl.BlockSpec((NA_ROWS, 2, GRID_W, n_keys), lambda i, h: (0, h, 0, 0))],
        out_specs=pl.BlockSpec((1, t, 128), lambda i, h: (i, 0, h)),
        out_shape=jax.ShapeDtypeStruct((b, t, n_pairs * 128), BF16),
        scratch_shapes=[pltpu.VMEM((t, 128), BF16), pltpu.VMEM((t, 128), BF16),
                        pltpu.VMEM((l, 128), BF16), pltpu.VMEM((l, 128), BF16)],
        compiler_params=_params("arbitrary", "arbitrary"),
        name="na_attention",
    )(p, p, p, pc, pc, bias)


def _na_bias_table(rpb):
    h = rpb.shape[0]
    col = np.arange(GRID_W)
    c_start = np.clip(col - NA_COLS // 2, 0, GRID_W - NA_COLS)
    col_ok = (col[None, :] >= c_start[:, None]) & (col[None, :] < c_start[:, None] + NA_COLS)
    dc = np.clip(col[None, :] - col[:, None], 1 - NA_COLS, NA_COLS - 1) + NA_COLS - 1
    pick = (dc.reshape(1, -1) == np.arange(2 * NA_COLS - 1)[:, None]).astype(np.float32)
    t = jnp.einsum("hrc,cx->hrx", rpb, pick, precision=HIGHEST)
    t = jnp.where(col_ok.reshape(-1), t, NEG_BIG).reshape(h, 2 * NA_ROWS - 1, GRID_W, GRID_W)
    tab = jnp.stack([t[:, NA_ROWS - 1 - off:2 * NA_ROWS - 1 - off] for off in range(NA_ROWS)], axis=0)
    return tab.transpose(0, 1, 3, 2, 4).reshape(NA_ROWS, h, GRID_W, NA_ROWS * GRID_W)


def _mix_out_kernel(y_ref, bonus_ref, gate_ref, ob_ref, x_ref, g1_ref, lg_ref, lb_ref, seg_ref, wa_ref, wb_ref, o_ref):
    y = y_ref[0]
    seg = seg_ref[...]
    inv = 1.0 / HEAD_DIM
    mu = _split_dot(y, seg) * inv
    yc = y - mu
    var = _split_dot(yc * yc, seg) * inv
    yn = (yc * lax.rsqrt(var + GN_EPS)) * lg_ref[...] + lb_ref[...]
    o_a = ((yn + bonus_ref[0]) * gate_ref[0]).astype(BF16)
    out = (jnp.dot(o_a, wa_ref[...], preferred_element_type=F32)
           + jnp.dot(ob_ref[0], wb_ref[...], preferred_element_type=F32))
    o_ref[0] = x_ref[0] + g1_ref[0] * out


def _mix_out(y, bonus, gate, o_b, x, g1, lnx_g, lnx_b, seg, w_a, w_b):
    b, t, d = x.shape
    aw = y.shape[2]
    tt = min(TOKEN_TILE, t)
    tok = lambda w: pl.BlockSpec((1, tt, w), lambda i, j: (i, j, 0))
    full = lambda a: pl.BlockSpec(a.shape, lambda i, j: (0,) * a.ndim)
    return pl.pallas_call(
        _mix_out_kernel,
        grid=(b, t // tt),
        in_specs=[tok(aw), tok(aw), tok(aw), tok(o_b.shape[2]), tok(d),
                  pl.BlockSpec((1, 1, d), lambda i, j: (i, 0, 0)),
                  full(lnx_g), full(lnx_b), full(seg), full(w_a), full(w_b)],
        out_specs=tok(d),
        out_shape=jax.ShapeDtypeStruct((b, t, d), F32),
        compiler_params=_params("arbitrary", "arbitrary"),
        name="mix_out",
    )(y, bonus, gate, o_b, x, g1, lnx_g, lnx_b, seg, w_a, w_b)


def _glu_kernel(x_ref, g_ref, sh_ref, sc_ref, w_ref, b_ref, o_ref):
    h = _adaln(x_ref[0], g_ref[...], sh_ref[0], sc_ref[0]).astype(BF16)
    u = jnp.dot(h, w_ref[...], preferred_element_type=F32) + b_ref[...]
    d = u.shape[1] // 2
    o_ref[0] = u[:, :d] * _sigmoid(u[:, d:])


def _glu_proj(x, g, sh, sc, w, bias):
    b, t, d = x.shape
    n = w.shape[1]
    tt = min(TOKEN_TILE, t)
    return pl.pallas_call(
        _glu_kernel,
        grid=(b, t // tt),
        in_specs=[pl.BlockSpec((1, tt, d), lambda i, j: (i, j, 0)),
                  pl.BlockSpec((1, d), lambda i, j: (0, 0)),
                  pl.BlockSpec((1, 1, d), lambda i, j: (i, 0, 0)),
                  pl.BlockSpec((1, 1, d), lambda i, j: (i, 0, 0)),
                  pl.BlockSpec((d, n), lambda i, j: (0, 0)),
                  pl.BlockSpec((1, n), lambda i, j: (0, 0))],
        out_specs=pl.BlockSpec((1, tt, n // 2), lambda i, j: (i, j, 0)),
        out_shape=jax.ShapeDtypeStruct((b, t, n // 2), F32),
        compiler_params=_params("arbitrary", "arbitrary"),
        name="glu_proj",
    )(x, g, sh, sc, w, bias)


def _conv_kernel(u_ref, up_ref, un_ref, dw_ref, dwb_ref, lg_ref, lb_ref, w2_ref, b2_ref, x_ref, g1_ref, o_ref,
                 win_ref, acc_ref, *, n_tiles):
    t = pl.program_id(1)
    tt, d = u_ref.shape[1], u_ref.shape[2]
    half = CONV_WIDTH // 2
    win_ref[0:HALO, :] = jnp.where(t > 0, up_ref[0], 0.0)
    win_ref[HALO:HALO + tt, :] = u_ref[0]
    win_ref[HALO + tt:2 * HALO + tt, :] = jnp.where(t < n_tiles - 1, un_ref[0], 0.0)
    rc = 32

    for base in range(0, tt, rc):
        for lc in range(d // 128):
            ls = slice(lc * 128, (lc + 1) * 128)
            acc = jnp.zeros((rc, 128), F32)
            for k in range(CONV_WIDTH):
                lo = base + HALO - half + k
                acc = acc + dw_ref[k:k + 1, ls] * win_ref[lo:lo + rc, ls]
            acc_ref[base:base + rc, ls] = acc
    u = acc_ref[...] + dwb_ref[...]
    mu = jnp.mean(u, axis=-1, keepdims=True)
    uc = u - mu
    var = jnp.mean(uc * uc, axis=-1, keepdims=True)
    un = (uc * lax.rsqrt(var + LN_EPS)) * lg_ref[...] + lb_ref[...]
    act = (un * _sigmoid(un)).astype(BF16)
    out = jnp.dot(act, w2_ref[...], preferred_element_type=F32) + b2_ref[...]
    o_ref[0] = x_ref[0] + g1_ref[0] * out


def _conv_module(u, dw, dwb, ln_g, ln_b, w2, b2, x, g1):
    b, t, d = x.shape
    tt = min(TOKEN_TILE, t)
    n_tiles = t // tt
    hb = tt // HALO
    tok = pl.BlockSpec((1, tt, d), lambda i, j: (i, j, 0))
    full = lambda a: pl.BlockSpec(a.shape, lambda i, j: (0,) * a.ndim)
    return pl.pallas_call(
        functools.partial(_conv_kernel, n_tiles=n_tiles),
        grid=(b, n_tiles),
        in_specs=[tok,
                  pl.BlockSpec((1, HALO, d), lambda i, j: (i, jnp.maximum(j * hb - 1, 0), 0)),
                  pl.BlockSpec((1, HALO, d), lambda i, j: (i, jnp.minimum((j + 1) * hb, t // HALO - 1), 0)),
                  full(dw), full(dwb), full(ln_g), full(ln_b), full(w2), full(b2), tok,
                  pl.BlockSpec((1, 1, d), lambda i, j: (i, 0, 0))],
        out_specs=tok,
        out_shape=jax.ShapeDtypeStruct((b, t, d), F32),
        scratch_shapes=[pltpu.VMEM((tt + 2 * HALO, d), F32), pltpu.VMEM((tt, d), F32)],
        compiler_params=_params("arbitrary", "arbitrary"),
        name="conv_module",
    )(u, u, u, dw, dwb, ln_g, ln_b, w2, b2, x, g1)


def _route_kernel(x_ref, g_ref, sh_ref, sc_ref, wr_ref, br_ref, hp_ref, route_ref, cnt_ref, carry_ref):
    @pl.when((pl.program_id(0) == 0) & (pl.program_id(1) == 0))
    def _():
        carry_ref[...] = jnp.zeros_like(carry_ref)

    h = _adaln(x_ref[0], g_ref[...], sh_ref[0], sc_ref[0])
    tt, d = h.shape
    hi = lax.bitcast_convert_type(h[:, :d // 2].astype(BF16).astype(F32), jnp.uint32)
    lo = lax.bitcast_convert_type(h[:, d // 2:].astype(BF16).astype(F32), jnp.uint32)
    packed = (hi & jnp.uint32(0xFFFF0000)) | (lo >> 16)
    n_ch = d // 2 // 128
    for c in range(n_ch):
        hp_ref[0, pl.ds(c, tt, stride=n_ch), :] = packed[:, c * 128:(c + 1) * 128]

    logits = jnp.dot(h, wr_ref[...], preferred_element_type=F32, precision=HIGHEST) + br_ref[...]
    ne = logits.shape[1]
    lane = lax.broadcasted_iota(jnp.int32, (tt, ne), 1).astype(F32)
    work = logits
    mask = jnp.zeros((tt, ne), F32)
    picks, es = [], []
    den = jnp.zeros((tt, 1), F32)
    for k in range(TOP_K):
        m = jnp.max(work, axis=-1, keepdims=True)
        idx = jnp.min(jnp.where(work == m, lane, float(ne)), axis=-1, keepdims=True)
        pick = lane == idx
        if k == 0:
            top = m
        e = jnp.exp(m - top)
        den = den + e
        picks.append((pick, idx))
        es.append(e)
        mask = jnp.where(pick, 1.0, mask)
        work = jnp.where(pick, -jnp.inf, work)

    ri = lax.broadcasted_iota(jnp.int32, (tt, tt), 0)
    ci = lax.broadcasted_iota(jnp.int32, (tt, tt), 1)
    lower = jnp.where(ci < ri, 1.0, 0.0).astype(BF16)
    rank = jnp.dot(lower, mask.astype(BF16), preferred_element_type=F32) + carry_ref[...]
    carry_ref[...] = carry_ref[...] + jnp.sum(mask, axis=0, keepdims=True)
    cnt_ref[...] = carry_ref[...]

    out_lane = lax.broadcasted_iota(jnp.int32, (tt, 128), 1)
    route = jnp.zeros((tt, 128), F32)
    for k in range(TOP_K):
        pick, idx = picks[k]
        rk = jnp.sum(jnp.where(pick, rank, 0.0), axis=-1, keepdims=True)
        route = jnp.where(out_lane == k, idx, route)
        route = jnp.where(out_lane == TOP_K + k, rk, route)
        route = jnp.where(out_lane == 2 * TOP_K + k, es[k] / den, route)
    route_ref[0] = route


def _route(x, g, sh, sc, w_r, b_r):
    b, t, d = x.shape
    ne = w_r.shape[1]
    tt = min(TOKEN_TILE, t)
    return pl.pallas_call(
        _route_kernel,
        grid=(b, t // tt),
        in_specs=[pl.BlockSpec((1, tt, d), lambda i, j: (i, j, 0)),
                  pl.BlockSpec((1, d), lambda i, j: (0, 0)),
                  pl.BlockSpec((1, 1, d), lambda i, j: (i, 0, 0)),
                  pl.BlockSpec((1, 1, d), lambda i, j: (i, 0, 0)),
                  pl.BlockSpec((d, ne), lambda i, j: (0, 0)),
                  pl.BlockSpec((1, ne), lambda i, j: (0, 0))],
        out_specs=[pl.BlockSpec((1, tt * (d // 256), 128), lambda i, j: (i, j, 0)),
                   pl.BlockSpec((1, tt, 128), lambda i, j: (i, j, 0)),
                   pl.BlockSpec((1, ne), lambda i, j: (0, 0))],
        out_shape=[jax.ShapeDtypeStruct((b, t * (d // 256), 128), jnp.uint32),
                   jax.ShapeDtypeStruct((b, t, 128), F32),
                   jax.ShapeDtypeStruct((1, ne), F32)],
        scratch_shapes=[pltpu.VMEM((1, ne), F32)],
        compiler_params=_params("arbitrary", "arbitrary"),
        name="moe_route",
    )(x, g, sh, sc, w_r, b_r)


def _dispatch_kernel(pos_hbm, hp_ref, xs_in, xs_out, idx_ref, sem_idx, sem_rows, *, tile, n_tiles, rc):
    del xs_in
    n_idx = tile * TOP_K

    def idx_copy(i, slot):
        return pltpu.make_async_copy(pos_hbm.at[i], idx_ref.at[pl.ds(slot * n_idx, n_idx)], sem_idx.at[slot])

    i = pl.program_id(0)
    slot = i % 2

    @pl.when(i == 0)
    def _():
        idx_copy(0, 0).start()

    idx_copy(i, slot).wait()

    @pl.when(i + 1 < n_tiles)
    def _():
        idx_copy(i + 1, 1 - slot).start()

    ibase = slot * n_idx

    def issue(jg, c2):
        j0 = jg * ISSUE_GROUP
        rows = [idx_ref[ibase + j0 * TOP_K + q] for q in range(ISSUE_GROUP * TOP_K)]
        for q, row in enumerate(rows):
            src = hp_ref.at[pl.ds(pl.multiple_of((j0 + q // TOP_K) * rc, rc), rc)]
            copy = pltpu.make_async_copy(src, xs_out.at[pl.ds(pl.multiple_of(row * rc, rc), rc)], sem_rows)
            copy.start(priority=q % 2)
        return c2

    lax.fori_loop(0, tile // ISSUE_GROUP, issue, 0)
    for _ in range(TOP_K):
        pltpu.make_async_copy(hp_ref, xs_out.at[pl.ds(0, tile * rc)], sem_rows).wait()


def _dispatch(pos_flat, hp, n_rows, rc):
    n, w = hp.shape[0] // rc, hp.shape[1]
    tile = TOKEN_TILE
    xs0 = jnp.zeros((n_rows * rc, w), jnp.uint32)
    any_spec = pl.BlockSpec(memory_space=pl.ANY)
    return pl.pallas_call(
        functools.partial(_dispatch_kernel, tile=tile, n_tiles=n // tile, rc=rc),
        grid=(n // tile,),
        in_specs=[any_spec, pl.BlockSpec((tile * rc, w), lambda i: (i, 0)), any_spec],
        out_specs=any_spec,
        out_shape=jax.ShapeDtypeStruct((n_rows * rc, w), jnp.uint32),
        scratch_shapes=[pltpu.SMEM((2 * tile * TOP_K,), jnp.int32),
                        pltpu.SemaphoreType.DMA((2,)), pltpu.SemaphoreType.DMA],
        input_output_aliases={2: 0},
        compiler_params=_params("arbitrary"),
        name="moe_dispatch",
    )(pos_flat.reshape(n // tile, tile * TOP_K), hp, xs0)


def _split_w1_kernel(w_ref, p_ref, g_ref, l_ref):
    w = w_ref[0].astype(BF16)
    n = w.shape[1]
    for c in range(n // 256):
        res = jnp.dot(w[:, c * 256:(c + 1) * 256], p_ref[...], preferred_element_type=F32)
        g_ref[0, :, c * 128:(c + 1) * 128] = res[:, :128].astype(BF16)
        l_ref[0, :, c * 128:(c + 1) * 128] = res[:, 128:].astype(BF16)


def _split_w1(w1):
    ne, d, f2 = w1.shape
    rows = 512
    r = np.arange(256)[:, None]
    c = np.arange(256)[None, :]
    sel = jnp.asarray(np.where(c < 128, r == 2 * c, r == 2 * (c - 128) + 1), BF16)
    out = jax.ShapeDtypeStruct((ne, d, f2 // 2), BF16)
    return pl.pallas_call(
        _split_w1_kernel,
        grid=(ne, d // rows),
        in_specs=[pl.BlockSpec((1, rows, f2), lambda e, i: (e, i, 0)),
                  pl.BlockSpec((256, 256), lambda e, i: (0, 0))],
        out_specs=[pl.BlockSpec((1, rows, f2 // 2), lambda e, i: (e, i, 0))] * 2,
        out_shape=[out, out],
        compiler_params=_params("arbitrary", "arbitrary"),
        name="split_w1",
    )(w1, sel)


def _expert_kernel(be_ref, nb_ref, xs_ref, w1g_ref, w1l_ref, b1g_ref, b1l_ref, w2_ref, b2_ref, ys_ref, *, bm):
    del be_ref
    rc = xs_ref.shape[0] // bm
    oc = ys_ref.shape[0] // bm

    @pl.when(pl.program_id(0) < nb_ref[0])
    def _():
        u = jnp.concatenate([xs_ref[pl.ds(c, bm, stride=rc), :] for c in range(rc)], axis=1)
        half = u.shape[1]
        xa = lax.bitcast_convert_type(u & jnp.uint32(0xFFFF0000), F32).astype(BF16)
        xb = lax.bitcast_convert_type(u << 16, F32).astype(BF16)
        dot = functools.partial(jnp.dot, preferred_element_type=F32)
        ug = dot(xa, w1g_ref[0, :half, :]) + dot(xb, w1g_ref[0, half:, :]) + b1g_ref[0]
        ul = dot(xa, w1l_ref[0, :half, :]) + dot(xb, w1l_ref[0, half:, :]) + b1l_ref[0]
        glu = jnp.minimum(ug, SWIGLU_LIMIT)
        lin = jnp.clip(ul, -SWIGLU_LIMIT, SWIGLU_LIMIT)
        act = (glu * _sigmoid(SWIGLU_ALPHA * glu)) * (lin + 1.0)
        y = dot(act.astype(BF16), w2_ref[0]) + b2_ref[0]
        for c in range(oc):
            ys_ref[pl.ds(c, bm, stride=oc), :] = y[:, c * 128:(c + 1) * 128]

    @pl.when(pl.program_id(0) >= nb_ref[0])
    def _():
        ys_ref[...] = jnp.zeros_like(ys_ref)


def _expert_ffn(block_e, n_used, xs, w1g, w1l, b1g, b1l, w2, b2, rc):
    n_rows = xs.shape[0] // rc
    ne, d, f = w1g.shape
    oc = d // 128
    bm = EXPERT_ROWS
    n_blocks = n_rows // bm
    wspec = lambda s: pl.BlockSpec((1,) + s, lambda i, be, nb: (be[i], 0, 0))
    return pl.pallas_call(
        functools.partial(_expert_kernel, bm=bm),
        grid_spec=pltpu.PrefetchScalarGridSpec(
            num_scalar_prefetch=2,
            grid=(n_blocks,),
            in_specs=[pl.BlockSpec((bm * rc, 128), lambda i, be, nb: (i, 0)),
                      wspec((d, f)), wspec((d, f)), wspec((1, f)), wspec((1, f)), wspec((f, d)), wspec((1, d))],
            out_specs=pl.BlockSpec((bm * oc, 128), lambda i, be, nb: (i, 0)),
        ),
        out_shape=jax.ShapeDtypeStruct((n_rows * oc, 128), F32),
        compiler_params=_params("arbitrary"),
        name="moe_experts",
    )(block_e, n_used, xs, w1g, w1l, b1g, b1l, w2, b2)


def _combine_kernel(pos_hbm, ys_hbm, x_ref, route_ref, g2_ref, fg_ref, o_ref, buf_ref, idx_ref, sem_idx, sem_rows,
                    *, tile, n_tiles, final):
    i = pl.program_id(0)
    n_idx = tile * TOP_K
    slot = i % 2
    oc = x_ref.shape[1] // 128

    def idx_copy(t, s):
        src = pos_hbm.at[pl.ds(pl.multiple_of(t * n_idx, n_idx), n_idx)]
        return pltpu.make_async_copy(src, idx_ref.at[pl.ds(s * n_idx, n_idx)], sem_idx.at[s])

    def gather(s):
        ibase = s * n_idx

        def issue(jg, c):
            j0 = jg * ISSUE_GROUP
            rows = [idx_ref[ibase + j0 * TOP_K + q] for q in range(ISSUE_GROUP * TOP_K)]
            for q, row in enumerate(rows):
                dst_row = pl.multiple_of((j0 + q // TOP_K) * oc, oc)
                copy = pltpu.make_async_copy(ys_hbm.at[pl.ds(pl.multiple_of(row * oc, oc), oc)],
                                             buf_ref.at[s, q % TOP_K, pl.ds(dst_row, oc)], sem_rows.at[s])
                copy.start(priority=q % 2)
            return c
        lax.fori_loop(0, tile // ISSUE_GROUP, issue, 0)

    @pl.when(i == 0)
    def _():
        idx_copy(0, 0).start()
        idx_copy(0, 0).wait()
        gather(0)
        if n_tiles > 1:
            idx_copy(1, 1).start()

    for k in range(TOP_K):
        pltpu.make_async_copy(ys_hbm.at[pl.ds(0, tile * oc)], buf_ref.at[slot, k], sem_rows.at[slot]).wait()

    @pl.when(i + 1 < n_tiles)
    def _():
        idx_copy(i + 1, 1 - slot).wait()
        gather(1 - slot)

    @pl.when(i + 2 < n_tiles)
    def _():
        idx_copy(i + 2, slot).start()

    route = route_ref[...]
    chunks = []
    for c in range(oc):
        acc = jnp.zeros((tile, 128), F32)
        for k in range(TOP_K):
            acc = acc + buf_ref[slot, k, pl.ds(c, tile, stride=oc), :] * route[:, 2 * TOP_K + k:2 * TOP_K + k + 1]
        chunks.append(acc)
    x = x_ref[...] + g2_ref[0] * jnp.concatenate(chunks, axis=1)
    if final:
        x = (x * lax.rsqrt(jnp.mean(x * x, axis=-1, keepdims=True) + RMS_EPS)) * fg_ref[...]
    o_ref[...] = x


def _combine(pos_flat, ys, x2, route2, g2, final_g, tiles_per_batch, final):
    n, d = x2.shape
    tile = TOKEN_TILE
    return pl.pallas_call(
        functools.partial(_combine_kernel, tile=tile, n_tiles=n // tile, final=final),
        grid=(n // tile,),
        in_specs=[pl.BlockSpec(memory_space=pl.ANY),
                  pl.BlockSpec(memory_space=pl.ANY),
                  pl.BlockSpec((tile, d), lambda i: (i, 0)),
                  pl.BlockSpec((tile, 128), lambda i: (i, 0)),
                  pl.BlockSpec((1, 1, d), lambda i: (i // tiles_per_batch, 0, 0)),
                  pl.BlockSpec((1, d), lambda i: (0, 0))],
        out_specs=pl.BlockSpec((tile, d), lambda i: (i, 0)),
        out_shape=jax.ShapeDtypeStruct((n, d), F32),
        scratch_shapes=[pltpu.VMEM((2, TOP_K, tile * (d // 128), 128), F32),
                        pltpu.SMEM((2 * tile * TOP_K,), jnp.int32),
                        pltpu.SemaphoreType.DMA((2,)), pltpu.SemaphoreType.DMA((2,))],
        compiler_params=_params("arbitrary"),
        name="moe_combine",
    )(pos_flat, ys, x2, route2, g2, final_g)


def _moe_layer(x, g, sh, sc, gate2, w_r, b_r, w1, b1, w2, b2, final_g, final):
    b, t, d = x.shape
    n = b * t
    ne = w_r.shape[1]
    bm = EXPERT_ROWS
    hp, route, counts = _route(x, g, sh, sc, w_r, b_r.reshape(1, ne))
    route2 = route.reshape(n, 128)

    counts = counts[0].astype(jnp.int32)
    padded = (counts + bm - 1) // bm * bm
    pad_end = jnp.cumsum(padded)
    pad_start = pad_end - padded
    n_blocks = -(-(n * TOP_K + ne * (bm - 1)) // bm)
    e_idx = route2[:, :TOP_K].astype(jnp.int32)
    experts = jnp.arange(ne, dtype=jnp.int32)
    start_of = jnp.sum(jnp.where(e_idx[:, :, None] == experts, pad_start, 0), axis=-1)
    pos_flat = (start_of + route2[:, TOP_K:2 * TOP_K].astype(jnp.int32)).reshape(-1)
    block_row = jnp.arange(n_blocks, dtype=jnp.int32) * bm
    block_e = jnp.minimum(jnp.sum((pad_end[None, :] <= block_row[:, None]).astype(jnp.int32), axis=-1), ne - 1)
    n_used = (pad_end[-1:] // bm).astype(jnp.int32)

    rc = d // 256
    xs = _dispatch(pos_flat, hp.reshape(n * rc, 128), n_blocks * bm, rc)
    f = w2.shape[1]
    w1g, w1l = _split_w1(w1)
    b1g = b1[:, 0::2].reshape(ne, 1, f)
    b1l = b1[:, 1::2].reshape(ne, 1, f)
    ys = _expert_ffn(block_e, n_used, xs, w1g, w1l, b1g, b1l, w2.astype(BF16), b2.reshape(ne, 1, d), rc)
    out = _combine(pos_flat, ys, x.reshape(n, d), route2, gate2, final_g.reshape(1, d), t // TOKEN_TILE, final)
    return out.reshape(b, t, d)


def _even_layer(x, ctx, mod, mod_c, norm_g, w_in, mu_prev, mu_next, w0, w2, a0, a2, g2, key_k, key_a, r_k,
                lnx_g, lnx_b, rpb, w_out):
    b, t, d = x.shape
    l = ctx.shape[1]
    aw = key_k.shape[0]
    dl = w2.shape[1]
    bw = (w_in.shape[1] - 3 * aw - 128 - 4 * dl) // 3
    n_heads = aw // HEAD_DIM
    sh1, sc1, g1 = (mod[:, None, i * d:(i + 1) * d] for i in range(3))
    shc = jnp.broadcast_to(mod_c[None, None, :d], (b, 1, d))
    scc = jnp.broadcast_to(mod_c[None, None, d:2 * d], (b, 1, d))

    c_ra, c_gd = bw, bw + aw
    c_ka = c_gd + 128
    c_va = c_ka + aw
    c_wd = c_va + aw
    c_ad = c_wd + 2 * dl
    c_kb = c_ad + 2 * dl
    c_vb = c_kb + bw
    cols = lambda a, lo, hi: a[..., lo:hi]
    pad = jnp.zeros((d, 128), F32)
    w_p = jnp.concatenate([cols(w_in, c_ra, c_gd), cols(w_in, c_ka, c_va), cols(w_in, c_va, c_wd),
                           cols(w_in, c_gd, c_ka), cols(w_in, c_wd, c_ad), cols(w_in, c_ad, c_kb), pad,
                           cols(w_in, 0, c_ra), cols(w_in, c_kb, c_vb), cols(w_in, c_vb, c_vb + bw)],
                          axis=1).astype(BF16)

    def shift_vec(mu):
        o = lambda c: c - c_ra
        return jnp.concatenate([mu[o(c_ra):o(c_gd)], mu[o(c_ka):o(c_va)], mu[o(c_va):o(c_wd)], mu[o(c_gd):o(c_ka)],
                                mu[o(c_wd):o(c_ad)], mu[o(c_ad):o(c_kb)], jnp.zeros((128,), F32)]).reshape(1, -1)

    blockdiag = lambda m: jnp.concatenate(
        [jnp.concatenate([m[0], jnp.zeros_like(m[0])], axis=1),
         jnp.concatenate([jnp.zeros_like(m[1]), m[1]], axis=1)], axis=0)
    head = jnp.arange(aw) // HEAD_DIM
    consts = {
        "mu_prev": shift_vec(mu_prev), "mu_next": shift_vec(mu_next),
        "w0": w0.reshape(1, 2 * aw), "w2": blockdiag(w2), "a0": a0.reshape(1, 2 * aw), "a2": blockdiag(a2),
        "g2": g2.astype(BF16), "key_k": key_k.reshape(1, aw), "key_a": key_a.reshape(1, aw),
        "r_k": r_k.reshape(1, aw), "seg": (head[:, None] == head[None, :]).astype(BF16),
    }

    g_row = norm_g.reshape(1, d)
    p = _in_proj(x, g_row, sh1, sc1, w_p)
    pc = _in_proj(ctx, g_row, shc, scc, w_p)
    z_m, bonus, gate = _rwkv_terms(p, consts)
    z_c, _, _ = _rwkv_terms(pc, consts)

    yf, yb = _wkv_scan(_to_scan_layout(z_c), _to_scan_layout(z_m))
    y = _from_scan_layout(yf, yb, b)

    qb = (4 * aw) // 128
    o_b = _neighbourhood_attention(p, pc, _na_bias_table(rpb), qb, qb + bw // 128, qb + 2 * bw // 128)
    return _mix_out(y, bonus, gate, o_b, x, g1, lnx_g.reshape(1, aw), lnx_b.reshape(1, aw), consts["seg"],
                    w_out[:aw].astype(BF16), w_out[aw:].astype(BF16))


def _odd_layer(x, mod, norm_g, pw1_w, pw1_b, dw_w, dw_b, ln_g, ln_b, pw2_w, pw2_b):
    b, t, d = x.shape
    sh1, sc1, g1 = (mod[:, None, i * d:(i + 1) * d] for i in range(3))
    u = _glu_proj(x, norm_g.reshape(1, d), sh1, sc1, pw1_w.astype(BF16), pw1_b.reshape(1, -1))
    dw = jnp.concatenate([dw_w, jnp.zeros((1, d), F32)], axis=0)
    return _conv_module(u, dw, dw_b.reshape(1, d), ln_g.reshape(1, d), ln_b.reshape(1, d), pw2_w.astype(BF16),
                        pw2_b.reshape(1, d), x, g1)


def kernel(x, c, ctx, c_ctx, ada_w, ada_b, norm_mix_g, norm_ffn_g, final_norm_g, mix_w_in, shift_mu_prev, shift_mu_next, decay_w0, decay_w2, iclr_a0, iclr_a2, gate_g2, key_k, key_a, bonus_r_k, lnx_g, lnx_b, na_rpb, mix_w_out, conv_pw1_w, conv_pw1_b, conv_dw_w, conv_dw_b, conv_ln_g, conv_ln_b, conv_pw2_w, conv_pw2_b, router_w, router_b, expert_w1, expert_b1, expert_w2, expert_b2):
    b, t, d = x.shape
    depth = ada_w.shape[0]
    rows = -(-(b + 1) // 8) * 8
    c_all = jnp.concatenate([c, c_ctx[None, :], jnp.zeros((rows - b - 1, d), F32)], axis=0)
    mod_all = _modulation(c_all, ada_w, ada_b)
    for l in range(depth):
        mod = mod_all[l, :b]
        i = l // 2
        if l % 2 == 0:
            x = _even_layer(x, ctx, mod, mod_all[l, b], norm_mix_g[l], mix_w_in[i], shift_mu_prev[i], shift_mu_next[i],
                            decay_w0[i], decay_w2[i], iclr_a0[i], iclr_a2[i], gate_g2[i], key_k[i], key_a[i],
                            bonus_r_k[i].reshape(-1), lnx_g[i], lnx_b[i], na_rpb[i], mix_w_out[i])
        else:
            x = _odd_layer(x, mod, norm_mix_g[l], conv_pw1_w[i], conv_pw1_b[i], conv_dw_w[i], conv_dw_b[i],
                           conv_ln_g[i], conv_ln_b[i], conv_pw2_w[i], conv_pw2_b[i])
        sh2, sc2, g2 = (mod[:, None, j * d:(j + 1) * d] for j in range(3, 6))
        x = _moe_layer(x, norm_ffn_g[l].reshape(1, d), sh2, sc2, g2, router_w[l], router_b[l], expert_w1[l],
                       expert_b1[l], expert_w2[l], expert_b2[l], final_norm_g, final=(l == depth - 1))
    return x
```

```python
import functools

import jax
import jax.numpy as jnp
import numpy as np
from jax import lax
from jax.experimental import pallas as pl
from jax.experimental.pallas import tpu as pltpu

F32 = jnp.float32
BF16 = jnp.bfloat16
HIGHEST = lax.Precision.HIGHEST

HEAD_DIM = 64
GRID_W = 64
NA_ROWS = 8
NA_COLS = 16
CONV_WIDTH = 31
N_EXPERTS = 32
TOP_K = 4
SWIGLU_ALPHA = 1.702
SWIGLU_LIMIT = 7.0
RMS_EPS = 1e-6
LN_EPS = 1e-5
GN_EPS = 64e-5
NEG_BIG = -1e30

VMEM_LIMIT_BYTES = 52 * 1024 * 1024
TOKEN_TILE = 256
SCAN_BLOCK = 16
EXPERT_ROWS = 256
HALO = 16
ISSUE_GROUP = 4
NA_GROUP = 4


def _params(*sem):
    return pltpu.CompilerParams(dimension_semantics=sem, vmem_limit_bytes=VMEM_LIMIT_BYTES)


def _adaln(x, g, sh, sc):
    y = x * lax.rsqrt(jnp.mean(x * x, axis=-1, keepdims=True) + RMS_EPS)
    return (y * g) * (1.0 + sc) + sh


def _sigmoid(x):
    return 1.0 / (1.0 + jnp.exp(-x))


def _split_dot(x, m):
    hi = x.astype(BF16)
    r1 = x - hi.astype(F32)
    mid = r1.astype(BF16)
    lo = (r1 - mid.astype(F32)).astype(BF16)
    dot = functools.partial(jnp.dot, preferred_element_type=F32)
    return dot(hi, m) + dot(mid, m) + dot(lo, m)


def _mod_kernel(c_ref, w_ref, b_ref, o_ref):
    c = c_ref[...]
    s = c * _sigmoid(c)
    o_ref[0] = jnp.dot(s, w_ref[0], preferred_element_type=F32, precision=HIGHEST) + b_ref[0]


def _modulation(c_all, ada_w, ada_b):
    depth, d, n = ada_w.shape
    rows = c_all.shape[0]
    tn = 1536
    return pl.pallas_call(
        _mod_kernel,
        grid=(depth, n // tn),
        in_specs=[pl.BlockSpec((rows, d), lambda l, j: (0, 0)),
                  pl.BlockSpec((1, d, tn), lambda l, j: (l, 0, j)),
                  pl.BlockSpec((1, 1, tn), lambda l, j: (l, 0, j))],
        out_specs=pl.BlockSpec((1, rows, tn), lambda l, j: (l, 0, j)),
        out_shape=jax.ShapeDtypeStruct((depth, rows, n), F32),
        compiler_params=_params("arbitrary", "arbitrary"),
        name="modulation",
    )(c_all, ada_w, ada_b.reshape(depth, 1, n))


def _proj_kernel(x_ref, g_ref, sh_ref, sc_ref, w_ref, o_ref):
    h = _adaln(x_ref[0], g_ref[...], sh_ref[0], sc_ref[0]).astype(BF16)
    o_ref[0] = jnp.dot(h, w_ref[...], preferred_element_type=F32)


def _in_proj(x, g, sh, sc, w):
    b, t, d = x.shape
    n = w.shape[1]
    tt = min(TOKEN_TILE, t)
    return pl.pallas_call(
        _proj_kernel,
        grid=(b, t // tt),
        in_specs=[pl.BlockSpec((1, tt, d), lambda i, j: (i, j, 0)),
                  pl.BlockSpec((1, d), lambda i, j: (0, 0)),
                  pl.BlockSpec((1, 1, d), lambda i, j: (i, 0, 0)),
                  pl.BlockSpec((1, 1, d), lambda i, j: (i, 0, 0)),
                  pl.BlockSpec((d, n), lambda i, j: (0, 0))],
        out_specs=pl.BlockSpec((1, tt, n), lambda i, j: (i, j, 0)),
        out_shape=jax.ShapeDtypeStruct((b, t, n), F32),
        compiler_params=_params("arbitrary", "arbitrary"),
        name="in_proj",
    )(x, g, sh, sc, w)


def _terms_kernel(p_ref, pp_ref, pn_ref, mup_ref, mun_ref, w0_ref, w2_ref, a0_ref, a2_ref, g2_ref,
                  kk_ref, ka_ref, rk_ref, seg_ref, z_ref, bonus_ref, gate_ref, *, n_tiles):
    t = pl.program_id(1)
    p = p_ref[0]
    tt, aw = p.shape[0], kk_ref.shape[1]
    prev_row = jnp.where(t > 0, pp_ref[0, 7:8, :], 0.0)
    next_row = jnp.where(t < n_tiles - 1, pn_ref[0, 0:1, :], 0.0)
    rows = lax.broadcasted_iota(jnp.int32, p.shape, 0)
    prev = jnp.where(rows == 0, prev_row, pltpu.roll(p, 1, axis=0))
    nxt = jnp.where(rows == tt - 1, next_row, pltpu.roll(p, tt - 1, axis=0))
    s = p + mup_ref[...] * (prev - p) + mun_ref[...] * (nxt - p)
    r, k, v = s[:, :aw], s[:, aw:2 * aw], s[:, 2 * aw:3 * aw]
    lora = s[:, 3 * aw:]
    g_in, wd, ad = lora[:, 0:128], lora[:, 128:256], lora[:, 256:384]
    dotf = functools.partial(jnp.dot, preferred_element_type=F32, precision=HIGHEST)
    zw = -(w0_ref[...] + dotf(jnp.tanh(wd), w2_ref[...]))
    softplus = jnp.maximum(zw, 0.0) + jnp.log(1.0 + jnp.exp(-jnp.abs(zw)))
    decay = jnp.exp(-jnp.exp(-softplus - 0.5))
    a = _sigmoid(a0_ref[...] + dotf(ad, a2_ref[...]))
    seg = seg_ref[...]
    kk = k * kk_ref[...]
    kk = kk / jnp.maximum(jnp.sqrt(_split_dot(kk * kk, seg)), 1e-12)
    z_ref[0, 0] = kk
    z_ref[1, 0] = v
    z_ref[2, 0] = r
    kd_sum = jnp.zeros_like(k)
    for d in range(2):
        a_d = a[:, d * aw:(d + 1) * aw]
        k_dir = k * (1.0 + (a_d - 1.0) * ka_ref[...])
        z_ref[3 + 3 * d, 0] = decay[:, d * aw:(d + 1) * aw]
        z_ref[4 + 3 * d, 0] = k_dir
        z_ref[5 + 3 * d, 0] = kk * a_d
        kd_sum = kd_sum + k_dir
    bonus_ref[0] = _split_dot(r * kd_sum * rk_ref[...], seg) * v
    gate_ref[0] = jnp.dot(_sigmoid(g_in).astype(BF16), g2_ref[...], preferred_element_type=F32)


def _rwkv_terms(p, consts):
    b, t, _ = p.shape
    aw = consts["key_k"].shape[1]
    sw = 4 * aw
    tt = min(TOKEN_TILE, t)
    n_tiles = t // tt
    hb = tt // 8
    full = lambda a: pl.BlockSpec(a.shape, lambda i, j: (0,) * a.ndim)
    names = ("mu_prev", "mu_next", "w0", "w2", "a0", "a2", "g2", "key_k", "key_a", "r_k", "seg")
    cs = [consts[n] for n in names]
    out3 = jax.ShapeDtypeStruct((b, t, aw), F32)
    return pl.pallas_call(
        functools.partial(_terms_kernel, n_tiles=n_tiles),
        grid=(b, n_tiles),
        in_specs=[pl.BlockSpec((1, tt, sw), lambda i, j: (i, j, 0)),
                  pl.BlockSpec((1, 8, sw), lambda i, j: (i, jnp.maximum(j * hb - 1, 0), 0)),
                  pl.BlockSpec((1, 8, sw), lambda i, j: (i, jnp.minimum((j + 1) * hb, t // 8 - 1), 0))]
                 + [full(a) for a in cs],
        out_specs=[pl.BlockSpec((9, 1, tt, aw), lambda i, j: (0, i, j, 0)),
                   pl.BlockSpec((1, tt, aw), lambda i, j: (i, j, 0)),
                   pl.BlockSpec((1, tt, aw), lambda i, j: (i, j, 0))],
        out_shape=[jax.ShapeDtypeStruct((9, b, t, aw), F32), out3, out3],
        compiler_params=_params("arbitrary", "arbitrary"),
        name="rwkv_terms",
    )(p, p, p, *cs)


def _scan_kernel(csf_ref, cdf_ref, csb_ref, cdb_ref, zsf_ref, zdf_ref, zsb_ref, zdb_ref, yf_ref, yb_ref, s_ref,
                 *, tb, nc):
    g = pl.program_id(0)

    @pl.when(g == 0)
    def _():
        s_ref[...] = jnp.zeros_like(s_ref)

    n = s_ref.shape[1]

    def run(dirs):
        def step(tf, carry):
            tidx = (tf, tb - 1 - tf)
            vecs = []
            for d, (zs, zd, _) in enumerate(dirs):
                ti = tidx[d]
                kk, r = zs[ti, 0], zs[ti, 2]
                w, kd, bb = zd[ti, 0], zd[ti, 1], zd[ti, 2]
                bbr = jnp.sum(bb * r, axis=0, keepdims=True)
                kr = jnp.sum(kd * r, axis=0, keepdims=True)
                vecs.append((kk, w * r, w, bb, kd, bbr, kr))

            def row(i, c):
                for d, (zs, _, y_ref) in enumerate(dirs):
                    kk, wr, w, bb, kd, bbr, kr = vecs[d]
                    ti = tidx[d]
                    si = s_ref[d, i]
                    sa = -jnp.sum(si * kk, axis=0, keepdims=True)
                    vi = zs[ti, 1, pl.ds(i, 1), :]
                    s_ref[d, i] = si * w + sa * bb + vi * kd
                    if y_ref is not None:
                        y0 = jnp.sum(si * wr, axis=0, keepdims=True)
                        y_ref[ti, pl.ds(i, 1), :] = y0 + sa * bbr + vi * kr
                return c

            lax.fori_loop(0, n, row, 0, unroll=8)
            return carry

        lax.fori_loop(0, tb, step, 0)

    @pl.when(g < nc)
    def _():
        run(((csf_ref, cdf_ref, None), (csb_ref, cdb_ref, None)))

    @pl.when(g >= nc)
    def _():
        run(((zsf_ref, zdf_ref, yf_ref), (zsb_ref, zdb_ref, yb_ref)))


def _wkv_scan(zc, zm):
    n, lanes = zm.shape[2:]
    tb = SCAN_BLOCK
    nc, nm = zc.shape[0] // tb, zm.shape[0] // tb
    cf = lambda g: jnp.minimum(g, nc - 1)
    cb = lambda g: jnp.maximum(nc - 1 - g, 0)
    mf = lambda g: jnp.maximum(g - nc, 0)
    mb = lambda g: jnp.minimum(nm - 1, nm - 1 + nc - g)
    blk = (tb, 3, n, lanes)
    spec = lambda t_of, part: pl.BlockSpec(blk, lambda g: (t_of(g), part, 0, 0))
    y_shape = jax.ShapeDtypeStruct((nm * tb, n, lanes), F32)
    return pl.pallas_call(
        functools.partial(_scan_kernel, tb=tb, nc=nc),
        grid=(nc + nm,),
        in_specs=[spec(cf, 0), spec(cf, 1), spec(cb, 0), spec(cb, 2),
                  spec(mf, 0), spec(mf, 1), spec(mb, 0), spec(mb, 2)],
        out_specs=[pl.BlockSpec((tb, n, lanes), lambda g: (mf(g), 0, 0)),
                   pl.BlockSpec((tb, n, lanes), lambda g: (mb(g), 0, 0))],
        out_shape=[y_shape, y_shape],
        scratch_shapes=[pltpu.VMEM((2, n, n, lanes), F32)],
        compiler_params=_params("arbitrary"),
        name="wkv_scan",
    )(zc, zc, zc, zc, zm, zm, zm, zm)


def _to_scan_kernel(x_ref, o_ref):
    n_comp, nb, tt, aw = x_ref.shape
    n_head = aw // HEAD_DIM
    low = lax.broadcasted_iota(jnp.int32, (nb, 128), 1) < HEAD_DIM

    def comp(c, carry):
        for tp in range(tt // 2):
            a = x_ref[c, :, 2 * tp, :]
            b = x_ref[c, :, 2 * tp + 1, :]
            pieces = []
            for h in range(n_head):
                ls = slice((h // 2) * 128, (h // 2 + 1) * 128)
                am, bm = a[:, ls], b[:, ls]
                if h % 2 == 0:
                    pieces.append(jnp.where(low, am, pltpu.roll(bm, HEAD_DIM, axis=1)))
                else:
                    pieces.append(jnp.where(low, pltpu.roll(am, HEAD_DIM, axis=1), bm))
            r2 = jnp.concatenate(pieces, axis=0).T
            o_ref[2 * tp, c] = r2[:HEAD_DIM]
            o_ref[2 * tp + 1, c] = r2[HEAD_DIM:]
        return carry

    lax.fori_loop(0, n_comp, comp, 0)


def _to_scan_layout(z):
    n_comp, b, t, aw = z.shape
    tt = SCAN_BLOCK
    lanes = (aw // HEAD_DIM) * b
    return pl.pallas_call(
        _to_scan_kernel,
        grid=(t // tt,),
        in_specs=[pl.BlockSpec((n_comp, b, tt, aw), lambda i: (0, 0, i, 0))],
        out_specs=pl.BlockSpec((tt, n_comp, HEAD_DIM, lanes), lambda i: (i, 0, 0, 0)),
        out_shape=jax.ShapeDtypeStruct((t, n_comp, HEAD_DIM, lanes), F32),
        compiler_params=_params("arbitrary"),
        name="to_scan_layout",
    )(z)


def _from_scan_kernel(yf_ref, yb_ref, o_ref):
    tt = yf_ref.shape[0]
    nb, _, aw = o_ref.shape
    n_head = aw // HEAD_DIM
    low = lax.broadcasted_iota(jnp.int32, (nb, 128), 1) < HEAD_DIM
    for tp in range(tt // 2):
        s = jnp.concatenate([yf_ref[2 * tp] + yb_ref[2 * tp], yf_ref[2 * tp + 1] + yb_ref[2 * tp + 1]], axis=0)
        r2 = s.T
        for m in range(n_head // 2):
            pe = r2[(2 * m) * nb:(2 * m + 1) * nb]
            po = r2[(2 * m + 1) * nb:(2 * m + 2) * nb]
            ls = slice(m * 128, (m + 1) * 128)
            o_ref[:, 2 * tp, ls] = jnp.where(low, pe, pltpu.roll(po, HEAD_DIM, axis=1))
            o_ref[:, 2 * tp + 1, ls] = jnp.where(low, pltpu.roll(pe, HEAD_DIM, axis=1), po)


def _from_scan_layout(yf, yb, b):
    t, n, lanes = yf.shape
    aw = (lanes // b) * n
    tt = SCAN_BLOCK
    return pl.pallas_call(
        _from_scan_kernel,
        grid=(t // tt,),
        in_specs=[pl.BlockSpec((tt, n, lanes), lambda i: (i, 0, 0))] * 2,
        out_specs=pl.BlockSpec((b, tt, aw), lambda i: (0, i, 0)),
        out_shape=jax.ShapeDtypeStruct((b, t, aw), F32),
        compiler_params=_params("arbitrary"),
        name="from_scan_layout",
    )(yf, yb)


def _na_kernel(q_ref, k_ref, v_ref, kc_ref, vc_ref, bias_ref, o_ref, kb_ref, vb_ref, kcb_ref, vcb_ref, *, rows):
    kh = NA_ROWS
    dn = (((1,), (1,)), ((), ()))
    kb_ref[...] = k_ref[0].astype(BF16)
    vb_ref[...] = v_ref[0].astype(BF16)
    kcb_ref[...] = kc_ref[0].astype(BF16)
    vcb_ref[...] = vc_ref[0].astype(BF16)
    nq = NA_GROUP * GRID_W
    nk = (kh + NA_GROUP - 1) * GRID_W
    n_groups = rows // NA_GROUP
    head_of_lane = lax.broadcasted_iota(jnp.int32, (nq, 2 * HEAD_DIM), 1) // HEAD_DIM

    def group(g, c):
        u = _na_union_start(g, rows)
        pat = jnp.where(g > 0, 1, 0) + jnp.where(g == n_groups - 1, 1, 0)
        q0 = pl.multiple_of(g * nq, nq)
        k0 = pl.multiple_of(u * GRID_W, GRID_W)
        q2 = q_ref[0, pl.ds(q0, nq), :] * (HEAD_DIM ** -0.5)
        kl = kb_ref[pl.ds(k0, nk), :]
        vl = vb_ref[pl.ds(k0, nk), :]
        out = jnp.zeros((nq, 2 * HEAD_DIM), F32)
        for hh in range(2):
            q = jnp.where(head_of_lane == hh, q2, 0.0).astype(BF16)
            s_loc = lax.dot_general(q, kl, dn, preferred_element_type=F32) + bias_ref[pat, hh]
            s_ctx = lax.dot_general(q, kcb_ref[...], dn, preferred_element_type=F32)
            m = jnp.maximum(jnp.max(s_loc, axis=-1, keepdims=True), jnp.max(s_ctx, axis=-1, keepdims=True))
            e_loc = jnp.exp(s_loc - m)
            e_ctx = jnp.exp(s_ctx - m)
            den = jnp.sum(e_loc, axis=-1, keepdims=True) + jnp.sum(e_ctx, axis=-1, keepdims=True)
            o = (jnp.dot(e_loc.astype(BF16), vl, preferred_element_type=F32)
                 + jnp.dot(e_ctx.astype(BF16), vcb_ref[...], preferred_element_type=F32))
            out = jnp.where(head_of_lane == hh, o / den, out)
        o_ref[0, pl.ds(q0, nq), :] = out.astype(o_ref.dtype)
        return c

    lax.fori_loop(0, n_groups, group, 0)


def _na_union_start(g, rows):
    lo = NA_GROUP * g - NA_ROWS // 2
    hi = rows - (NA_ROWS + NA_GROUP - 1)
    if isinstance(g, int):
        return min(max(lo, 0), hi)
    return jnp.clip(lo, 0, hi)


def _neighbourhood_attention(p, pc, bias, col_q, col_k, col_v):
    b, t, _ = p.shape
    l = pc.shape[1]
    rows = t // GRID_W
    n_pairs = bias.shape[1] // 2
    return pl.pallas_call(
        functools.partial(_na_kernel, rows=rows),
        grid=(n_pairs, b),
        in_specs=[pl.BlockSpec((1, t, 128), lambda h, i: (i, 0, col_q + h)),
                  pl.BlockSpec((1, t, 128), lambda h, i: (i, 0, col_k + h)),
                  pl.BlockSpec((1, t, 128), lambda h, i: (i, 0, col_v + h)),
                  pl.BlockSpec((1, l, 128), lambda h, i: (i, 0, col_k + h)),
                  pl.BlockSpec((1, l, 128), lambda h, i: (i, 0, col_v + h)),
                  pl.BlockSpec((bias.shape[0], 2) + bias.shape[2:], lambda h, i: (0, h, 0, 0))],
        out_specs=pl.BlockSpec((1, t, 128), lambda h, i: (i, 0, h)),
        out_shape=jax.ShapeDtypeStruct((b, t, n_pairs * 128), BF16),
        scratch_shapes=[pltpu.VMEM((t, 128), BF16), pltpu.VMEM((t, 128), BF16),
                        pltpu.VMEM((l, 128), BF16), pltpu.VMEM((l, 128), BF16)],
        compiler_params=_params("arbitrary", "arbitrary"),
        name="na_attention",
    )(p, p, p, pc, pc, bias)


def _na_bias_table(rpb, rows):
    h = rpb.shape[0]
    n_groups = rows // NA_GROUP
    assert rows % NA_GROUP == 0 and rows >= NA_ROWS + 2 * NA_GROUP - 1
    col = np.arange(GRID_W)
    c_start = np.clip(col - NA_COLS // 2, 0, GRID_W - NA_COLS)
    col_ok = (col[None, :] >= c_start[:, None]) & (col[None, :] < c_start[:, None] + NA_COLS)
    dc = np.clip(col[None, :] - col[:, None], 1 - NA_COLS, NA_COLS - 1) + NA_COLS - 1
    pick = (dc.reshape(1, -1) == np.arange(2 * NA_COLS - 1)[:, None]).astype(np.float32)
    t = jnp.einsum("hrc,cx->hrx", rpb, pick, precision=HIGHEST)
    t = jnp.where(col_ok.reshape(-1), t, NEG_BIG).reshape(h, 2 * NA_ROWS - 1, GRID_W, GRID_W)
    masked = jnp.full((h, GRID_W, GRID_W), NEG_BIG, F32)
    n_union = NA_ROWS + NA_GROUP - 1
    pats = []
    for g in (0, 1, n_groups - 1):
        u = _na_union_start(g, rows)
        blocks = []
        for ri in range(NA_GROUP):
            r = g * NA_GROUP + ri
            r_start = min(max(r - NA_ROWS // 2, 0), rows - NA_ROWS)
            row_blocks = []
            for kr in range(n_union):
                key_row = u + kr
                inside = r_start <= key_row < r_start + NA_ROWS
                row_blocks.append(t[:, key_row - r + NA_ROWS - 1] if inside else masked)
            blocks.append(jnp.stack(row_blocks, axis=2))
        pats.append(jnp.stack(blocks, axis=1))
    tab = jnp.stack(pats, axis=0)
    return tab.reshape(3, h, NA_GROUP * GRID_W, n_union * GRID_W)


def _mix_out_kernel(y_ref, bonus_ref, gate_ref, ob_ref, x_ref, g1_ref, lg_ref, lb_ref, seg_ref, wa_ref, wb_ref, o_ref):
    y = y_ref[0]
    seg = seg_ref[...]
    inv = 1.0 / HEAD_DIM
    mu = _split_dot(y, seg) * inv
    yc = y - mu
    var = _split_dot(yc * yc, seg) * inv
    yn = (yc * lax.rsqrt(var + GN_EPS)) * lg_ref[...] + lb_ref[...]
    o_a = ((yn + bonus_ref[0]) * gate_ref[0]).astype(BF16)
    out = (jnp.dot(o_a, wa_ref[...], preferred_element_type=F32)
           + jnp.dot(ob_ref[0], wb_ref[...], preferred_element_type=F32))
    o_ref[0] = x_ref[0] + g1_ref[0] * out


def _mix_out(y, bonus, gate, o_b, x, g1, lnx_g, lnx_b, seg, w_a, w_b):
    b, t, d = x.shape
    aw = y.shape[2]
    tt = min(TOKEN_TILE, t)
    tok = lambda w: pl.BlockSpec((1, tt, w), lambda i, j: (i, j, 0))
    full = lambda a: pl.BlockSpec(a.shape, lambda i, j: (0,) * a.ndim)
    return pl.pallas_call(
        _mix_out_kernel,
        grid=(b, t // tt),
        in_specs=[tok(aw), tok(aw), tok(aw), tok(o_b.shape[2]), tok(d),
                  pl.BlockSpec((1, 1, d), lambda i, j: (i, 0, 0)),
                  full(lnx_g), full(lnx_b), full(seg), full(w_a), full(w_b)],
        out_specs=tok(d),
        out_shape=jax.ShapeDtypeStruct((b, t, d), F32),
        compiler_params=_params("arbitrary", "arbitrary"),
        name="mix_out",
    )(y, bonus, gate, o_b, x, g1, lnx_g, lnx_b, seg, w_a, w_b)


def _glu_kernel(x_ref, g_ref, sh_ref, sc_ref, w_ref, b_ref, o_ref):
    h = _adaln(x_ref[0], g_ref[...], sh_ref[0], sc_ref[0]).astype(BF16)
    u = jnp.dot(h, w_ref[...], preferred_element_type=F32) + b_ref[...]
    d = u.shape[1] // 2
    o_ref[0] = u[:, :d] * _sigmoid(u[:, d:])


def _glu_proj(x, g, sh, sc, w, bias):
    b, t, d = x.shape
    n = w.shape[1]
    tt = min(TOKEN_TILE, t)
    return pl.pallas_call(
        _glu_kernel,
        grid=(b, t // tt),
        in_specs=[pl.BlockSpec((1, tt, d), lambda i, j: (i, j, 0)),
                  pl.BlockSpec((1, d), lambda i, j: (0, 0)),
                  pl.BlockSpec((1, 1, d), lambda i, j: (i, 0, 0)),
                  pl.BlockSpec((1, 1, d), lambda i, j: (i, 0, 0)),
                  pl.BlockSpec((d, n), lambda i, j: (0, 0)),
                  pl.BlockSpec((1, n), lambda i, j: (0, 0))],
        out_specs=pl.BlockSpec((1, tt, n // 2), lambda i, j: (i, j, 0)),
        out_shape=jax.ShapeDtypeStruct((b, t, n // 2), F32),
        compiler_params=_params("arbitrary", "arbitrary"),
        name="glu_proj",
    )(x, g, sh, sc, w, bias)


def _conv_kernel(u_ref, up_ref, un_ref, dw_ref, dwb_ref, lg_ref, lb_ref, w2_ref, b2_ref, x_ref, g1_ref, o_ref,
                 win_ref, acc_ref, *, n_tiles):
    t = pl.program_id(1)
    tt, d = u_ref.shape[1], u_ref.shape[2]
    half = CONV_WIDTH // 2
    win_ref[0:HALO, :] = jnp.where(t > 0, up_ref[0], 0.0)
    win_ref[HALO:HALO + tt, :] = u_ref[0]
    win_ref[HALO + tt:2 * HALO + tt, :] = jnp.where(t < n_tiles - 1, un_ref[0], 0.0)
    rc = 64
    first = HALO - half

    for base in range(0, tt, rc):
        for lc in range(d // 128):
            ls = slice(lc * 128, (lc + 1) * 128)
            out = None
            for s in range(8):
                acc = None
                for a in range((first + CONV_WIDTH - 1 - s) // 8 + 1):
                    k = 8 * a + s - first
                    if k < 0:
                        continue
                    term = dw_ref[k:k + 1, ls] * win_ref[base + 8 * a:base + 8 * a + rc + 8, ls]
                    acc = term if acc is None else acc + term
                part = acc[s:s + rc]
                out = part if out is None else out + part
            acc_ref[base:base + rc, ls] = out
    u = acc_ref[...] + dwb_ref[...]
    mu = jnp.mean(u, axis=-1, keepdims=True)
    uc = u - mu
    var = jnp.mean(uc * uc, axis=-1, keepdims=True)
    un = (uc * lax.rsqrt(var + LN_EPS)) * lg_ref[...] + lb_ref[...]
    act = (un * _sigmoid(un)).astype(BF16)
    out = jnp.dot(act, w2_ref[...], preferred_element_type=F32) + b2_ref[...]
    o_ref[0] = x_ref[0] + g1_ref[0] * out


def _conv_module(u, dw, dwb, ln_g, ln_b, w2, b2, x, g1):
    b, t, d = x.shape
    tt = min(TOKEN_TILE, t)
    n_tiles = t // tt
    hb = tt // HALO
    tok = pl.BlockSpec((1, tt, d), lambda i, j: (i, j, 0))
    full = lambda a: pl.BlockSpec(a.shape, lambda i, j: (0,) * a.ndim)
    return pl.pallas_call(
        functools.partial(_conv_kernel, n_tiles=n_tiles),
        grid=(b, n_tiles),
        in_specs=[tok,
                  pl.BlockSpec((1, HALO, d), lambda i, j: (i, jnp.maximum(j * hb - 1, 0), 0)),
                  pl.BlockSpec((1, HALO, d), lambda i, j: (i, jnp.minimum((j + 1) * hb, t // HALO - 1), 0)),
                  full(dw), full(dwb), full(ln_g), full(ln_b), full(w2), full(b2), tok,
                  pl.BlockSpec((1, 1, d), lambda i, j: (i, 0, 0))],
        out_specs=tok,
        out_shape=jax.ShapeDtypeStruct((b, t, d), F32),
        scratch_shapes=[pltpu.VMEM((tt + 2 * HALO, d), F32), pltpu.VMEM((tt, d), F32)],
        compiler_params=_params("arbitrary", "arbitrary"),
        name="conv_module",
    )(u, u, u, dw, dwb, ln_g, ln_b, w2, b2, x, g1)


def _route_kernel(x_ref, g_ref, sh_ref, sc_ref, wr_ref, br_ref, hp_ref, route_ref, cnt_ref, carry_ref):
    @pl.when((pl.program_id(0) == 0) & (pl.program_id(1) == 0))
    def _():
        carry_ref[...] = jnp.zeros_like(carry_ref)

    h = _adaln(x_ref[0], g_ref[...], sh_ref[0], sc_ref[0])
    tt, d = h.shape
    hi = lax.bitcast_convert_type(h[:, :d // 2].astype(BF16).astype(F32), jnp.uint32)
    lo = lax.bitcast_convert_type(h[:, d // 2:].astype(BF16).astype(F32), jnp.uint32)
    packed = (hi & jnp.uint32(0xFFFF0000)) | (lo >> 16)
    n_ch = d // 2 // 128
    for c in range(n_ch):
        hp_ref[0, pl.ds(c, tt, stride=n_ch), :] = packed[:, c * 128:(c + 1) * 128]

    logits = jnp.dot(h, wr_ref[...], preferred_element_type=F32, precision=HIGHEST) + br_ref[...]
    ne = logits.shape[1]
    lane = lax.broadcasted_iota(jnp.int32, (tt, ne), 1).astype(F32)
    work = logits
    mask = jnp.zeros((tt, ne), F32)
    picks, es = [], []
    den = jnp.zeros((tt, 1), F32)
    for k in range(TOP_K):
        m = jnp.max(work, axis=-1, keepdims=True)
        idx = jnp.min(jnp.where(work == m, lane, float(ne)), axis=-1, keepdims=True)
        pick = lane == idx
        if k == 0:
            top = m
        e = jnp.exp(m - top)
        den = den + e
        picks.append((pick, idx))
        es.append(e)
        mask = jnp.where(pick, 1.0, mask)
        work = jnp.where(pick, -jnp.inf, work)

    ri = lax.broadcasted_iota(jnp.int32, (tt, tt), 0)
    ci = lax.broadcasted_iota(jnp.int32, (tt, tt), 1)
    lower = jnp.where(ci < ri, 1.0, 0.0).astype(BF16)
    rank = jnp.dot(lower, mask.astype(BF16), preferred_element_type=F32) + carry_ref[...]
    carry_ref[...] = carry_ref[...] + jnp.sum(mask, axis=0, keepdims=True)
    cnt_ref[...] = carry_ref[...]

    out_lane = lax.broadcasted_iota(jnp.int32, (tt, 128), 1)
    route = jnp.zeros((tt, 128), F32)
    for k in range(TOP_K):
        pick, idx = picks[k]
        rk = jnp.sum(jnp.where(pick, rank, 0.0), axis=-1, keepdims=True)
        route = jnp.where(out_lane == k, idx, route)
        route = jnp.where(out_lane == TOP_K + k, rk, route)
        route = jnp.where(out_lane == 2 * TOP_K + k, es[k] / den, route)
    route_ref[0] = route


def _route(x, g, sh, sc, w_r, b_r):
    b, t, d = x.shape
    ne = w_r.shape[1]
    tt = min(TOKEN_TILE, t)
    return pl.pallas_call(
        _route_kernel,
        grid=(b, t // tt),
        in_specs=[pl.BlockSpec((1, tt, d), lambda i, j: (i, j, 0)),
                  pl.BlockSpec((1, d), lambda i, j: (0, 0)),
                  pl.BlockSpec((1, 1, d), lambda i, j: (i, 0, 0)),
                  pl.BlockSpec((1, 1, d), lambda i, j: (i, 0, 0)),
                  pl.BlockSpec((d, ne), lambda i, j: (0, 0)),
                  pl.BlockSpec((1, ne), lambda i, j: (0, 0))],
        out_specs=[pl.BlockSpec((1, tt * (d // 256), 128), lambda i, j: (i, j, 0)),
                   pl.BlockSpec((1, tt, 128), lambda i, j: (i, j, 0)),
                   pl.BlockSpec((1, ne), lambda i, j: (0, 0))],
        out_shape=[jax.ShapeDtypeStruct((b, t * (d // 256), 128), jnp.uint32),
                   jax.ShapeDtypeStruct((b, t, 128), F32),
                   jax.ShapeDtypeStruct((1, ne), F32)],
        scratch_shapes=[pltpu.VMEM((1, ne), F32)],
        compiler_params=_params("arbitrary", "arbitrary"),
        name="moe_route",
    )(x, g, sh, sc, w_r, b_r)


def _dispatch_kernel(pos_hbm, hp_ref, xs_in, xs_out, idx_ref, sem_idx, sem_rows, *, tile, n_tiles, rc):
    del xs_in
    n_idx = tile * TOP_K

    def idx_copy(i, slot):
        return pltpu.make_async_copy(pos_hbm.at[i], idx_ref.at[pl.ds(slot * n_idx, n_idx)], sem_idx.at[slot])

    i = pl.program_id(0)
    slot = i % 2

    @pl.when(i == 0)
    def _():
        idx_copy(0, 0).start()

    idx_copy(i, slot).wait()

    @pl.when(i + 1 < n_tiles)
    def _():
        idx_copy(i + 1, 1 - slot).start()

    ibase = slot * n_idx

    def issue(jg, c2):
        j0 = jg * ISSUE_GROUP
        rows = [idx_ref[ibase + j0 * TOP_K + q] for q in range(ISSUE_GROUP * TOP_K)]
        for q, row in enumerate(rows):
            src = hp_ref.at[pl.ds(pl.multiple_of((j0 + q // TOP_K) * rc, rc), rc)]
            copy = pltpu.make_async_copy(src, xs_out.at[pl.ds(pl.multiple_of(row * rc, rc), rc)], sem_rows)
            copy.start(priority=q % 2)
        return c2

    lax.fori_loop(0, tile // ISSUE_GROUP, issue, 0)
    for _ in range(TOP_K):
        pltpu.make_async_copy(hp_ref, xs_out.at[pl.ds(0, tile * rc)], sem_rows).wait()


def _dispatch(pos_flat, hp, n_rows, rc):
    n, w = hp.shape[0] // rc, hp.shape[1]
    tile = TOKEN_TILE
    xs0 = jnp.zeros((n_rows * rc, w), jnp.uint32)
    any_spec = pl.BlockSpec(memory_space=pl.ANY)
    return pl.pallas_call(
        functools.partial(_dispatch_kernel, tile=tile, n_tiles=n // tile, rc=rc),
        grid=(n // tile,),
        in_specs=[any_spec, pl.BlockSpec((tile * rc, w), lambda i: (i, 0)), any_spec],
        out_specs=any_spec,
        out_shape=jax.ShapeDtypeStruct((n_rows * rc, w), jnp.uint32),
        scratch_shapes=[pltpu.SMEM((2 * tile * TOP_K,), jnp.int32),
                        pltpu.SemaphoreType.DMA((2,)), pltpu.SemaphoreType.DMA],
        input_output_aliases={2: 0},
        compiler_params=_params("arbitrary"),
        name="moe_dispatch",
    )(pos_flat.reshape(n // tile, tile * TOP_K), hp, xs0)


def _split_w1_kernel(w_ref, p_ref, g_ref, l_ref):
    w = w_ref[0, 0].astype(BF16)
    n = w.shape[1]
    for c in range(n // 256):
        res = jnp.dot(w[:, c * 256:(c + 1) * 256], p_ref[...], preferred_element_type=F32)
        g_ref[0, :, c * 128:(c + 1) * 128] = res[:, :128].astype(BF16)
        l_ref[0, :, c * 128:(c + 1) * 128] = res[:, 128:].astype(BF16)


def _split_w1(w1, layer):
    _, ne, d, f2 = w1.shape
    rows = 512
    r = np.arange(256)[:, None]
    c = np.arange(256)[None, :]
    sel = jnp.asarray(np.where(c < 128, r == 2 * c, r == 2 * (c - 128) + 1), BF16)
    out = jax.ShapeDtypeStruct((ne, d, f2 // 2), BF16)
    return pl.pallas_call(
        _split_w1_kernel,
        grid=(ne, d // rows),
        in_specs=[pl.BlockSpec((1, 1, rows, f2), lambda e, i: (layer, e, i, 0)),
                  pl.BlockSpec((256, 256), lambda e, i: (0, 0))],
        out_specs=[pl.BlockSpec((1, rows, f2 // 2), lambda e, i: (e, i, 0))] * 2,
        out_shape=[out, out],
        compiler_params=_params("arbitrary", "arbitrary"),
        name="split_w1",
    )(w1, sel)


def _expert_kernel(be_ref, nb_ref, xs_ref, w1g_ref, w1l_ref, b1g_ref, b1l_ref, w2_ref, b2_ref, ys_ref, *, bm):
    del be_ref
    rc = xs_ref.shape[0] // bm
    oc = ys_ref.shape[0] // bm

    @pl.when(pl.program_id(0) < nb_ref[0])
    def _():
        u = jnp.concatenate([xs_ref[pl.ds(c, bm, stride=rc), :] for c in range(rc)], axis=1)
        half = u.shape[1]
        xa = lax.bitcast_convert_type(u & jnp.uint32(0xFFFF0000), F32).astype(BF16)
        xb = lax.bitcast_convert_type(u << 16, F32).astype(BF16)
        dot = functools.partial(jnp.dot, preferred_element_type=F32)
        ug = dot(xa, w1g_ref[0, :half, :]) + dot(xb, w1g_ref[0, half:, :]) + b1g_ref[0]
        ul = dot(xa, w1l_ref[0, :half, :]) + dot(xb, w1l_ref[0, half:, :]) + b1l_ref[0]
        glu = jnp.minimum(ug, SWIGLU_LIMIT)
        lin = jnp.clip(ul, -SWIGLU_LIMIT, SWIGLU_LIMIT)
        act = (glu * _sigmoid(SWIGLU_ALPHA * glu)) * (lin + 1.0)
        y = dot(act.astype(BF16), w2_ref[0]) + b2_ref[0]
        for c in range(oc):
            ys_ref[pl.ds(c, bm, stride=oc), :] = y[:, c * 128:(c + 1) * 128]

    @pl.when(pl.program_id(0) >= nb_ref[0])
    def _():
        ys_ref[...] = jnp.zeros_like(ys_ref)


def _expert_ffn(block_e, n_used, xs, w1g, w1l, b1g, b1l, w2, b2, rc):
    n_rows = xs.shape[0] // rc
    ne, d, f = w1g.shape
    oc = d // 128
    bm = EXPERT_ROWS
    n_blocks = n_rows // bm
    wspec = lambda s: pl.BlockSpec((1,) + s, lambda i, be, nb: (be[i], 0, 0))
    return pl.pallas_call(
        functools.partial(_expert_kernel, bm=bm),
        grid_spec=pltpu.PrefetchScalarGridSpec(
            num_scalar_prefetch=2,
            grid=(n_blocks,),
            in_specs=[pl.BlockSpec((bm * rc, 128), lambda i, be, nb: (i, 0)),
                      wspec((d, f)), wspec((d, f)), wspec((1, f)), wspec((1, f)), wspec((f, d)), wspec((1, d))],
            out_specs=pl.BlockSpec((bm * oc, 128), lambda i, be, nb: (i, 0)),
        ),
        out_shape=jax.ShapeDtypeStruct((n_rows * oc, 128), F32),
        compiler_params=_params("arbitrary"),
        name="moe_experts",
    )(block_e, n_used, xs, w1g, w1l, b1g, b1l, w2, b2)


def _combine_kernel(pos_hbm, ys_hbm, x_ref, route_ref, g2_ref, fg_ref, o_ref, buf_ref, idx_ref, sem_idx, sem_rows,
                    *, tile, n_tiles, final):
    i = pl.program_id(0)
    n_idx = tile * TOP_K
    slot = i % 2
    oc = x_ref.shape[1] // 128

    def idx_copy(t, s):
        src = pos_hbm.at[pl.ds(pl.multiple_of(t * n_idx, n_idx), n_idx)]
        return pltpu.make_async_copy(src, idx_ref.at[pl.ds(s * n_idx, n_idx)], sem_idx.at[s])

    def gather(s):
        ibase = s * n_idx

        def issue(jg, c):
            j0 = jg * ISSUE_GROUP
            rows = [idx_ref[ibase + j0 * TOP_K + q] for q in range(ISSUE_GROUP * TOP_K)]
            for q, row in enumerate(rows):
                dst_row = pl.multiple_of((j0 + q // TOP_K) * oc, oc)
                copy = pltpu.make_async_copy(ys_hbm.at[pl.ds(pl.multiple_of(row * oc, oc), oc)],
                                             buf_ref.at[s, q % TOP_K, pl.ds(dst_row, oc)], sem_rows.at[s])
                copy.start(priority=q % 2)
            return c
        lax.fori_loop(0, tile // ISSUE_GROUP, issue, 0)

    @pl.when(i == 0)
    def _():
        idx_copy(0, 0).start()
        idx_copy(0, 0).wait()
        gather(0)
        if n_tiles > 1:
            idx_copy(1, 1).start()

    for k in range(TOP_K):
        pltpu.make_async_copy(ys_hbm.at[pl.ds(0, tile * oc)], buf_ref.at[slot, k], sem_rows.at[slot]).wait()

    @pl.when(i + 1 < n_tiles)
    def _():
        idx_copy(i + 1, 1 - slot).wait()
        gather(1 - slot)

    @pl.when(i + 2 < n_tiles)
    def _():
        idx_copy(i + 2, slot).start()

    route = route_ref[...]
    chunks = []
    for c in range(oc):
        acc = jnp.zeros((tile, 128), F32)
        for k in range(TOP_K):
            acc = acc + buf_ref[slot, k, pl.ds(c, tile, stride=oc), :] * route[:, 2 * TOP_K + k:2 * TOP_K + k + 1]
        chunks.append(acc)
    x = x_ref[...] + g2_ref[0] * jnp.concatenate(chunks, axis=1)
    if final:
        x = (x * lax.rsqrt(jnp.mean(x * x, axis=-1, keepdims=True) + RMS_EPS)) * fg_ref[...]
    o_ref[...] = x


def _combine(pos_flat, ys, x2, route2, g2, final_g, tiles_per_batch, final):
    n, d = x2.shape
    tile = TOKEN_TILE
    return pl.pallas_call(
        functools.partial(_combine_kernel, tile=tile, n_tiles=n // tile, final=final),
        grid=(n // tile,),
        in_specs=[pl.BlockSpec(memory_space=pl.ANY),
                  pl.BlockSpec(memory_space=pl.ANY),
                  pl.BlockSpec((tile, d), lambda i: (i, 0)),
                  pl.BlockSpec((tile, 128), lambda i: (i, 0)),
                  pl.BlockSpec((1, 1, d), lambda i: (i // tiles_per_batch, 0, 0)),
                  pl.BlockSpec((1, d), lambda i: (0, 0))],
        out_specs=pl.BlockSpec((tile, d), lambda i: (i, 0)),
        out_shape=jax.ShapeDtypeStruct((n, d), F32),
        scratch_shapes=[pltpu.VMEM((2, TOP_K, tile * (d // 128), 128), F32),
                        pltpu.SMEM((2 * tile * TOP_K,), jnp.int32),
                        pltpu.SemaphoreType.DMA((2,)), pltpu.SemaphoreType.DMA((2,))],
        compiler_params=_params("arbitrary"),
        name="moe_combine",
    )(pos_flat, ys, x2, route2, g2, final_g)


def _moe_layer(x, g, sh, sc, gate2, w_r, b_r, w1_all, layer, b1, w2, b2, final_g, final):
    b, t, d = x.shape
    n = b * t
    ne = w_r.shape[1]
    bm = EXPERT_ROWS
    hp, route, counts = _route(x, g, sh, sc, w_r, b_r.reshape(1, ne))
    route2 = route.reshape(n, 128)

    counts = counts[0].astype(jnp.int32)
    padded = (counts + bm - 1) // bm * bm
    pad_end = jnp.cumsum(padded)
    pad_start = pad_end - padded
    n_blocks = -(-(n * TOP_K + ne * (bm - 1)) // bm)
    e_idx = route2[:, :TOP_K].astype(jnp.int32)
    experts = jnp.arange(ne, dtype=jnp.int32)
    start_of = jnp.sum(jnp.where(e_idx[:, :, None] == experts, pad_start, 0), axis=-1)
    pos_flat = (start_of + route2[:, TOP_K:2 * TOP_K].astype(jnp.int32)).reshape(-1)
    block_row = jnp.arange(n_blocks, dtype=jnp.int32) * bm
    block_e = jnp.minimum(jnp.sum((pad_end[None, :] <= block_row[:, None]).astype(jnp.int32), axis=-1), ne - 1)
    n_used = (pad_end[-1:] // bm).astype(jnp.int32)

    rc = d // 256
    xs = _dispatch(pos_flat, hp.reshape(n * rc, 128), n_blocks * bm, rc)
    f = w2.shape[1]
    w1g, w1l = _split_w1(w1_all, layer)
    b1g = b1[:, 0::2].reshape(ne, 1, f)
    b1l = b1[:, 1::2].reshape(ne, 1, f)
    ys = _expert_ffn(block_e, n_used, xs, w1g, w1l, b1g, b1l, w2.astype(BF16), b2.reshape(ne, 1, d), rc)
    out = _combine(pos_flat, ys, x.reshape(n, d), route2, gate2, final_g.reshape(1, d), t // TOKEN_TILE, final)
    return out.reshape(b, t, d)


def _even_layer(x, ctx, mod, mod_c, norm_g, w_in, mu_prev, mu_next, w0, w2, a0, a2, g2, key_k, key_a, r_k,
                lnx_g, lnx_b, rpb, w_out):
    b, t, d = x.shape
    l = ctx.shape[1]
    aw = key_k.shape[0]
    dl = w2.shape[1]
    bw = (w_in.shape[1] - 3 * aw - 128 - 4 * dl) // 3
    n_heads = aw // HEAD_DIM
    sh1, sc1, g1 = (mod[:, None, i * d:(i + 1) * d] for i in range(3))
    shc = jnp.broadcast_to(mod_c[None, None, :d], (b, 1, d))
    scc = jnp.broadcast_to(mod_c[None, None, d:2 * d], (b, 1, d))

    c_ra, c_gd = bw, bw + aw
    c_ka = c_gd + 128
    c_va = c_ka + aw
    c_wd = c_va + aw
    c_ad = c_wd + 2 * dl
    c_kb = c_ad + 2 * dl
    c_vb = c_kb + bw
    cols = lambda a, lo, hi: a[..., lo:hi]
    pad = jnp.zeros((d, 128), F32)
    w_p = jnp.concatenate([cols(w_in, c_ra, c_gd), cols(w_in, c_ka, c_va), cols(w_in, c_va, c_wd),
                           cols(w_in, c_gd, c_ka), cols(w_in, c_wd, c_ad), cols(w_in, c_ad, c_kb), pad,
                           cols(w_in, 0, c_ra), cols(w_in, c_kb, c_vb), cols(w_in, c_vb, c_vb + bw)],
                          axis=1).astype(BF16)

    def shift_vec(mu):
        o = lambda c: c - c_ra
        return jnp.concatenate([mu[o(c_ra):o(c_gd)], mu[o(c_ka):o(c_va)], mu[o(c_va):o(c_wd)], mu[o(c_gd):o(c_ka)],
                                mu[o(c_wd):o(c_ad)], mu[o(c_ad):o(c_kb)], jnp.zeros((128,), F32)]).reshape(1, -1)

    blockdiag = lambda m: jnp.concatenate(
        [jnp.concatenate([m[0], jnp.zeros_like(m[0])], axis=1),
         jnp.concatenate([jnp.zeros_like(m[1]), m[1]], axis=1)], axis=0)
    head = jnp.arange(aw) // HEAD_DIM
    consts = {
        "mu_prev": shift_vec(mu_prev), "mu_next": shift_vec(mu_next),
        "w0": w0.reshape(1, 2 * aw), "w2": blockdiag(w2), "a0": a0.reshape(1, 2 * aw), "a2": blockdiag(a2),
        "g2": g2.astype(BF16), "key_k": key_k.reshape(1, aw), "key_a": key_a.reshape(1, aw),
        "r_k": r_k.reshape(1, aw), "seg": (head[:, None] == head[None, :]).astype(BF16),
    }

    g_row = norm_g.reshape(1, d)
    p = _in_proj(x, g_row, sh1, sc1, w_p)
    pc = _in_proj(ctx, g_row, shc, scc, w_p)
    z_m, bonus, gate = _rwkv_terms(p, consts)
    z_c, _, _ = _rwkv_terms(pc, consts)

    yf, yb = _wkv_scan(_to_scan_layout(z_c), _to_scan_layout(z_m))
    y = _from_scan_layout(yf, yb, b)

    qb = (4 * aw) // 128
    o_b = _neighbourhood_attention(p, pc, _na_bias_table(rpb, t // GRID_W), qb, qb + bw // 128, qb + 2 * bw // 128)
    return _mix_out(y, bonus, gate, o_b, x, g1, lnx_g.reshape(1, aw), lnx_b.reshape(1, aw), consts["seg"],
                    w_out[:aw].astype(BF16), w_out[aw:].astype(BF16))


def _odd_layer(x, mod, norm_g, pw1_w, pw1_b, dw_w, dw_b, ln_g, ln_b, pw2_w, pw2_b):
    b, t, d = x.shape
    sh1, sc1, g1 = (mod[:, None, i * d:(i + 1) * d] for i in range(3))
    u = _glu_proj(x, norm_g.reshape(1, d), sh1, sc1, pw1_w.astype(BF16), pw1_b.reshape(1, -1))
    dw = jnp.concatenate([dw_w, jnp.zeros((1, d), F32)], axis=0)
    return _conv_module(u, dw, dw_b.reshape(1, d), ln_g.reshape(1, d), ln_b.reshape(1, d), pw2_w.astype(BF16),
                        pw2_b.reshape(1, d), x, g1)


def kernel(x, c, ctx, c_ctx, ada_w, ada_b, norm_mix_g, norm_ffn_g, final_norm_g, mix_w_in, shift_mu_prev, shift_mu_next, decay_w0, decay_w2, iclr_a0, iclr_a2, gate_g2, key_k, key_a, bonus_r_k, lnx_g, lnx_b, na_rpb, mix_w_out, conv_pw1_w, conv_pw1_b, conv_dw_w, conv_dw_b, conv_ln_g, conv_ln_b, conv_pw2_w, conv_pw2_b, router_w, router_b, expert_w1, expert_b1, expert_w2, expert_b2):
    b, t, d = x.shape
    depth = ada_w.shape[0]
    rows = -(-(b + 1) // 8) * 8
    c_all = jnp.concatenate([c, c_ctx[None, :], jnp.zeros((rows - b - 1, d), F32)], axis=0)
    mod_all = _modulation(c_all, ada_w, ada_b)
    for l in range(depth):
        mod = mod_all[l, :b]
        i = l // 2
        if l % 2 == 0:
            x = _even_layer(x, ctx, mod, mod_all[l, b], norm_mix_g[l], mix_w_in[i], shift_mu_prev[i], shift_mu_next[i],
                            decay_w0[i], decay_w2[i], iclr_a0[i], iclr_a2[i], gate_g2[i], key_k[i], key_a[i],
                            bonus_r_k[i].reshape(-1), lnx_g[i], lnx_b[i], na_rpb[i], mix_w_out[i])
        else:
            x = _odd_layer(x, mod, norm_mix_g[l], conv_pw1_w[i], conv_pw1_b[i], conv_dw_w[i], conv_dw_b[i],
                           conv_ln_g[i], conv_ln_b[i], conv_pw2_w[i], conv_pw2_b[i])
        sh2, sc2, g2 = (mod[:, None, j * d:(j + 1) * d] for j in range(3, 6))
        x = _moe_layer(x, norm_ffn_g[l].reshape(1, d), sh2, sc2, g2, router_w[l], router_b[l], expert_w1, l,
                       expert_b1[l], expert_w2[l], expert_b2[l], final_norm_g, final=(l == depth - 1))
    return x
```

```python
import functools

import jax
import jax.numpy as jnp
import numpy as np
from jax import lax
from jax.experimental import pallas as pl
from jax.experimental.pallas import tpu as pltpu

F32 = jnp.float32
BF16 = jnp.bfloat16
HIGHEST = lax.Precision.HIGHEST

HEAD_DIM = 64
GRID_W = 64
NA_ROWS = 8
NA_COLS = 16
CONV_WIDTH = 31
N_EXPERTS = 32
TOP_K = 4
SWIGLU_ALPHA = 1.702
SWIGLU_LIMIT = 7.0
RMS_EPS = 1e-6
LN_EPS = 1e-5
GN_EPS = 64e-5
NEG_BIG = -1e30

VMEM_LIMIT_BYTES = 52 * 1024 * 1024
TOKEN_TILE = 256
SCAN_BLOCK = 16
EXPERT_ROWS = 512
HALO = 16
ISSUE_GROUP = 4
NA_GROUP = 4


def _params(*sem):
    return pltpu.CompilerParams(dimension_semantics=sem, vmem_limit_bytes=VMEM_LIMIT_BYTES)


def _adaln(x, g, sh, sc):
    y = x * lax.rsqrt(jnp.mean(x * x, axis=-1, keepdims=True) + RMS_EPS)
    return (y * g) * (1.0 + sc) + sh


def _sigmoid(x):
    return 1.0 / (1.0 + jnp.exp(-x))


def _split_dot(x, m):
    hi = x.astype(BF16)
    r1 = x - hi.astype(F32)
    mid = r1.astype(BF16)
    lo = (r1 - mid.astype(F32)).astype(BF16)
    dot = functools.partial(jnp.dot, preferred_element_type=F32)
    return dot(hi, m) + dot(mid, m) + dot(lo, m)


def _dot3(x, w):
    xh = x.astype(BF16)
    xl = (x - xh.astype(F32)).astype(BF16)
    wh = w.astype(BF16)
    wl = (w - wh.astype(F32)).astype(BF16)
    dot = functools.partial(jnp.dot, preferred_element_type=F32)
    return dot(xh, wh) + (dot(xl, wh) + dot(xh, wl))


def _mod_kernel(c_ref, w_ref, b_ref, o_ref):
    c = c_ref[...]
    s = c * _sigmoid(c)
    o_ref[0] = jnp.dot(s, w_ref[0], preferred_element_type=F32, precision=HIGHEST) + b_ref[0]


def _modulation(c_all, ada_w, ada_b):
    depth, d, n = ada_w.shape
    rows = c_all.shape[0]
    tn = 1536
    return pl.pallas_call(
        _mod_kernel,
        grid=(depth, n // tn),
        in_specs=[pl.BlockSpec((rows, d), lambda l, j: (0, 0)),
                  pl.BlockSpec((1, d, tn), lambda l, j: (l, 0, j)),
                  pl.BlockSpec((1, 1, tn), lambda l, j: (l, 0, j))],
        out_specs=pl.BlockSpec((1, rows, tn), lambda l, j: (l, 0, j)),
        out_shape=jax.ShapeDtypeStruct((depth, rows, n), F32),
        compiler_params=_params("arbitrary", "arbitrary"),
        name="modulation",
    )(c_all, ada_w, ada_b.reshape(depth, 1, n))


def _proj_kernel(x_ref, g_ref, sh_ref, sc_ref, w_ref, o_ref):
    h = _adaln(x_ref[0], g_ref[...], sh_ref[0], sc_ref[0]).astype(BF16)
    o_ref[0] = jnp.dot(h, w_ref[...], preferred_element_type=F32)


def _in_proj(x, g, sh, sc, w):
    b, t, d = x.shape
    n = w.shape[1]
    tt = min(TOKEN_TILE, t)
    return pl.pallas_call(
        _proj_kernel,
        grid=(b, t // tt),
        in_specs=[pl.BlockSpec((1, tt, d), lambda i, j: (i, j, 0)),
                  pl.BlockSpec((1, d), lambda i, j: (0, 0)),
                  pl.BlockSpec((1, 1, d), lambda i, j: (i, 0, 0)),
                  pl.BlockSpec((1, 1, d), lambda i, j: (i, 0, 0)),
                  pl.BlockSpec((d, n), lambda i, j: (0, 0))],
        out_specs=pl.BlockSpec((1, tt, n), lambda i, j: (i, j, 0)),
        out_shape=jax.ShapeDtypeStruct((b, t, n), F32),
        compiler_params=_params("arbitrary", "arbitrary"),
        name="in_proj",
    )(x, g, sh, sc, w)


def _terms_kernel(p_ref, pp_ref, pn_ref, mup_ref, mun_ref, w0_ref, w2_ref, a0_ref, a2_ref, g2_ref,
                  kk_ref, ka_ref, rk_ref, seg_ref, z_ref, bonus_ref, gate_ref, *, n_tiles):
    t = pl.program_id(1)
    p = p_ref[0]
    tt, aw = p.shape[0], kk_ref.shape[1]
    prev_row = jnp.where(t > 0, pp_ref[0, 7:8, :], 0.0)
    next_row = jnp.where(t < n_tiles - 1, pn_ref[0, 0:1, :], 0.0)
    rows = lax.broadcasted_iota(jnp.int32, p.shape, 0)
    prev = jnp.where(rows == 0, prev_row, pltpu.roll(p, 1, axis=0))
    nxt = jnp.where(rows == tt - 1, next_row, pltpu.roll(p, tt - 1, axis=0))
    s = p + mup_ref[...] * (prev - p) + mun_ref[...] * (nxt - p)
    r, k, v = s[:, :aw], s[:, aw:2 * aw], s[:, 2 * aw:3 * aw]
    lora = s[:, 3 * aw:]
    g_in, wd, ad = lora[:, 0:128], lora[:, 128:256], lora[:, 256:384]
    dotf = _dot3
    zw = -(w0_ref[...] + dotf(jnp.tanh(wd), w2_ref[...]))
    softplus = jnp.maximum(zw, 0.0) + jnp.log(1.0 + jnp.exp(-jnp.abs(zw)))
    decay = jnp.exp(-jnp.exp(-softplus - 0.5))
    a = _sigmoid(a0_ref[...] + dotf(ad, a2_ref[...]))
    seg = seg_ref[...]
    kk = k * kk_ref[...]
    kk = kk / jnp.maximum(jnp.sqrt(_split_dot(kk * kk, seg)), 1e-12)
    z_ref[0, 0] = kk
    z_ref[1, 0] = v
    z_ref[2, 0] = r
    kd_sum = jnp.zeros_like(k)
    for d in range(2):
        a_d = a[:, d * aw:(d + 1) * aw]
        k_dir = k * (1.0 + (a_d - 1.0) * ka_ref[...])
        z_ref[3 + 3 * d, 0] = decay[:, d * aw:(d + 1) * aw]
        z_ref[4 + 3 * d, 0] = k_dir
        z_ref[5 + 3 * d, 0] = kk * a_d
        kd_sum = kd_sum + k_dir
    bonus_ref[0] = _split_dot(r * kd_sum * rk_ref[...], seg) * v
    gate_ref[0] = jnp.dot(_sigmoid(g_in).astype(BF16), g2_ref[...], preferred_element_type=F32)


def _rwkv_terms(p, consts):
    b, t, _ = p.shape
    aw = consts["key_k"].shape[1]
    sw = 4 * aw
    tt = min(TOKEN_TILE, t)
    n_tiles = t // tt
    hb = tt // 8
    full = lambda a: pl.BlockSpec(a.shape, lambda i, j: (0,) * a.ndim)
    names = ("mu_prev", "mu_next", "w0", "w2", "a0", "a2", "g2", "key_k", "key_a", "r_k", "seg")
    cs = [consts[n] for n in names]
    out3 = jax.ShapeDtypeStruct((b, t, aw), F32)
    return pl.pallas_call(
        functools.partial(_terms_kernel, n_tiles=n_tiles),
        grid=(b, n_tiles),
        in_specs=[pl.BlockSpec((1, tt, sw), lambda i, j: (i, j, 0)),
                  pl.BlockSpec((1, 8, sw), lambda i, j: (i, jnp.maximum(j * hb - 1, 0), 0)),
                  pl.BlockSpec((1, 8, sw), lambda i, j: (i, jnp.minimum((j + 1) * hb, t // 8 - 1), 0))]
                 + [full(a) for a in cs],
        out_specs=[pl.BlockSpec((9, 1, tt, aw), lambda i, j: (0, i, j, 0)),
                   pl.BlockSpec((1, tt, aw), lambda i, j: (i, j, 0)),
                   pl.BlockSpec((1, tt, aw), lambda i, j: (i, j, 0))],
        out_shape=[jax.ShapeDtypeStruct((9, b, t, aw), F32), out3, out3],
        compiler_params=_params("arbitrary", "arbitrary"),
        name="rwkv_terms",
    )(p, p, p, *cs)


def _scan_kernel(csf_ref, cdf_ref, csb_ref, cdb_ref, zsf_ref, zdf_ref, zsb_ref, zdb_ref, yf_ref, yb_ref, s_ref,
                 *, tb, nc):
    g = pl.program_id(0)

    @pl.when(g == 0)
    def _():
        s_ref[...] = jnp.zeros_like(s_ref)

    n = s_ref.shape[1]

    def run(dirs):
        def step(tf, carry):
            tidx = (tf, tb - 1 - tf)
            vecs = []
            for d, (zs, zd, _) in enumerate(dirs):
                ti = tidx[d]
                kk, r = zs[ti, 0], zs[ti, 2]
                w, kd, bb = zd[ti, 0], zd[ti, 1], zd[ti, 2]
                bbr = jnp.sum(bb * r, axis=0, keepdims=True)
                kr = jnp.sum(kd * r, axis=0, keepdims=True)
                vecs.append((kk, w * r, w, bb, kd, bbr, kr))

            def row(i, c):
                for d, (zs, _, y_ref) in enumerate(dirs):
                    kk, wr, w, bb, kd, bbr, kr = vecs[d]
                    ti = tidx[d]
                    si = s_ref[d, i]
                    sa = -jnp.sum(si * kk, axis=0, keepdims=True)
                    vi = zs[ti, 1, pl.ds(i, 1), :]
                    s_ref[d, i] = si * w + sa * bb + vi * kd
                    if y_ref is not None:
                        y0 = jnp.sum(si * wr, axis=0, keepdims=True)
                        y_ref[ti, pl.ds(i, 1), :] = y0 + sa * bbr + vi * kr
                return c

            lax.fori_loop(0, n, row, 0, unroll=8)
            return carry

        lax.fori_loop(0, tb, step, 0)

    @pl.when(g < nc)
    def _():
        run(((csf_ref, cdf_ref, None), (csb_ref, cdb_ref, None)))

    @pl.when(g >= nc)
    def _():
        run(((zsf_ref, zdf_ref, yf_ref), (zsb_ref, zdb_ref, yb_ref)))


def _wkv_scan(zc, zm):
    n, lanes = zm.shape[2:]
    tb = SCAN_BLOCK
    nc, nm = zc.shape[0] // tb, zm.shape[0] // tb
    cf = lambda g: jnp.minimum(g, nc - 1)
    cb = lambda g: jnp.maximum(nc - 1 - g, 0)
    mf = lambda g: jnp.maximum(g - nc, 0)
    mb = lambda g: jnp.minimum(nm - 1, nm - 1 + nc - g)
    blk = (tb, 3, n, lanes)
    spec = lambda t_of, part: pl.BlockSpec(blk, lambda g: (t_of(g), part, 0, 0))
    y_shape = jax.ShapeDtypeStruct((nm * tb, n, lanes), F32)
    return pl.pallas_call(
        functools.partial(_scan_kernel, tb=tb, nc=nc),
        grid=(nc + nm,),
        in_specs=[spec(cf, 0), spec(cf, 1), spec(cb, 0), spec(cb, 2),
                  spec(mf, 0), spec(mf, 1), spec(mb, 0), spec(mb, 2)],
        out_specs=[pl.BlockSpec((tb, n, lanes), lambda g: (mf(g), 0, 0)),
                   pl.BlockSpec((tb, n, lanes), lambda g: (mb(g), 0, 0))],
        out_shape=[y_shape, y_shape],
        scratch_shapes=[pltpu.VMEM((2, n, n, lanes), F32)],
        compiler_params=_params("arbitrary"),
        name="wkv_scan",
    )(zc, zc, zc, zc, zm, zm, zm, zm)


def _to_scan_kernel(x_ref, o_ref):
    n_comp, nb, tt, aw = x_ref.shape
    n_head = aw // HEAD_DIM
    low = lax.broadcasted_iota(jnp.int32, (nb, 128), 1) < HEAD_DIM

    def comp(c, carry):
        for tp in range(tt // 2):
            a = x_ref[c, :, 2 * tp, :]
            b = x_ref[c, :, 2 * tp + 1, :]
            pieces = []
            for h in range(n_head):
                ls = slice((h // 2) * 128, (h // 2 + 1) * 128)
                am, bm = a[:, ls], b[:, ls]
                if h % 2 == 0:
                    pieces.append(jnp.where(low, am, pltpu.roll(bm, HEAD_DIM, axis=1)))
                else:
                    pieces.append(jnp.where(low, pltpu.roll(am, HEAD_DIM, axis=1), bm))
            r2 = jnp.concatenate(pieces, axis=0).T
            o_ref[2 * tp, c] = r2[:HEAD_DIM]
            o_ref[2 * tp + 1, c] = r2[HEAD_DIM:]
        return carry

    lax.fori_loop(0, n_comp, comp, 0)


def _to_scan_layout(z):
    n_comp, b, t, aw = z.shape
    tt = SCAN_BLOCK
    lanes = (aw // HEAD_DIM) * b
    return pl.pallas_call(
        _to_scan_kernel,
        grid=(t // tt,),
        in_specs=[pl.BlockSpec((n_comp, b, tt, aw), lambda i: (0, 0, i, 0))],
        out_specs=pl.BlockSpec((tt, n_comp, HEAD_DIM, lanes), lambda i: (i, 0, 0, 0)),
        out_shape=jax.ShapeDtypeStruct((t, n_comp, HEAD_DIM, lanes), F32),
        compiler_params=_params("arbitrary"),
        name="to_scan_layout",
    )(z)


def _from_scan_kernel(yf_ref, yb_ref, o_ref):
    tt = yf_ref.shape[0]
    nb, _, aw = o_ref.shape
    n_head = aw // HEAD_DIM
    low = lax.broadcasted_iota(jnp.int32, (nb, 128), 1) < HEAD_DIM
    for tp in range(tt // 2):
        s = jnp.concatenate([yf_ref[2 * tp] + yb_ref[2 * tp], yf_ref[2 * tp + 1] + yb_ref[2 * tp + 1]], axis=0)
        r2 = s.T
        for m in range(n_head // 2):
            pe = r2[(2 * m) * nb:(2 * m + 1) * nb]
            po = r2[(2 * m + 1) * nb:(2 * m + 2) * nb]
            ls = slice(m * 128, (m + 1) * 128)
            o_ref[:, 2 * tp, ls] = jnp.where(low, pe, pltpu.roll(po, HEAD_DIM, axis=1))
            o_ref[:, 2 * tp + 1, ls] = jnp.where(low, pltpu.roll(pe, HEAD_DIM, axis=1), po)


def _from_scan_layout(yf, yb, b):
    t, n, lanes = yf.shape
    aw = (lanes // b) * n
    tt = SCAN_BLOCK
    return pl.pallas_call(
        _from_scan_kernel,
        grid=(t // tt,),
        in_specs=[pl.BlockSpec((tt, n, lanes), lambda i: (i, 0, 0))] * 2,
        out_specs=pl.BlockSpec((b, tt, aw), lambda i: (0, i, 0)),
        out_shape=jax.ShapeDtypeStruct((b, t, aw), F32),
        compiler_params=_params("arbitrary"),
        name="from_scan_layout",
    )(yf, yb)


def _na_kernel(q_ref, k_ref, v_ref, kc_ref, vc_ref, bias_ref, o_ref, kb_ref, vb_ref, kcb_ref, vcb_ref, *, rows):
    kh = NA_ROWS
    dn = (((1,), (1,)), ((), ()))
    kb_ref[...] = k_ref[0].astype(BF16)
    vb_ref[...] = v_ref[0].astype(BF16)
    kcb_ref[...] = kc_ref[0].astype(BF16)
    vcb_ref[...] = vc_ref[0].astype(BF16)
    nq = NA_GROUP * GRID_W
    nk = (kh + NA_GROUP - 1) * GRID_W
    n_groups = rows // NA_GROUP
    head_of_lane = lax.broadcasted_iota(jnp.int32, (nq, 2 * HEAD_DIM), 1) // HEAD_DIM

    def group(g, c):
        u = _na_union_start(g, rows)
        pat = jnp.where(g > 0, 1, 0) + jnp.where(g == n_groups - 1, 1, 0)
        q0 = pl.multiple_of(g * nq, nq)
        k0 = pl.multiple_of(u * GRID_W, GRID_W)
        q2 = q_ref[0, pl.ds(q0, nq), :] * (HEAD_DIM ** -0.5)
        kl = kb_ref[pl.ds(k0, nk), :]
        vl = vb_ref[pl.ds(k0, nk), :]
        out = jnp.zeros((nq, 2 * HEAD_DIM), F32)
        for hh in range(2):
            q = jnp.where(head_of_lane == hh, q2, 0.0).astype(BF16)
            s_loc = lax.dot_general(q, kl, dn, preferred_element_type=F32) + bias_ref[pat, hh]
            s_ctx = lax.dot_general(q, kcb_ref[...], dn, preferred_element_type=F32)
            m = jnp.maximum(jnp.max(s_loc, axis=-1, keepdims=True), jnp.max(s_ctx, axis=-1, keepdims=True))
            e_loc = jnp.exp(s_loc - m)
            e_ctx = jnp.exp(s_ctx - m)
            den = jnp.sum(e_loc, axis=-1, keepdims=True) + jnp.sum(e_ctx, axis=-1, keepdims=True)
            o = (jnp.dot(e_loc.astype(BF16), vl, preferred_element_type=F32)
                 + jnp.dot(e_ctx.astype(BF16), vcb_ref[...], preferred_element_type=F32))
            out = jnp.where(head_of_lane == hh, o / den, out)
        o_ref[0, pl.ds(q0, nq), :] = out.astype(o_ref.dtype)
        return c

    lax.fori_loop(0, n_groups, group, 0)


def _na_union_start(g, rows):
    lo = NA_GROUP * g - NA_ROWS // 2
    hi = rows - (NA_ROWS + NA_GROUP - 1)
    if isinstance(g, int):
        return min(max(lo, 0), hi)
    return jnp.clip(lo, 0, hi)


def _neighbourhood_attention(p, pc, bias, col_q, col_k, col_v):
    b, t, _ = p.shape
    l = pc.shape[1]
    rows = t // GRID_W
    n_pairs = bias.shape[1] // 2
    return pl.pallas_call(
        functools.partial(_na_kernel, rows=rows),
        grid=(n_pairs, b),
        in_specs=[pl.BlockSpec((1, t, 128), lambda h, i: (i, 0, col_q + h)),
                  pl.BlockSpec((1, t, 128), lambda h, i: (i, 0, col_k + h)),
                  pl.BlockSpec((1, t, 128), lambda h, i: (i, 0, col_v + h)),
                  pl.BlockSpec((1, l, 128), lambda h, i: (i, 0, col_k + h)),
                  pl.BlockSpec((1, l, 128), lambda h, i: (i, 0, col_v + h)),
                  pl.BlockSpec((bias.shape[0], 2) + bias.shape[2:], lambda h, i: (0, h, 0, 0))],
        out_specs=pl.BlockSpec((1, t, 128), lambda h, i: (i, 0, h)),
        out_shape=jax.ShapeDtypeStruct((b, t, n_pairs * 128), BF16),
        scratch_shapes=[pltpu.VMEM((t, 128), BF16), pltpu.VMEM((t, 128), BF16),
                        pltpu.VMEM((l, 128), BF16), pltpu.VMEM((l, 128), BF16)],
        compiler_params=_params("arbitrary", "arbitrary"),
        name="na_attention",
    )(p, p, p, pc, pc, bias)


def _na_bias_table(rpb, rows):
    h = rpb.shape[0]
    n_groups = rows // NA_GROUP
    assert rows % NA_GROUP == 0 and rows >= NA_ROWS + 2 * NA_GROUP - 1
    col = np.arange(GRID_W)
    c_start = np.clip(col - NA_COLS // 2, 0, GRID_W - NA_COLS)
    col_ok = (col[None, :] >= c_start[:, None]) & (col[None, :] < c_start[:, None] + NA_COLS)
    dc = np.clip(col[None, :] - col[:, None], 1 - NA_COLS, NA_COLS - 1) + NA_COLS - 1
    pick = (dc.reshape(1, -1) == np.arange(2 * NA_COLS - 1)[:, None]).astype(np.float32)
    t = jnp.einsum("hrc,cx->hrx", rpb, pick, precision=HIGHEST)
    t = jnp.where(col_ok.reshape(-1), t, NEG_BIG).reshape(h, 2 * NA_ROWS - 1, GRID_W, GRID_W)
    masked = jnp.full((h, GRID_W, GRID_W), NEG_BIG, F32)
    n_union = NA_ROWS + NA_GROUP - 1
    pats = []
    for g in (0, 1, n_groups - 1):
        u = _na_union_start(g, rows)
        blocks = []
        for ri in range(NA_GROUP):
            r = g * NA_GROUP + ri
            r_start = min(max(r - NA_ROWS // 2, 0), rows - NA_ROWS)
            row_blocks = []
            for kr in range(n_union):
                key_row = u + kr
                inside = r_start <= key_row < r_start + NA_ROWS
                row_blocks.append(t[:, key_row - r + NA_ROWS - 1] if inside else masked)
            blocks.append(jnp.stack(row_blocks, axis=2))
        pats.append(jnp.stack(blocks, axis=1))
    tab = jnp.stack(pats, axis=0)
    return tab.reshape(3, h, NA_GROUP * GRID_W, n_union * GRID_W)


def _mix_out_kernel(y_ref, bonus_ref, gate_ref, ob_ref, x_ref, g1_ref, lg_ref, lb_ref, seg_ref, wa_ref, wb_ref, o_ref):
    y = y_ref[0]
    seg = seg_ref[...]
    inv = 1.0 / HEAD_DIM
    mu = _split_dot(y, seg) * inv
    yc = y - mu
    var = _split_dot(yc * yc, seg) * inv
    yn = (yc * lax.rsqrt(var + GN_EPS)) * lg_ref[...] + lb_ref[...]
    o_a = ((yn + bonus_ref[0]) * gate_ref[0]).astype(BF16)
    out = (jnp.dot(o_a, wa_ref[...], preferred_element_type=F32)
           + jnp.dot(ob_ref[0], wb_ref[...], preferred_element_type=F32))
    o_ref[0] = x_ref[0] + g1_ref[0] * out


def _mix_out(y, bonus, gate, o_b, x, g1, lnx_g, lnx_b, seg, w_a, w_b):
    b, t, d = x.shape
    aw = y.shape[2]
    tt = min(TOKEN_TILE, t)
    tok = lambda w: pl.BlockSpec((1, tt, w), lambda i, j: (i, j, 0))
    full = lambda a: pl.BlockSpec(a.shape, lambda i, j: (0,) * a.ndim)
    return pl.pallas_call(
        _mix_out_kernel,
        grid=(b, t // tt),
        in_specs=[tok(aw), tok(aw), tok(aw), tok(o_b.shape[2]), tok(d),
                  pl.BlockSpec((1, 1, d), lambda i, j: (i, 0, 0)),
                  full(lnx_g), full(lnx_b), full(seg), full(w_a), full(w_b)],
        out_specs=tok(d),
        out_shape=jax.ShapeDtypeStruct((b, t, d), F32),
        compiler_params=_params("arbitrary", "arbitrary"),
        name="mix_out",
    )(y, bonus, gate, o_b, x, g1, lnx_g, lnx_b, seg, w_a, w_b)


def _glu_kernel(x_ref, g_ref, sh_ref, sc_ref, w_ref, b_ref, o_ref):
    h = _adaln(x_ref[0], g_ref[...], sh_ref[0], sc_ref[0]).astype(BF16)
    u = jnp.dot(h, w_ref[...], preferred_element_type=F32) + b_ref[...]
    d = u.shape[1] // 2
    o_ref[0] = u[:, :d] * _sigmoid(u[:, d:])


def _glu_proj(x, g, sh, sc, w, bias):
    b, t, d = x.shape
    n = w.shape[1]
    tt = min(TOKEN_TILE, t)
    return pl.pallas_call(
        _glu_kernel,
        grid=(b, t // tt),
        in_specs=[pl.BlockSpec((1, tt, d), lambda i, j: (i, j, 0)),
                  pl.BlockSpec((1, d), lambda i, j: (0, 0)),
                  pl.BlockSpec((1, 1, d), lambda i, j: (i, 0, 0)),
                  pl.BlockSpec((1, 1, d), lambda i, j: (i, 0, 0)),
                  pl.BlockSpec((d, n), lambda i, j: (0, 0)),
                  pl.BlockSpec((1, n), lambda i, j: (0, 0))],
        out_specs=pl.BlockSpec((1, tt, n // 2), lambda i, j: (i, j, 0)),
        out_shape=jax.ShapeDtypeStruct((b, t, n // 2), F32),
        compiler_params=_params("arbitrary", "arbitrary"),
        name="glu_proj",
    )(x, g, sh, sc, w, bias)


def _conv_kernel(u_ref, up_ref, un_ref, dw_ref, dwb_ref, lg_ref, lb_ref, w2_ref, b2_ref, x_ref, g1_ref, o_ref,
                 win_ref, acc_ref, *, n_tiles):
    t = pl.program_id(1)
    tt, d = u_ref.shape[1], u_ref.shape[2]
    half = CONV_WIDTH // 2
    win_ref[0:HALO, :] = jnp.where(t > 0, up_ref[0], 0.0)
    win_ref[HALO:HALO + tt, :] = u_ref[0]
    win_ref[HALO + tt:2 * HALO + tt, :] = jnp.where(t < n_tiles - 1, un_ref[0], 0.0)
    rc = 64
    first = HALO - half

    for base in range(0, tt, rc):
        for lc in range(d // 128):
            ls = slice(lc * 128, (lc + 1) * 128)
            out = None
            for s in range(8):
                acc = None
                for a in range((first + CONV_WIDTH - 1 - s) // 8 + 1):
                    k = 8 * a + s - first
                    if k < 0:
                        continue
                    term = dw_ref[k:k + 1, ls] * win_ref[base + 8 * a:base + 8 * a + rc + 8, ls]
                    acc = term if acc is None else acc + term
                part = acc[s:s + rc]
                out = part if out is None else out + part
            acc_ref[base:base + rc, ls] = out
    u = acc_ref[...] + dwb_ref[...]
    mu = jnp.mean(u, axis=-1, keepdims=True)
    uc = u - mu
    var = jnp.mean(uc * uc, axis=-1, keepdims=True)
    un = (uc * lax.rsqrt(var + LN_EPS)) * lg_ref[...] + lb_ref[...]
    act = (un * _sigmoid(un)).astype(BF16)
    out = jnp.dot(act, w2_ref[...], preferred_element_type=F32) + b2_ref[...]
    o_ref[0] = x_ref[0] + g1_ref[0] * out


def _conv_module(u, dw, dwb, ln_g, ln_b, w2, b2, x, g1):
    b, t, d = x.shape
    tt = min(TOKEN_TILE, t)
    n_tiles = t // tt
    hb = tt // HALO
    tok = pl.BlockSpec((1, tt, d), lambda i, j: (i, j, 0))
    full = lambda a: pl.BlockSpec(a.shape, lambda i, j: (0,) * a.ndim)
    return pl.pallas_call(
        functools.partial(_conv_kernel, n_tiles=n_tiles),
        grid=(b, n_tiles),
        in_specs=[tok,
                  pl.BlockSpec((1, HALO, d), lambda i, j: (i, jnp.maximum(j * hb - 1, 0), 0)),
                  pl.BlockSpec((1, HALO, d), lambda i, j: (i, jnp.minimum((j + 1) * hb, t // HALO - 1), 0)),
                  full(dw), full(dwb), full(ln_g), full(ln_b), full(w2), full(b2), tok,
                  pl.BlockSpec((1, 1, d), lambda i, j: (i, 0, 0))],
        out_specs=tok,
        out_shape=jax.ShapeDtypeStruct((b, t, d), F32),
        scratch_shapes=[pltpu.VMEM((tt + 2 * HALO, d), F32), pltpu.VMEM((tt, d), F32)],
        compiler_params=_params("arbitrary", "arbitrary"),
        name="conv_module",
    )(u, u, u, dw, dwb, ln_g, ln_b, w2, b2, x, g1)


def _route_kernel(x_ref, g_ref, sh_ref, sc_ref, wr_ref, br_ref, hp_ref, route_ref, cnt_ref, carry_ref):
    @pl.when((pl.program_id(0) == 0) & (pl.program_id(1) == 0))
    def _():
        carry_ref[...] = jnp.zeros_like(carry_ref)

    h = _adaln(x_ref[0], g_ref[...], sh_ref[0], sc_ref[0])
    tt, d = h.shape
    hi = lax.bitcast_convert_type(h[:, :d // 2].astype(BF16).astype(F32), jnp.uint32)
    lo = lax.bitcast_convert_type(h[:, d // 2:].astype(BF16).astype(F32), jnp.uint32)
    packed = (hi & jnp.uint32(0xFFFF0000)) | (lo >> 16)
    n_ch = d // 2 // 128
    for c in range(n_ch):
        hp_ref[0, pl.ds(c, tt, stride=n_ch), :] = packed[:, c * 128:(c + 1) * 128]

    logits = _dot3(h, wr_ref[...]) + br_ref[...]
    ne = logits.shape[1]
    lane = lax.broadcasted_iota(jnp.int32, (tt, ne), 1).astype(F32)
    work = logits
    mask = jnp.zeros((tt, ne), F32)
    picks, es = [], []
    den = jnp.zeros((tt, 1), F32)
    for k in range(TOP_K):
        m = jnp.max(work, axis=-1, keepdims=True)
        idx = jnp.min(jnp.where(work == m, lane, float(ne)), axis=-1, keepdims=True)
        pick = lane == idx
        if k == 0:
            top = m
        e = jnp.exp(m - top)
        den = den + e
        picks.append((pick, idx))
        es.append(e)
        mask = jnp.where(pick, 1.0, mask)
        work = jnp.where(pick, -jnp.inf, work)

    ri = lax.broadcasted_iota(jnp.int32, (tt, tt), 0)
    ci = lax.broadcasted_iota(jnp.int32, (tt, tt), 1)
    lower = jnp.where(ci < ri, 1.0, 0.0).astype(BF16)
    rank = jnp.dot(lower, mask.astype(BF16), preferred_element_type=F32) + carry_ref[...]
    carry_ref[...] = carry_ref[...] + jnp.sum(mask, axis=0, keepdims=True)
    cnt_ref[...] = carry_ref[...]

    out_lane = lax.broadcasted_iota(jnp.int32, (tt, 128), 1)
    route = jnp.zeros((tt, 128), F32)
    for k in range(TOP_K):
        pick, idx = picks[k]
        rk = jnp.sum(jnp.where(pick, rank, 0.0), axis=-1, keepdims=True)
        route = jnp.where(out_lane == k, idx, route)
        route = jnp.where(out_lane == TOP_K + k, rk, route)
        route = jnp.where(out_lane == 2 * TOP_K + k, es[k] / den, route)
    route_ref[0] = route


def _route(x, g, sh, sc, w_r, b_r):
    b, t, d = x.shape
    ne = w_r.shape[1]
    tt = min(TOKEN_TILE, t)
    return pl.pallas_call(
        _route_kernel,
        grid=(b, t // tt),
        in_specs=[pl.BlockSpec((1, tt, d), lambda i, j: (i, j, 0)),
                  pl.BlockSpec((1, d), lambda i, j: (0, 0)),
                  pl.BlockSpec((1, 1, d), lambda i, j: (i, 0, 0)),
                  pl.BlockSpec((1, 1, d), lambda i, j: (i, 0, 0)),
                  pl.BlockSpec((d, ne), lambda i, j: (0, 0)),
                  pl.BlockSpec((1, ne), lambda i, j: (0, 0))],
        out_specs=[pl.BlockSpec((1, tt * (d // 256), 128), lambda i, j: (i, j, 0)),
                   pl.BlockSpec((1, tt, 128), lambda i, j: (i, j, 0)),
                   pl.BlockSpec((1, ne), lambda i, j: (0, 0))],
        out_shape=[jax.ShapeDtypeStruct((b, t * (d // 256), 128), jnp.uint32),
                   jax.ShapeDtypeStruct((b, t, 128), F32),
                   jax.ShapeDtypeStruct((1, ne), F32)],
        scratch_shapes=[pltpu.VMEM((1, ne), F32)],
        compiler_params=_params("arbitrary", "arbitrary"),
        name="moe_route",
    )(x, g, sh, sc, w_r, b_r)


def _dispatch_kernel(pos_hbm, hp_ref, xs_in, xs_out, idx_ref, sem_idx, sem_rows, *, tile, n_tiles, rc):
    del xs_in
    n_idx = tile * TOP_K

    def idx_copy(i, slot):
        return pltpu.make_async_copy(pos_hbm.at[i], idx_ref.at[pl.ds(slot * n_idx, n_idx)], sem_idx.at[slot])

    i = pl.program_id(0)
    slot = i % 2

    @pl.when(i == 0)
    def _():
        idx_copy(0, 0).start()

    idx_copy(i, slot).wait()

    @pl.when(i + 1 < n_tiles)
    def _():
        idx_copy(i + 1, 1 - slot).start()

    ibase = slot * n_idx

    def issue(jg, c2):
        j0 = jg * ISSUE_GROUP
        rows = [idx_ref[ibase + j0 * TOP_K + q] for q in range(ISSUE_GROUP * TOP_K)]
        for q, row in enumerate(rows):
            src = hp_ref.at[pl.ds(pl.multiple_of((j0 + q // TOP_K) * rc, rc), rc)]
            copy = pltpu.make_async_copy(src, xs_out.at[pl.ds(pl.multiple_of(row, rc), rc)], sem_rows)
            copy.start(priority=q % 2)
        return c2

    lax.fori_loop(0, tile // ISSUE_GROUP, issue, 0)
    for _ in range(TOP_K):
        pltpu.make_async_copy(hp_ref, xs_out.at[pl.ds(0, tile * rc)], sem_rows).wait()


def _dispatch(pos_flat, hp, n_rows, rc):
    n, w = hp.shape[0] // rc, hp.shape[1]
    tile = TOKEN_TILE
    xs0 = jnp.zeros((n_rows * rc, w), jnp.uint32)
    any_spec = pl.BlockSpec(memory_space=pl.ANY)
    return pl.pallas_call(
        functools.partial(_dispatch_kernel, tile=tile, n_tiles=n // tile, rc=rc),
        grid=(n // tile,),
        in_specs=[any_spec, pl.BlockSpec((tile * rc, w), lambda i: (i, 0)), any_spec],
        out_specs=any_spec,
        out_shape=jax.ShapeDtypeStruct((n_rows * rc, w), jnp.uint32),
        scratch_shapes=[pltpu.SMEM((2 * tile * TOP_K,), jnp.int32),
                        pltpu.SemaphoreType.DMA((2,)), pltpu.SemaphoreType.DMA],
        input_output_aliases={2: 0},
        compiler_params=_params("arbitrary"),
        name="moe_dispatch",
    )(pos_flat.reshape(n // tile, tile * TOP_K), hp, xs0)


def _split_w1_kernel(w_ref, p_ref, g_ref, l_ref):
    w = w_ref[0, 0].astype(BF16)
    n = w.shape[1]
    for c in range(n // 256):
        res = jnp.dot(w[:, c * 256:(c + 1) * 256], p_ref[...], preferred_element_type=F32)
        g_ref[0, :, c * 128:(c + 1) * 128] = res[:, :128].astype(BF16)
        l_ref[0, :, c * 128:(c + 1) * 128] = res[:, 128:].astype(BF16)


def _split_w1(w1, layer):
    _, ne, d, f2 = w1.shape
    rows = 512
    r = np.arange(256)[:, None]
    c = np.arange(256)[None, :]
    sel = jnp.asarray(np.where(c < 128, r == 2 * c, r == 2 * (c - 128) + 1), BF16)
    out = jax.ShapeDtypeStruct((ne, d, f2 // 2), BF16)
    return pl.pallas_call(
        _split_w1_kernel,
        grid=(ne, d // rows),
        in_specs=[pl.BlockSpec((1, 1, rows, f2), lambda e, i: (layer, e, i, 0)),
                  pl.BlockSpec((256, 256), lambda e, i: (0, 0))],
        out_specs=[pl.BlockSpec((1, rows, f2 // 2), lambda e, i: (e, i, 0))] * 2,
        out_shape=[out, out],
        compiler_params=_params("arbitrary", "arbitrary"),
        name="split_w1",
    )(w1, sel)


def _expert_kernel(be_ref, nb_ref, xs_ref, w1g_ref, w1l_ref, b1g_ref, b1l_ref, w2_ref, b2_ref, ys_ref, *, bm):
    del be_ref
    rc = xs_ref.shape[0] // bm
    oc = ys_ref.shape[0] // bm

    @pl.when(pl.program_id(0) < nb_ref[0])
    def _():
        u = jnp.concatenate([xs_ref[pl.ds(c, bm, stride=rc), :] for c in range(rc)], axis=1)
        half = u.shape[1]
        xa = lax.bitcast_convert_type(u & jnp.uint32(0xFFFF0000), F32).astype(BF16)
        xb = lax.bitcast_convert_type(u << 16, F32).astype(BF16)
        dot = functools.partial(jnp.dot, preferred_element_type=F32)
        ug = dot(xa, w1g_ref[0, :half, :]) + dot(xb, w1g_ref[0, half:, :]) + b1g_ref[0]
        ul = dot(xa, w1l_ref[0, :half, :]) + dot(xb, w1l_ref[0, half:, :]) + b1l_ref[0]
        glu = jnp.minimum(ug, SWIGLU_LIMIT)
        lin = jnp.clip(ul, -SWIGLU_LIMIT, SWIGLU_LIMIT)
        act = (glu * _sigmoid(SWIGLU_ALPHA * glu)) * (lin + 1.0)
        y = dot(act.astype(BF16), w2_ref[0]) + b2_ref[0]
        for c in range(oc):
            ys_ref[pl.ds(c, bm, stride=oc), :] = y[:, c * 128:(c + 1) * 128]

    @pl.when(pl.program_id(0) >= nb_ref[0])
    def _():
        ys_ref[...] = jnp.zeros_like(ys_ref)


def _expert_ffn(block_e, n_used, xs, w1g, w1l, b1g, b1l, w2, b2, rc):
    n_rows = xs.shape[0] // rc
    ne, d, f = w1g.shape
    oc = d // 128
    bm = EXPERT_ROWS
    n_blocks = n_rows // bm
    wspec = lambda s: pl.BlockSpec((1,) + s, lambda i, be, nb: (be[i], 0, 0))
    return pl.pallas_call(
        functools.partial(_expert_kernel, bm=bm),
        grid_spec=pltpu.PrefetchScalarGridSpec(
            num_scalar_prefetch=2,
            grid=(n_blocks,),
            in_specs=[pl.BlockSpec((bm * rc, 128), lambda i, be, nb: (i, 0)),
                      wspec((d, f)), wspec((d, f)), wspec((1, f)), wspec((1, f)), wspec((f, d)), wspec((1, d))],
            out_specs=pl.BlockSpec((bm * oc, 128), lambda i, be, nb: (i, 0)),
        ),
        out_shape=jax.ShapeDtypeStruct((n_rows * oc, 128), F32),
        compiler_params=_params("arbitrary"),
        name="moe_experts",
    )(block_e, n_used, xs, w1g, w1l, b1g, b1l, w2, b2)


def _combine_kernel(pos_hbm, ys_hbm, x_ref, route_ref, g2_ref, fg_ref, o_ref, buf_ref, idx_ref, sem_idx, sem_rows,
                    *, tile, n_tiles, final):
    i = pl.program_id(0)
    n_idx = tile * TOP_K
    slot = i % 2
    oc = x_ref.shape[1] // 128

    def idx_copy(t, s):
        src = pos_hbm.at[pl.ds(pl.multiple_of(t * n_idx, n_idx), n_idx)]
        return pltpu.make_async_copy(src, idx_ref.at[pl.ds(s * n_idx, n_idx)], sem_idx.at[s])

    def gather(s):
        ibase = s * n_idx

        def issue(jg, c):
            j0 = jg * ISSUE_GROUP
            rows = [idx_ref[ibase + j0 * TOP_K + q] for q in range(ISSUE_GROUP * TOP_K)]
            for q, row in enumerate(rows):
                dst_row = pl.multiple_of((j0 + q // TOP_K) * oc, oc)
                copy = pltpu.make_async_copy(ys_hbm.at[pl.ds(pl.multiple_of(row, oc), oc)],
                                             buf_ref.at[s, q % TOP_K, pl.ds(dst_row, oc)], sem_rows.at[s])
                copy.start(priority=q % 2)
            return c
        lax.fori_loop(0, tile // ISSUE_GROUP, issue, 0)

    @pl.when(i == 0)
    def _():
        idx_copy(0, 0).start()
        idx_copy(0, 0).wait()
        gather(0)
        if n_tiles > 1:
            idx_copy(1, 1).start()

    for k in range(TOP_K):
        pltpu.make_async_copy(ys_hbm.at[pl.ds(0, tile * oc)], buf_ref.at[slot, k], sem_rows.at[slot]).wait()

    @pl.when(i + 1 < n_tiles)
    def _():
        idx_copy(i + 1, 1 - slot).wait()
        gather(1 - slot)

    @pl.when(i + 2 < n_tiles)
    def _():
        idx_copy(i + 2, slot).start()

    route = route_ref[...]
    chunks = []
    for c in range(oc):
        acc = jnp.zeros((tile, 128), F32)
        for k in range(TOP_K):
            acc = acc + buf_ref[slot, k, pl.ds(c, tile, stride=oc), :] * route[:, 2 * TOP_K + k:2 * TOP_K + k + 1]
        chunks.append(acc)
    x = x_ref[...] + g2_ref[0] * jnp.concatenate(chunks, axis=1)
    if final:
        x = (x * lax.rsqrt(jnp.mean(x * x, axis=-1, keepdims=True) + RMS_EPS)) * fg_ref[...]
    o_ref[...] = x


def _combine(pos_flat, ys, x2, route2, g2, final_g, tiles_per_batch, final):
    n, d = x2.shape
    tile = TOKEN_TILE
    return pl.pallas_call(
        functools.partial(_combine_kernel, tile=tile, n_tiles=n // tile, final=final),
        grid=(n // tile,),
        in_specs=[pl.BlockSpec(memory_space=pl.ANY),
                  pl.BlockSpec(memory_space=pl.ANY),
                  pl.BlockSpec((tile, d), lambda i: (i, 0)),
                  pl.BlockSpec((tile, 128), lambda i: (i, 0)),
                  pl.BlockSpec((1, 1, d), lambda i: (i // tiles_per_batch, 0, 0)),
                  pl.BlockSpec((1, d), lambda i: (0, 0))],
        out_specs=pl.BlockSpec((tile, d), lambda i: (i, 0)),
        out_shape=jax.ShapeDtypeStruct((n, d), F32),
        scratch_shapes=[pltpu.VMEM((2, TOP_K, tile * (d // 128), 128), F32),
                        pltpu.SMEM((2 * tile * TOP_K,), jnp.int32),
                        pltpu.SemaphoreType.DMA((2,)), pltpu.SemaphoreType.DMA((2,))],
        compiler_params=_params("arbitrary"),
        name="moe_combine",
    )(pos_flat, ys, x2, route2, g2, final_g)


def _moe_layer(x, g, sh, sc, gate2, w_r, b_r, w1_all, layer, b1, w2, b2, final_g, final):
    b, t, d = x.shape
    n = b * t
    ne = w_r.shape[1]
    bm = EXPERT_ROWS
    hp, route, counts = _route(x, g, sh, sc, w_r, b_r.reshape(1, ne))
    route2 = route.reshape(n, 128)

    counts = counts[0].astype(jnp.int32)
    padded = (counts + bm - 1) // bm * bm
    pad_end = jnp.cumsum(padded)
    pad_start = pad_end - padded
    n_blocks = -(-(n * TOP_K + ne * (bm - 1)) // bm)
    e_idx = route2[:, :TOP_K].astype(jnp.int32)
    experts = jnp.arange(ne, dtype=jnp.int32)
    start_of = jnp.sum(jnp.where(e_idx[:, :, None] == experts, pad_start, 0), axis=-1)
    pos_flat = (start_of + route2[:, TOP_K:2 * TOP_K].astype(jnp.int32)).reshape(-1)
    block_row = jnp.arange(n_blocks, dtype=jnp.int32) * bm
    block_e = jnp.minimum(jnp.sum((pad_end[None, :] <= block_row[:, None]).astype(jnp.int32), axis=-1), ne - 1)
    n_used = (pad_end[-1:] // bm).astype(jnp.int32)

    rc = d // 256
    xs = _dispatch(pos_flat * rc, hp.reshape(n * rc, 128), n_blocks * bm, rc)
    f = w2.shape[1]
    w1g, w1l = _split_w1(w1_all, layer)
    b1g = b1[:, 0::2].reshape(ne, 1, f)
    b1l = b1[:, 1::2].reshape(ne, 1, f)
    ys = _expert_ffn(block_e, n_used, xs, w1g, w1l, b1g, b1l, w2.astype(BF16), b2.reshape(ne, 1, d), rc)
    out = _combine(pos_flat * (d // 128), ys, x.reshape(n, d), route2, gate2, final_g.reshape(1, d),
                   t // TOKEN_TILE, final)
    return out.reshape(b, t, d)


def _even_layer(x, ctx, mod, mod_c, norm_g, w_in, mu_prev, mu_next, w0, w2, a0, a2, g2, key_k, key_a, r_k,
                lnx_g, lnx_b, rpb, w_out):
    b, t, d = x.shape
    l = ctx.shape[1]
    aw = key_k.shape[0]
    dl = w2.shape[1]
    bw = (w_in.shape[1] - 3 * aw - 128 - 4 * dl) // 3
    n_heads = aw // HEAD_DIM
    sh1, sc1, g1 = (mod[:, None, i * d:(i + 1) * d] for i in range(3))
    shc = jnp.broadcast_to(mod_c[None, None, :d], (b, 1, d))
    scc = jnp.broadcast_to(mod_c[None, None, d:2 * d], (b, 1, d))

    c_ra, c_gd = bw, bw + aw
    c_ka = c_gd + 128
    c_va = c_ka + aw
    c_wd = c_va + aw
    c_ad = c_wd + 2 * dl
    c_kb = c_ad + 2 * dl
    c_vb = c_kb + bw
    cols = lambda a, lo, hi: a[..., lo:hi]
    pad = jnp.zeros((d, 128), F32)
    w_p = jnp.concatenate([cols(w_in, c_ra, c_gd), cols(w_in, c_ka, c_va), cols(w_in, c_va, c_wd),
                           cols(w_in, c_gd, c_ka), cols(w_in, c_wd, c_ad), cols(w_in, c_ad, c_kb), pad,
                           cols(w_in, 0, c_ra), cols(w_in, c_kb, c_vb), cols(w_in, c_vb, c_vb + bw)],
                          axis=1).astype(BF16)

    def shift_vec(mu):
        o = lambda c: c - c_ra
        return jnp.concatenate([mu[o(c_ra):o(c_gd)], mu[o(c_ka):o(c_va)], mu[o(c_va):o(c_wd)], mu[o(c_gd):o(c_ka)],
                                mu[o(c_wd):o(c_ad)], mu[o(c_ad):o(c_kb)], jnp.zeros((128,), F32)]).reshape(1, -1)

    blockdiag = lambda m: jnp.concatenate(
        [jnp.concatenate([m[0], jnp.zeros_like(m[0])], axis=1),
         jnp.concatenate([jnp.zeros_like(m[1]), m[1]], axis=1)], axis=0)
    head = jnp.arange(aw) // HEAD_DIM
    consts = {
        "mu_prev": shift_vec(mu_prev), "mu_next": shift_vec(mu_next),
        "w0": w0.reshape(1, 2 * aw), "w2": blockdiag(w2), "a0": a0.reshape(1, 2 * aw), "a2": blockdiag(a2),
        "g2": g2.astype(BF16), "key_k": key_k.reshape(1, aw), "key_a": key_a.reshape(1, aw),
        "r_k": r_k.reshape(1, aw), "seg": (head[:, None] == head[None, :]).astype(BF16),
    }

    g_row = norm_g.reshape(1, d)
    p = _in_proj(x, g_row, sh1, sc1, w_p)
    pc = _in_proj(ctx, g_row, shc, scc, w_p)
    z_m, bonus, gate = _rwkv_terms(p, consts)
    z_c, _, _ = _rwkv_terms(pc, consts)

    yf, yb = _wkv_scan(_to_scan_layout(z_c), _to_scan_layout(z_m))
    y = _from_scan_layout(yf, yb, b)

    qb = (4 * aw) // 128
    o_b = _neighbourhood_attention(p, pc, _na_bias_table(rpb, t // GRID_W), qb, qb + bw // 128, qb + 2 * bw // 128)
    return _mix_out(y, bonus, gate, o_b, x, g1, lnx_g.reshape(1, aw), lnx_b.reshape(1, aw), consts["seg"],
                    w_out[:aw].astype(BF16), w_out[aw:].astype(BF16))


def _odd_layer(x, mod, norm_g, pw1_w, pw1_b, dw_w, dw_b, ln_g, ln_b, pw2_w, pw2_b):
    b, t, d = x.shape
    sh1, sc1, g1 = (mod[:, None, i * d:(i + 1) * d] for i in range(3))
    u = _glu_proj(x, norm_g.reshape(1, d), sh1, sc1, pw1_w.astype(BF16), pw1_b.reshape(1, -1))
    dw = jnp.concatenate([dw_w, jnp.zeros((1, d), F32)], axis=0)
    return _conv_module(u, dw, dw_b.reshape(1, d), ln_g.reshape(1, d), ln_b.reshape(1, d), pw2_w.astype(BF16),
                        pw2_b.reshape(1, d), x, g1)


def kernel(x, c, ctx, c_ctx, ada_w, ada_b, norm_mix_g, norm_ffn_g, final_norm_g, mix_w_in, shift_mu_prev, shift_mu_next, decay_w0, decay_w2, iclr_a0, iclr_a2, gate_g2, key_k, key_a, bonus_r_k, lnx_g, lnx_b, na_rpb, mix_w_out, conv_pw1_w, conv_pw1_b, conv_dw_w, conv_dw_b, conv_ln_g, conv_ln_b, conv_pw2_w, conv_pw2_b, router_w, router_b, expert_w1, expert_b1, expert_w2, expert_b2):
    b, t, d = x.shape
    depth = ada_w.shape[0]
    rows = -(-(b + 1) // 8) * 8
    c_all = jnp.concatenate([c, c_ctx[None, :], jnp.zeros((rows - b - 1, d), F32)], axis=0)
    mod_all = _modulation(c_all, ada_w, ada_b)
    for l in range(depth):
        mod = mod_all[l, :b]
        i = l // 2
        if l % 2 == 0:
            x = _even_layer(x, ctx, mod, mod_all[l, b], norm_mix_g[l], mix_w_in[i], shift_mu_prev[i], shift_mu_next[i],
                            decay_w0[i], decay_w2[i], iclr_a0[i], iclr_a2[i], gate_g2[i], key_k[i], key_a[i],
                            bonus_r_k[i].reshape(-1), lnx_g[i], lnx_b[i], na_rpb[i], mix_w_out[i])
        else:
            x = _odd_layer(x, mod, norm_mix_g[l], conv_pw1_w[i], conv_pw1_b[i], conv_dw_w[i], conv_dw_b[i],
                           conv_ln_g[i], conv_ln_b[i], conv_pw2_w[i], conv_pw2_b[i])
        sh2, sc2, g2 = (mod[:, None, j * d:(j + 1) * d] for j in range(3, 6))
        x = _moe_layer(x, norm_ffn_g[l].reshape(1, d), sh2, sc2, g2, router_w[l], router_b[l], expert_w1, l,
                       expert_b1[l], expert_w2[l], expert_b2[l], final_norm_g, final=(l == depth - 1))
    return x
```

```python
import functools

import jax
import jax.numpy as jnp
import numpy as np
from jax import lax
from jax.experimental import pallas as pl
from jax.experimental.pallas import tpu as pltpu

F32 = jnp.float32
BF16 = jnp.bfloat16
HIGHEST = lax.Precision.HIGHEST

HEAD_DIM = 64
GRID_W = 64
NA_ROWS = 8
NA_COLS = 16
CONV_WIDTH = 31
N_EXPERTS = 32
TOP_K = 4
SWIGLU_ALPHA = 1.702
SWIGLU_LIMIT = 7.0
RMS_EPS = 1e-6
LN_EPS = 1e-5
GN_EPS = 64e-5
NEG_BIG = -1e30

VMEM_LIMIT_BYTES = 52 * 1024 * 1024
TOKEN_TILE = 256
SCAN_BLOCK = 16
EXPERT_ROWS = 512
HALO = 16
ISSUE_GROUP = 4
NA_GROUP = 4


def _params(*sem):
    return pltpu.CompilerParams(dimension_semantics=sem, vmem_limit_bytes=VMEM_LIMIT_BYTES)


def _adaln(x, g, sh, sc):
    y = x * lax.rsqrt(jnp.mean(x * x, axis=-1, keepdims=True) + RMS_EPS)
    return (y * g) * (1.0 + sc) + sh


def _sigmoid(x):
    return 1.0 / (1.0 + jnp.exp(-x))


def _split_dot(x, m):
    hi = x.astype(BF16)
    r1 = x - hi.astype(F32)
    mid = r1.astype(BF16)
    lo = (r1 - mid.astype(F32)).astype(BF16)
    dot = functools.partial(jnp.dot, preferred_element_type=F32)
    return dot(hi, m) + dot(mid, m) + dot(lo, m)


def _dot3(x, w):
    xh = x.astype(BF16)
    xl = (x - xh.astype(F32)).astype(BF16)
    wh = w.astype(BF16)
    wl = (w - wh.astype(F32)).astype(BF16)
    dot = functools.partial(jnp.dot, preferred_element_type=F32)
    return dot(xh, wh) + (dot(xl, wh) + dot(xh, wl))


def _mod_kernel(c_ref, w_ref, b_ref, o_ref):
    c = c_ref[...]
    s = c * _sigmoid(c)
    o_ref[0] = jnp.dot(s, w_ref[0], preferred_element_type=F32, precision=HIGHEST) + b_ref[0]


def _modulation(c_all, ada_w, ada_b):
    depth, d, n = ada_w.shape
    rows = c_all.shape[0]
    tn = 1536
    return pl.pallas_call(
        _mod_kernel,
        grid=(depth, n // tn),
        in_specs=[pl.BlockSpec((rows, d), lambda l, j: (0, 0)),
                  pl.BlockSpec((1, d, tn), lambda l, j: (l, 0, j)),
                  pl.BlockSpec((1, 1, tn), lambda l, j: (l, 0, j))],
        out_specs=pl.BlockSpec((1, rows, tn), lambda l, j: (l, 0, j)),
        out_shape=jax.ShapeDtypeStruct((depth, rows, n), F32),
        compiler_params=_params("arbitrary", "arbitrary"),
        name="modulation",
    )(c_all, ada_w, ada_b.reshape(depth, 1, n))


def _proj_kernel(x_ref, g_ref, sh_ref, sc_ref, w_ref, o_ref):
    h = _adaln(x_ref[0], g_ref[...], sh_ref[0], sc_ref[0]).astype(BF16)
    o_ref[0] = jnp.dot(h, w_ref[...], preferred_element_type=F32)


def _in_proj(x, g, sh, sc, w):
    b, t, d = x.shape
    n = w.shape[1]
    tt = min(TOKEN_TILE, t)
    return pl.pallas_call(
        _proj_kernel,
        grid=(b, t // tt),
        in_specs=[pl.BlockSpec((1, tt, d), lambda i, j: (i, j, 0)),
                  pl.BlockSpec((1, d), lambda i, j: (0, 0)),
                  pl.BlockSpec((1, 1, d), lambda i, j: (i, 0, 0)),
                  pl.BlockSpec((1, 1, d), lambda i, j: (i, 0, 0)),
                  pl.BlockSpec((d, n), lambda i, j: (0, 0))],
        out_specs=pl.BlockSpec((1, tt, n), lambda i, j: (i, j, 0)),
        out_shape=jax.ShapeDtypeStruct((b, t, n), F32),
        compiler_params=_params("arbitrary", "arbitrary"),
        name="in_proj",
    )(x, g, sh, sc, w)


def _terms_kernel(p_ref, pp_ref, pn_ref, mup_ref, mun_ref, w0_ref, w2_ref, a0_ref, a2_ref, g2_ref,
                  kk_ref, ka_ref, rk_ref, seg_ref, z_ref, bonus_ref, gate_ref, *, n_tiles):
    t = pl.program_id(1)
    p = p_ref[0]
    tt, aw = p.shape[0], kk_ref.shape[1]
    prev_row = jnp.where(t > 0, pp_ref[0, 7:8, :], 0.0)
    next_row = jnp.where(t < n_tiles - 1, pn_ref[0, 0:1, :], 0.0)
    rows = lax.broadcasted_iota(jnp.int32, p.shape, 0)
    prev = jnp.where(rows == 0, prev_row, pltpu.roll(p, 1, axis=0))
    nxt = jnp.where(rows == tt - 1, next_row, pltpu.roll(p, tt - 1, axis=0))
    s = p + mup_ref[...] * (prev - p) + mun_ref[...] * (nxt - p)
    r, k, v = s[:, :aw], s[:, aw:2 * aw], s[:, 2 * aw:3 * aw]
    lora = s[:, 3 * aw:]
    g_in, wd, ad = lora[:, 0:128], lora[:, 128:256], lora[:, 256:384]
    dotf = _dot3
    zw = -(w0_ref[...] + dotf(jnp.tanh(wd), w2_ref[...]))
    softplus = jnp.maximum(zw, 0.0) + jnp.log(1.0 + jnp.exp(-jnp.abs(zw)))
    decay = jnp.exp(-jnp.exp(-softplus - 0.5))
    a = _sigmoid(a0_ref[...] + dotf(ad, a2_ref[...]))
    seg = seg_ref[...]
    kk = k * kk_ref[...]
    kk = kk / jnp.maximum(jnp.sqrt(_split_dot(kk * kk, seg)), 1e-12)
    z_ref[0, 0] = kk
    z_ref[1, 0] = v
    z_ref[2, 0] = r
    kd_sum = jnp.zeros_like(k)
    for d in range(2):
        a_d = a[:, d * aw:(d + 1) * aw]
        k_dir = k * (1.0 + (a_d - 1.0) * ka_ref[...])
        z_ref[3 + 3 * d, 0] = decay[:, d * aw:(d + 1) * aw]
        z_ref[4 + 3 * d, 0] = k_dir
        z_ref[5 + 3 * d, 0] = kk * a_d
        kd_sum = kd_sum + k_dir
    bonus_ref[0] = _split_dot(r * kd_sum * rk_ref[...], seg) * v
    gate_ref[0] = jnp.dot(_sigmoid(g_in).astype(BF16), g2_ref[...], preferred_element_type=F32)


def _rwkv_terms(p, consts):
    b, t, _ = p.shape
    aw = consts["key_k"].shape[1]
    sw = 4 * aw
    tt = min(TOKEN_TILE, t)
    n_tiles = t // tt
    hb = tt // 8
    full = lambda a: pl.BlockSpec(a.shape, lambda i, j: (0,) * a.ndim)
    names = ("mu_prev", "mu_next", "w0", "w2", "a0", "a2", "g2", "key_k", "key_a", "r_k", "seg")
    cs = [consts[n] for n in names]
    out3 = jax.ShapeDtypeStruct((b, t, aw), F32)
    return pl.pallas_call(
        functools.partial(_terms_kernel, n_tiles=n_tiles),
        grid=(b, n_tiles),
        in_specs=[pl.BlockSpec((1, tt, sw), lambda i, j: (i, j, 0)),
                  pl.BlockSpec((1, 8, sw), lambda i, j: (i, jnp.maximum(j * hb - 1, 0), 0)),
                  pl.BlockSpec((1, 8, sw), lambda i, j: (i, jnp.minimum((j + 1) * hb, t // 8 - 1), 0))]
                 + [full(a) for a in cs],
        out_specs=[pl.BlockSpec((9, 1, tt, aw), lambda i, j: (0, i, j, 0)),
                   pl.BlockSpec((1, tt, aw), lambda i, j: (i, j, 0)),
                   pl.BlockSpec((1, tt, aw), lambda i, j: (i, j, 0))],
        out_shape=[jax.ShapeDtypeStruct((9, b, t, aw), F32), out3, out3],
        compiler_params=_params("arbitrary", "arbitrary"),
        name="rwkv_terms",
    )(p, p, p, *cs)


def _scan_kernel(csf_ref, cdf_ref, csb_ref, cdb_ref, zsf_ref, zdf_ref, zsb_ref, zdb_ref, yf_ref, yb_ref, s_ref,
                 *, tb, nc):
    g = pl.program_id(0)

    @pl.when(g == 0)
    def _():
        s_ref[...] = jnp.zeros_like(s_ref)

    n = s_ref.shape[1]

    def run(dirs):
        def step(tf, carry):
            tidx = (tf, tb - 1 - tf)
            vecs = []
            for d, (zs, zd, _) in enumerate(dirs):
                ti = tidx[d]
                kk, r = zs[ti, 0], zs[ti, 2]
                w, kd, bb = zd[ti, 0], zd[ti, 1], zd[ti, 2]
                bbr = jnp.sum(bb * r, axis=0, keepdims=True)
                kr = jnp.sum(kd * r, axis=0, keepdims=True)
                vecs.append((kk, w * r, w, bb, kd, bbr, kr))

            def row(i, c):
                for d, (zs, _, y_ref) in enumerate(dirs):
                    kk, wr, w, bb, kd, bbr, kr = vecs[d]
                    ti = tidx[d]
                    si = s_ref[d, i]
                    sa = -jnp.sum(si * kk, axis=0, keepdims=True)
                    vi = zs[ti, 1, pl.ds(i, 1), :]
                    s_ref[d, i] = si * w + sa * bb + vi * kd
                    if y_ref is not None:
                        y0 = jnp.sum(si * wr, axis=0, keepdims=True)
                        y_ref[ti, pl.ds(i, 1), :] = y0 + sa * bbr + vi * kr
                return c

            lax.fori_loop(0, n, row, 0, unroll=8)
            return carry

        lax.fori_loop(0, tb, step, 0)

    @pl.when(g < nc)
    def _():
        run(((csf_ref, cdf_ref, None), (csb_ref, cdb_ref, None)))

    @pl.when(g >= nc)
    def _():
        run(((zsf_ref, zdf_ref, yf_ref), (zsb_ref, zdb_ref, yb_ref)))


def _wkv_scan(zc, zm):
    n, lanes = zm.shape[2:]
    tb = SCAN_BLOCK
    nc, nm = zc.shape[0] // tb, zm.shape[0] // tb
    cf = lambda g: jnp.minimum(g, nc - 1)
    cb = lambda g: jnp.maximum(nc - 1 - g, 0)
    mf = lambda g: jnp.maximum(g - nc, 0)
    mb = lambda g: jnp.minimum(nm - 1, nm - 1 + nc - g)
    blk = (tb, 3, n, lanes)
    spec = lambda t_of, part: pl.BlockSpec(blk, lambda g: (t_of(g), part, 0, 0))
    y_shape = jax.ShapeDtypeStruct((nm * tb, n, lanes), F32)
    return pl.pallas_call(
        functools.partial(_scan_kernel, tb=tb, nc=nc),
        grid=(nc + nm,),
        in_specs=[spec(cf, 0), spec(cf, 1), spec(cb, 0), spec(cb, 2),
                  spec(mf, 0), spec(mf, 1), spec(mb, 0), spec(mb, 2)],
        out_specs=[pl.BlockSpec((tb, n, lanes), lambda g: (mf(g), 0, 0)),
                   pl.BlockSpec((tb, n, lanes), lambda g: (mb(g), 0, 0))],
        out_shape=[y_shape, y_shape],
        scratch_shapes=[pltpu.VMEM((2, n, n, lanes), F32)],
        compiler_params=_params("arbitrary"),
        name="wkv_scan",
    )(zc, zc, zc, zc, zm, zm, zm, zm)


def _to_scan_kernel(x_ref, o_ref):
    n_comp, nb, tt, aw = x_ref.shape
    n_head = aw // HEAD_DIM
    low = lax.broadcasted_iota(jnp.int32, (nb, 128), 1) < HEAD_DIM

    def comp(c, carry):
        for tp in range(tt // 2):
            a = x_ref[c, :, 2 * tp, :]
            b = x_ref[c, :, 2 * tp + 1, :]
            pieces = []
            for h in range(n_head):
                ls = slice((h // 2) * 128, (h // 2 + 1) * 128)
                am, bm = a[:, ls], b[:, ls]
                if h % 2 == 0:
                    pieces.append(jnp.where(low, am, pltpu.roll(bm, HEAD_DIM, axis=1)))
                else:
                    pieces.append(jnp.where(low, pltpu.roll(am, HEAD_DIM, axis=1), bm))
            r2 = jnp.concatenate(pieces, axis=0).T
            o_ref[2 * tp, c] = r2[:HEAD_DIM]
            o_ref[2 * tp + 1, c] = r2[HEAD_DIM:]
        return carry

    lax.fori_loop(0, n_comp, comp, 0)


def _to_scan_layout(z):
    n_comp, b, t, aw = z.shape
    tt = SCAN_BLOCK
    lanes = (aw // HEAD_DIM) * b
    return pl.pallas_call(
        _to_scan_kernel,
        grid=(t // tt,),
        in_specs=[pl.BlockSpec((n_comp, b, tt, aw), lambda i: (0, 0, i, 0))],
        out_specs=pl.BlockSpec((tt, n_comp, HEAD_DIM, lanes), lambda i: (i, 0, 0, 0)),
        out_shape=jax.ShapeDtypeStruct((t, n_comp, HEAD_DIM, lanes), F32),
        compiler_params=_params("arbitrary"),
        name="to_scan_layout",
    )(z)


def _from_scan_tile(yf_ref, yb_ref, o_ref):
    tt = yf_ref.shape[0]
    nb, _, aw = o_ref.shape
    n_head = aw // HEAD_DIM
    low = lax.broadcasted_iota(jnp.int32, (nb, 128), 1) < HEAD_DIM
    for tp in range(tt // 2):
        s = jnp.concatenate([yf_ref[2 * tp] + yb_ref[2 * tp], yf_ref[2 * tp + 1] + yb_ref[2 * tp + 1]], axis=0)
        r2 = s.T
        for m in range(n_head // 2):
            pe = r2[(2 * m) * nb:(2 * m + 1) * nb]
            po = r2[(2 * m + 1) * nb:(2 * m + 2) * nb]
            ls = slice(m * 128, (m + 1) * 128)
            o_ref[:, 2 * tp, ls] = jnp.where(low, pe, pltpu.roll(po, HEAD_DIM, axis=1))
            o_ref[:, 2 * tp + 1, ls] = jnp.where(low, pltpu.roll(pe, HEAD_DIM, axis=1), po)


def _na_kernel(q_ref, k_ref, v_ref, kc_ref, vc_ref, bias_ref, o_ref, kb_ref, vb_ref, kcb_ref, vcb_ref, *, rows):
    kh = NA_ROWS
    dn = (((1,), (1,)), ((), ()))
    kb_ref[...] = k_ref[0].astype(BF16)
    vb_ref[...] = v_ref[0].astype(BF16)
    kcb_ref[...] = kc_ref[0].astype(BF16)
    vcb_ref[...] = vc_ref[0].astype(BF16)
    nq = NA_GROUP * GRID_W
    nk = (kh + NA_GROUP - 1) * GRID_W
    n_groups = rows // NA_GROUP
    head_of_lane = lax.broadcasted_iota(jnp.int32, (nq, 2 * HEAD_DIM), 1) // HEAD_DIM

    def group(g, c):
        u = _na_union_start(g, rows)
        pat = jnp.where(g > 0, 1, 0) + jnp.where(g == n_groups - 1, 1, 0)
        q0 = pl.multiple_of(g * nq, nq)
        k0 = pl.multiple_of(u * GRID_W, GRID_W)
        q2 = q_ref[0, pl.ds(q0, nq), :] * (HEAD_DIM ** -0.5)
        kl = kb_ref[pl.ds(k0, nk), :]
        vl = vb_ref[pl.ds(k0, nk), :]
        out = jnp.zeros((nq, 2 * HEAD_DIM), F32)
        for hh in range(2):
            q = jnp.where(head_of_lane == hh, q2, 0.0).astype(BF16)
            s_loc = lax.dot_general(q, kl, dn, preferred_element_type=F32) + bias_ref[pat, hh]
            s_ctx = lax.dot_general(q, kcb_ref[...], dn, preferred_element_type=F32)
            m = jnp.maximum(jnp.max(s_loc, axis=-1, keepdims=True), jnp.max(s_ctx, axis=-1, keepdims=True))
            e_loc = jnp.exp(s_loc - m)
            e_ctx = jnp.exp(s_ctx - m)
            den = jnp.sum(e_loc, axis=-1, keepdims=True) + jnp.sum(e_ctx, axis=-1, keepdims=True)
            o = (jnp.dot(e_loc.astype(BF16), vl, preferred_element_type=F32)
                 + jnp.dot(e_ctx.astype(BF16), vcb_ref[...], preferred_element_type=F32))
            out = jnp.where(head_of_lane == hh, o / den, out)
        o_ref[0, pl.ds(q0, nq), :] = out.astype(o_ref.dtype)
        return c

    lax.fori_loop(0, n_groups, group, 0)


def _na_union_start(g, rows):
    lo = NA_GROUP * g - NA_ROWS // 2
    hi = rows - (NA_ROWS + NA_GROUP - 1)
    if isinstance(g, int):
        return min(max(lo, 0), hi)
    return jnp.clip(lo, 0, hi)


def _neighbourhood_attention(p, pc, bias, col_q, col_k, col_v):
    b, t, _ = p.shape
    l = pc.shape[1]
    rows = t // GRID_W
    n_pairs = bias.shape[1] // 2
    return pl.pallas_call(
        functools.partial(_na_kernel, rows=rows),
        grid=(n_pairs, b),
        in_specs=[pl.BlockSpec((1, t, 128), lambda h, i: (i, 0, col_q + h)),
                  pl.BlockSpec((1, t, 128), lambda h, i: (i, 0, col_k + h)),
                  pl.BlockSpec((1, t, 128), lambda h, i: (i, 0, col_v + h)),
                  pl.BlockSpec((1, l, 128), lambda h, i: (i, 0, col_k + h)),
                  pl.BlockSpec((1, l, 128), lambda h, i: (i, 0, col_v + h)),
                  pl.BlockSpec((bias.shape[0], 2) + bias.shape[2:], lambda h, i: (0, h, 0, 0))],
        out_specs=pl.BlockSpec((1, t, 128), lambda h, i: (i, 0, h)),
        out_shape=jax.ShapeDtypeStruct((b, t, n_pairs * 128), BF16),
        scratch_shapes=[pltpu.VMEM((t, 128), BF16), pltpu.VMEM((t, 128), BF16),
                        pltpu.VMEM((l, 128), BF16), pltpu.VMEM((l, 128), BF16)],
        compiler_params=_params("arbitrary", "arbitrary"),
        name="na_attention",
    )(p, p, p, pc, pc, bias)


def _na_bias_table(rpb, rows):
    h = rpb.shape[0]
    n_groups = rows // NA_GROUP
    assert rows % NA_GROUP == 0 and rows >= NA_ROWS + 2 * NA_GROUP - 1
    col = np.arange(GRID_W)
    c_start = np.clip(col - NA_COLS // 2, 0, GRID_W - NA_COLS)
    col_ok = (col[None, :] >= c_start[:, None]) & (col[None, :] < c_start[:, None] + NA_COLS)
    dc = np.clip(col[None, :] - col[:, None], 1 - NA_COLS, NA_COLS - 1) + NA_COLS - 1
    pick = (dc.reshape(1, -1) == np.arange(2 * NA_COLS - 1)[:, None]).astype(np.float32)
    t = jnp.einsum("hrc,cx->hrx", rpb, pick, precision=HIGHEST)
    t = jnp.where(col_ok.reshape(-1), t, NEG_BIG).reshape(h, 2 * NA_ROWS - 1, GRID_W, GRID_W)
    masked = jnp.full((h, GRID_W, GRID_W), NEG_BIG, F32)
    n_union = NA_ROWS + NA_GROUP - 1
    pats = []
    for g in (0, 1, n_groups - 1):
        u = _na_union_start(g, rows)
        blocks = []
        for ri in range(NA_GROUP):
            r = g * NA_GROUP + ri
            r_start = min(max(r - NA_ROWS // 2, 0), rows - NA_ROWS)
            row_blocks = []
            for kr in range(n_union):
                key_row = u + kr
                inside = r_start <= key_row < r_start + NA_ROWS
                row_blocks.append(t[:, key_row - r + NA_ROWS - 1] if inside else masked)
            blocks.append(jnp.stack(row_blocks, axis=2))
        pats.append(jnp.stack(blocks, axis=1))
    tab = jnp.stack(pats, axis=0)
    return tab.reshape(3, h, NA_GROUP * GRID_W, n_union * GRID_W)


def _mix_out_kernel(yf_ref, yb_ref, bonus_ref, gate_ref, ob_ref, x_ref, g1_ref, lg_ref, lb_ref, seg_ref, wa_ref,
                    wb_ref, o_ref, y_ref):
    _from_scan_tile(yf_ref, yb_ref, y_ref)
    nb, tt, aw = y_ref.shape
    rows = nb * tt
    y = y_ref[...].reshape(rows, aw)
    seg = seg_ref[...]
    inv = 1.0 / HEAD_DIM
    mu = _split_dot(y, seg) * inv
    yc = y - mu
    var = _split_dot(yc * yc, seg) * inv
    yn = (yc * lax.rsqrt(var + GN_EPS)) * lg_ref[...] + lb_ref[...]
    o_a = ((yn + bonus_ref[...].reshape(rows, aw)) * gate_ref[...].reshape(rows, aw)).astype(BF16)
    o_b = ob_ref[...].reshape(rows, ob_ref.shape[2])
    out = (jnp.dot(o_a, wa_ref[...], preferred_element_type=F32)
           + jnp.dot(o_b, wb_ref[...], preferred_element_type=F32))
    o_ref[...] = x_ref[...] + g1_ref[...] * out.reshape(nb, tt, out.shape[1])


def _mix_out(yf, yb, bonus, gate, o_b, x, g1, lnx_g, lnx_b, seg, w_a, w_b):
    b, t, d = x.shape
    aw = bonus.shape[2]
    tt = SCAN_BLOCK
    tok = lambda w: pl.BlockSpec((b, tt, w), lambda i: (0, i, 0))
    full = lambda a: pl.BlockSpec(a.shape, lambda i: (0,) * a.ndim)
    scan = pl.BlockSpec((tt,) + yf.shape[1:], lambda i: (i, 0, 0))
    return pl.pallas_call(
        _mix_out_kernel,
        grid=(t // tt,),
        in_specs=[scan, scan, tok(aw), tok(aw), tok(o_b.shape[2]), tok(d), full(g1),
                  full(lnx_g), full(lnx_b), full(seg), full(w_a), full(w_b)],
        out_specs=tok(d),
        out_shape=jax.ShapeDtypeStruct((b, t, d), F32),
        scratch_shapes=[pltpu.VMEM((b, tt, aw), F32)],
        compiler_params=_params("arbitrary"),
        name="mix_out",
    )(yf, yb, bonus, gate, o_b, x, g1, lnx_g, lnx_b, seg, w_a, w_b)


def _glu_kernel(x_ref, g_ref, sh_ref, sc_ref, w_ref, b_ref, o_ref):
    h = _adaln(x_ref[0], g_ref[...], sh_ref[0], sc_ref[0]).astype(BF16)
    u = jnp.dot(h, w_ref[...], preferred_element_type=F32) + b_ref[...]
    d = u.shape[1] // 2
    o_ref[0] = u[:, :d] * _sigmoid(u[:, d:])


def _glu_proj(x, g, sh, sc, w, bias):
    b, t, d = x.shape
    n = w.shape[1]
    tt = min(TOKEN_TILE, t)
    return pl.pallas_call(
        _glu_kernel,
        grid=(b, t // tt),
        in_specs=[pl.BlockSpec((1, tt, d), lambda i, j: (i, j, 0)),
                  pl.BlockSpec((1, d), lambda i, j: (0, 0)),
                  pl.BlockSpec((1, 1, d), lambda i, j: (i, 0, 0)),
                  pl.BlockSpec((1, 1, d), lambda i, j: (i, 0, 0)),
                  pl.BlockSpec((d, n), lambda i, j: (0, 0)),
                  pl.BlockSpec((1, n), lambda i, j: (0, 0))],
        out_specs=pl.BlockSpec((1, tt, n // 2), lambda i, j: (i, j, 0)),
        out_shape=jax.ShapeDtypeStruct((b, t, n // 2), F32),
        compiler_params=_params("arbitrary", "arbitrary"),
        name="glu_proj",
    )(x, g, sh, sc, w, bias)


def _conv_kernel(u_ref, up_ref, un_ref, dw_ref, dwb_ref, lg_ref, lb_ref, w2_ref, b2_ref, x_ref, g1_ref, o_ref,
                 win_ref, acc_ref, *, n_tiles):
    t = pl.program_id(1)
    tt, d = u_ref.shape[1], u_ref.shape[2]
    half = CONV_WIDTH // 2
    win_ref[0:HALO, :] = jnp.where(t > 0, up_ref[0], 0.0)
    win_ref[HALO:HALO + tt, :] = u_ref[0]
    win_ref[HALO + tt:2 * HALO + tt, :] = jnp.where(t < n_tiles - 1, un_ref[0], 0.0)
    rc = 64
    first = HALO - half

    for base in range(0, tt, rc):
        for lc in range(d // 128):
            ls = slice(lc * 128, (lc + 1) * 128)
            out = None
            for s in range(8):
                acc = None
                for a in range((first + CONV_WIDTH - 1 - s) // 8 + 1):
                    k = 8 * a + s - first
                    if k < 0:
                        continue
                    term = dw_ref[k:k + 1, ls] * win_ref[base + 8 * a:base + 8 * a + rc + 8, ls]
                    acc = term if acc is None else acc + term
                part = acc[s:s + rc]
                out = part if out is None else out + part
            acc_ref[base:base + rc, ls] = out
    u = acc_ref[...] + dwb_ref[...]
    mu = jnp.mean(u, axis=-1, keepdims=True)
    uc = u - mu
    var = jnp.mean(uc * uc, axis=-1, keepdims=True)
    un = (uc * lax.rsqrt(var + LN_EPS)) * lg_ref[...] + lb_ref[...]
    act = (un * _sigmoid(un)).astype(BF16)
    out = jnp.dot(act, w2_ref[...], preferred_element_type=F32) + b2_ref[...]
    o_ref[0] = x_ref[0] + g1_ref[0] * out


def _conv_module(u, dw, dwb, ln_g, ln_b, w2, b2, x, g1):
    b, t, d = x.shape
    tt = min(TOKEN_TILE, t)
    n_tiles = t // tt
    hb = tt // HALO
    tok = pl.BlockSpec((1, tt, d), lambda i, j: (i, j, 0))
    full = lambda a: pl.BlockSpec(a.shape, lambda i, j: (0,) * a.ndim)
    return pl.pallas_call(
        functools.partial(_conv_kernel, n_tiles=n_tiles),
        grid=(b, n_tiles),
        in_specs=[tok,
                  pl.BlockSpec((1, HALO, d), lambda i, j: (i, jnp.maximum(j * hb - 1, 0), 0)),
                  pl.BlockSpec((1, HALO, d), lambda i, j: (i, jnp.minimum((j + 1) * hb, t // HALO - 1), 0)),
                  full(dw), full(dwb), full(ln_g), full(ln_b), full(w2), full(b2), tok,
                  pl.BlockSpec((1, 1, d), lambda i, j: (i, 0, 0))],
        out_specs=tok,
        out_shape=jax.ShapeDtypeStruct((b, t, d), F32),
        scratch_shapes=[pltpu.VMEM((tt + 2 * HALO, d), F32), pltpu.VMEM((tt, d), F32)],
        compiler_params=_params("arbitrary", "arbitrary"),
        name="conv_module",
    )(u, u, u, dw, dwb, ln_g, ln_b, w2, b2, x, g1)


def _route_kernel(x_ref, g_ref, sh_ref, sc_ref, wr_ref, br_ref, hp_ref, route_ref, cnt_ref, carry_ref):
    @pl.when((pl.program_id(0) == 0) & (pl.program_id(1) == 0))
    def _():
        carry_ref[...] = jnp.zeros_like(carry_ref)

    h = _adaln(x_ref[0], g_ref[...], sh_ref[0], sc_ref[0])
    tt, d = h.shape
    hi = lax.bitcast_convert_type(h[:, :d // 2].astype(BF16).astype(F32), jnp.uint32)
    lo = lax.bitcast_convert_type(h[:, d // 2:].astype(BF16).astype(F32), jnp.uint32)
    packed = (hi & jnp.uint32(0xFFFF0000)) | (lo >> 16)
    n_ch = d // 2 // 128
    for c in range(n_ch):
        hp_ref[0, pl.ds(c, tt, stride=n_ch), :] = packed[:, c * 128:(c + 1) * 128]

    logits = _dot3(h, wr_ref[...]) + br_ref[...]
    ne = logits.shape[1]
    lane = lax.broadcasted_iota(jnp.int32, (tt, ne), 1).astype(F32)
    work = logits
    mask = jnp.zeros((tt, ne), F32)
    picks, es = [], []
    den = jnp.zeros((tt, 1), F32)
    for k in range(TOP_K):
        m = jnp.max(work, axis=-1, keepdims=True)
        idx = jnp.min(jnp.where(work == m, lane, float(ne)), axis=-1, keepdims=True)
        pick = lane == idx
        if k == 0:
            top = m
        e = jnp.exp(m - top)
        den = den + e
        picks.append((pick, idx))
        es.append(e)
        mask = jnp.where(pick, 1.0, mask)
        work = jnp.where(pick, -jnp.inf, work)

    ri = lax.broadcasted_iota(jnp.int32, (tt, tt), 0)
    ci = lax.broadcasted_iota(jnp.int32, (tt, tt), 1)
    lower = jnp.where(ci < ri, 1.0, 0.0).astype(BF16)
    rank = jnp.dot(lower, mask.astype(BF16), preferred_element_type=F32) + carry_ref[...]
    carry_ref[...] = carry_ref[...] + jnp.sum(mask, axis=0, keepdims=True)
    cnt_ref[...] = carry_ref[...]

    out_lane = lax.broadcasted_iota(jnp.int32, (tt, 128), 1)
    route = jnp.zeros((tt, 128), F32)
    for k in range(TOP_K):
        pick, idx = picks[k]
        rk = jnp.sum(jnp.where(pick, rank, 0.0), axis=-1, keepdims=True)
        route = jnp.where(out_lane == k, idx, route)
        route = jnp.where(out_lane == TOP_K + k, rk, route)
        route = jnp.where(out_lane == 2 * TOP_K + k, es[k] / den, route)
    route_ref[0] = route


def _route(x, g, sh, sc, w_r, b_r):
    b, t, d = x.shape
    ne = w_r.shape[1]
    tt = min(TOKEN_TILE, t)
    return pl.pallas_call(
        _route_kernel,
        grid=(b, t // tt),
        in_specs=[pl.BlockSpec((1, tt, d), lambda i, j: (i, j, 0)),
                  pl.BlockSpec((1, d), lambda i, j: (0, 0)),
                  pl.BlockSpec((1, 1, d), lambda i, j: (i, 0, 0)),
                  pl.BlockSpec((1, 1, d), lambda i, j: (i, 0, 0)),
                  pl.BlockSpec((d, ne), lambda i, j: (0, 0)),
                  pl.BlockSpec((1, ne), lambda i, j: (0, 0))],
        out_specs=[pl.BlockSpec((1, tt * (d // 256), 128), lambda i, j: (i, j, 0)),
                   pl.BlockSpec((1, tt, 128), lambda i, j: (i, j, 0)),
                   pl.BlockSpec((1, ne), lambda i, j: (0, 0))],
        out_shape=[jax.ShapeDtypeStruct((b, t * (d // 256), 128), jnp.uint32),
                   jax.ShapeDtypeStruct((b, t, 128), F32),
                   jax.ShapeDtypeStruct((1, ne), F32)],
        scratch_shapes=[pltpu.VMEM((1, ne), F32)],
        compiler_params=_params("arbitrary", "arbitrary"),
        name="moe_route",
    )(x, g, sh, sc, w_r, b_r)


def _dispatch_kernel(pos_hbm, hp_ref, xs_in, xs_out, idx_ref, sem_idx, sem_rows, *, tile, n_tiles, rc):
    del xs_in
    n_idx = tile * TOP_K

    def idx_copy(i, slot):
        return pltpu.make_async_copy(pos_hbm.at[i], idx_ref.at[pl.ds(slot * n_idx, n_idx)], sem_idx.at[slot])

    i = pl.program_id(0)
    slot = i % 2

    @pl.when(i == 0)
    def _():
        idx_copy(0, 0).start()

    idx_copy(i, slot).wait()

    @pl.when(i + 1 < n_tiles)
    def _():
        idx_copy(i + 1, 1 - slot).start()

    ibase = slot * n_idx

    def issue(jg, c2):
        j0 = jg * ISSUE_GROUP
        rows = [idx_ref[ibase + j0 * TOP_K + q] for q in range(ISSUE_GROUP * TOP_K)]
        for q, row in enumerate(rows):
            src = hp_ref.at[pl.ds(pl.multiple_of((j0 + q // TOP_K) * rc, rc), rc)]
            copy = pltpu.make_async_copy(src, xs_out.at[pl.ds(pl.multiple_of(row, rc), rc)], sem_rows)
            copy.start(priority=q % 2)
        return c2

    lax.fori_loop(0, tile // ISSUE_GROUP, issue, 0)
    for _ in range(TOP_K):
        pltpu.make_async_copy(hp_ref, xs_out.at[pl.ds(0, tile * rc)], sem_rows).wait()


def _dispatch(pos_flat, hp, n_rows, rc):
    n, w = hp.shape[0] // rc, hp.shape[1]
    tile = TOKEN_TILE
    xs0 = jnp.zeros((n_rows * rc, w), jnp.uint32)
    any_spec = pl.BlockSpec(memory_space=pl.ANY)
    return pl.pallas_call(
        functools.partial(_dispatch_kernel, tile=tile, n_tiles=n // tile, rc=rc),
        grid=(n // tile,),
        in_specs=[any_spec, pl.BlockSpec((tile * rc, w), lambda i: (i, 0)), any_spec],
        out_specs=any_spec,
        out_shape=jax.ShapeDtypeStruct((n_rows * rc, w), jnp.uint32),
        scratch_shapes=[pltpu.SMEM((2 * tile * TOP_K,), jnp.int32),
                        pltpu.SemaphoreType.DMA((2,)), pltpu.SemaphoreType.DMA],
        input_output_aliases={2: 0},
        compiler_params=_params("arbitrary"),
        name="moe_dispatch",
    )(pos_flat.reshape(n // tile, tile * TOP_K), hp, xs0)


def _expert_weights_kernel(w1_ref, p_ref, w2_ref, g_ref, l_ref, w2b_ref):
    w = w1_ref[0, 0].astype(BF16)
    n = w.shape[1]
    for c in range(n // 256):
        res = jnp.dot(w[:, c * 256:(c + 1) * 256], p_ref[...], preferred_element_type=F32)
        g_ref[0, :, c * 128:(c + 1) * 128] = res[:, :128].astype(BF16)
        l_ref[0, :, c * 128:(c + 1) * 128] = res[:, 128:].astype(BF16)
    w2b_ref[0] = w2_ref[0, 0].astype(BF16)


def _expert_weights(w1, w2, layer):
    _, ne, d, f2 = w1.shape
    f = w2.shape[2]
    n_steps = 2
    r = np.arange(256)[:, None]
    c = np.arange(256)[None, :]
    sel = jnp.asarray(np.where(c < 128, r == 2 * c, r == 2 * (c - 128) + 1), BF16)
    out1 = jax.ShapeDtypeStruct((ne, d, f2 // 2), BF16)
    return pl.pallas_call(
        _expert_weights_kernel,
        grid=(ne, n_steps),
        in_specs=[pl.BlockSpec((1, 1, d // n_steps, f2), lambda e, i: (layer, e, i, 0)),
                  pl.BlockSpec((256, 256), lambda e, i: (0, 0)),
                  pl.BlockSpec((1, 1, f // n_steps, d), lambda e, i: (layer, e, i, 0))],
        out_specs=[pl.BlockSpec((1, d // n_steps, f2 // 2), lambda e, i: (e, i, 0))] * 2
                  + [pl.BlockSpec((1, f // n_steps, d), lambda e, i: (e, i, 0))],
        out_shape=[out1, out1, jax.ShapeDtypeStruct((ne, f, d), BF16)],
        compiler_params=_params("arbitrary", "arbitrary"),
        name="expert_weights",
    )(w1, sel, w2)


def _expert_kernel(be_ref, nb_ref, xs_ref, w1g_ref, w1l_ref, b1g_ref, b1l_ref, w2_ref, b2_ref, ys_ref, *, bm):
    del be_ref
    rc = xs_ref.shape[0] // bm
    oc = ys_ref.shape[0] // bm

    @pl.when(pl.program_id(0) < nb_ref[0])
    def _():
        u = jnp.concatenate([xs_ref[pl.ds(c, bm, stride=rc), :] for c in range(rc)], axis=1)
        half = u.shape[1]
        xa = lax.bitcast_convert_type(u & jnp.uint32(0xFFFF0000), F32).astype(BF16)
        xb = lax.bitcast_convert_type(u << 16, F32).astype(BF16)
        dot = functools.partial(jnp.dot, preferred_element_type=F32)
        ug = dot(xa, w1g_ref[0, :half, :]) + dot(xb, w1g_ref[0, half:, :]) + b1g_ref[0]
        ul = dot(xa, w1l_ref[0, :half, :]) + dot(xb, w1l_ref[0, half:, :]) + b1l_ref[0]
        glu = jnp.minimum(ug, SWIGLU_LIMIT)
        lin = jnp.clip(ul, -SWIGLU_LIMIT, SWIGLU_LIMIT)
        act = (glu * _sigmoid(SWIGLU_ALPHA * glu)) * (lin + 1.0)
        y = dot(act.astype(BF16), w2_ref[0]) + b2_ref[0]
        for c in range(oc):
            ys_ref[pl.ds(c, bm, stride=oc), :] = y[:, c * 128:(c + 1) * 128]

    @pl.when(pl.program_id(0) >= nb_ref[0])
    def _():
        ys_ref[...] = jnp.zeros_like(ys_ref)


def _expert_ffn(block_e, n_used, xs, w1g, w1l, b1g, b1l, w2, b2, rc):
    n_rows = xs.shape[0] // rc
    ne, d, f = w1g.shape
    oc = d // 128
    bm = EXPERT_ROWS
    n_blocks = n_rows // bm
    wspec = lambda s: pl.BlockSpec((1,) + s, lambda i, be, nb: (be[i], 0, 0))
    return pl.pallas_call(
        functools.partial(_expert_kernel, bm=bm),
        grid_spec=pltpu.PrefetchScalarGridSpec(
            num_scalar_prefetch=2,
            grid=(n_blocks,),
            in_specs=[pl.BlockSpec((bm * rc, 128), lambda i, be, nb: (i, 0)),
                      wspec((d, f)), wspec((d, f)), wspec((1, f)), wspec((1, f)), wspec((f, d)), wspec((1, d))],
            out_specs=pl.BlockSpec((bm * oc, 128), lambda i, be, nb: (i, 0)),
        ),
        out_shape=jax.ShapeDtypeStruct((n_rows * oc, 128), F32),
        compiler_params=_params("arbitrary"),
        name="moe_experts",
    )(block_e, n_used, xs, w1g, w1l, b1g, b1l, w2, b2)


def _combine_kernel(pos_hbm, ys_hbm, x_ref, route_ref, g2_ref, fg_ref, o_ref, buf_ref, idx_ref, sem_idx, sem_rows,
                    *, tile, n_tiles, final):
    i = pl.program_id(0)
    n_idx = tile * TOP_K
    slot = i % 2
    oc = x_ref.shape[1] // 128

    def idx_copy(t, s):
        src = pos_hbm.at[pl.ds(pl.multiple_of(t * n_idx, n_idx), n_idx)]
        return pltpu.make_async_copy(src, idx_ref.at[pl.ds(s * n_idx, n_idx)], sem_idx.at[s])

    def gather(s):
        ibase = s * n_idx

        def issue(jg, c):
            j0 = jg * ISSUE_GROUP
            rows = [idx_ref[ibase + j0 * TOP_K + q] for q in range(ISSUE_GROUP * TOP_K)]
            for q, row in enumerate(rows):
                dst_row = pl.multiple_of((j0 + q // TOP_K) * oc, oc)
                copy = pltpu.make_async_copy(ys_hbm.at[pl.ds(pl.multiple_of(row, oc), oc)],
                                             buf_ref.at[s, q % TOP_K, pl.ds(dst_row, oc)], sem_rows.at[s])
                copy.start(priority=q % 2)
            return c
        lax.fori_loop(0, tile // ISSUE_GROUP, issue, 0)

    @pl.when(i == 0)
    def _():
        idx_copy(0, 0).start()
        idx_copy(0, 0).wait()
        gather(0)
        if n_tiles > 1:
            idx_copy(1, 1).start()

    for k in range(TOP_K):
        pltpu.make_async_copy(ys_hbm.at[pl.ds(0, tile * oc)], buf_ref.at[slot, k], sem_rows.at[slot]).wait()

    @pl.when(i + 1 < n_tiles)
    def _():
        idx_copy(i + 1, 1 - slot).wait()
        gather(1 - slot)

    @pl.when(i + 2 < n_tiles)
    def _():
        idx_copy(i + 2, slot).start()

    route = route_ref[...]
    chunks = []
    for c in range(oc):
        acc = jnp.zeros((tile, 128), F32)
        for k in range(TOP_K):
            acc = acc + buf_ref[slot, k, pl.ds(c, tile, stride=oc), :] * route[:, 2 * TOP_K + k:2 * TOP_K + k + 1]
        chunks.append(acc)
    x = x_ref[...] + g2_ref[0] * jnp.concatenate(chunks, axis=1)
    if final:
        x = (x * lax.rsqrt(jnp.mean(x * x, axis=-1, keepdims=True) + RMS_EPS)) * fg_ref[...]
    o_ref[...] = x


def _combine(pos_flat, ys, x2, route2, g2, final_g, tiles_per_batch, final):
    n, d = x2.shape
    tile = TOKEN_TILE
    return pl.pallas_call(
        functools.partial(_combine_kernel, tile=tile, n_tiles=n // tile, final=final),
        grid=(n // tile,),
        in_specs=[pl.BlockSpec(memory_space=pl.ANY),
                  pl.BlockSpec(memory_space=pl.ANY),
                  pl.BlockSpec((tile, d), lambda i: (i, 0)),
                  pl.BlockSpec((tile, 128), lambda i: (i, 0)),
                  pl.BlockSpec((1, 1, d), lambda i: (i // tiles_per_batch, 0, 0)),
                  pl.BlockSpec((1, d), lambda i: (0, 0))],
        out_specs=pl.BlockSpec((tile, d), lambda i: (i, 0)),
        out_shape=jax.ShapeDtypeStruct((n, d), F32),
        scratch_shapes=[pltpu.VMEM((2, TOP_K, tile * (d // 128), 128), F32),
                        pltpu.SMEM((2 * tile * TOP_K,), jnp.int32),
                        pltpu.SemaphoreType.DMA((2,)), pltpu.SemaphoreType.DMA((2,))],
        compiler_params=_params("arbitrary"),
        name="moe_combine",
    )(pos_flat, ys, x2, route2, g2, final_g)


def _moe_layer(x, g, sh, sc, gate2, w_r, b_r, w1_all, w2_all, layer, b1, b2, final_g, final):
    b, t, d = x.shape
    n = b * t
    ne = w_r.shape[1]
    bm = EXPERT_ROWS
    hp, route, counts = _route(x, g, sh, sc, w_r, b_r.reshape(1, ne))
    route2 = route.reshape(n, 128)

    counts = counts[0].astype(jnp.int32)
    padded = (counts + bm - 1) // bm * bm
    pad_end = jnp.cumsum(padded)
    pad_start = pad_end - padded
    n_blocks = -(-(n * TOP_K + ne * (bm - 1)) // bm)
    e_idx = route2[:, :TOP_K].astype(jnp.int32)
    experts = jnp.arange(ne, dtype=jnp.int32)
    start_of = jnp.sum(jnp.where(e_idx[:, :, None] == experts, pad_start, 0), axis=-1)
    pos_flat = (start_of + route2[:, TOP_K:2 * TOP_K].astype(jnp.int32)).reshape(-1)
    block_row = jnp.arange(n_blocks, dtype=jnp.int32) * bm
    block_e = jnp.minimum(jnp.sum((pad_end[None, :] <= block_row[:, None]).astype(jnp.int32), axis=-1), ne - 1)
    n_used = (pad_end[-1:] // bm).astype(jnp.int32)

    rc = d // 256
    xs = _dispatch(pos_flat * rc, hp.reshape(n * rc, 128), n_blocks * bm, rc)
    f = w2_all.shape[2]
    w1g, w1l, w2b = _expert_weights(w1_all, w2_all, layer)
    b1g = b1[:, 0::2].reshape(ne, 1, f)
    b1l = b1[:, 1::2].reshape(ne, 1, f)
    ys = _expert_ffn(block_e, n_used, xs, w1g, w1l, b1g, b1l, w2b, b2.reshape(ne, 1, d), rc)
    out = _combine(pos_flat * (d // 128), ys, x.reshape(n, d), route2, gate2, final_g.reshape(1, d),
                   t // TOKEN_TILE, final)
    return out.reshape(b, t, d)


def _even_layer(x, ctx, mod, mod_c, norm_g, w_in, mu_prev, mu_next, w0, w2, a0, a2, g2, key_k, key_a, r_k,
                lnx_g, lnx_b, rpb, w_out):
    b, t, d = x.shape
    l = ctx.shape[1]
    aw = key_k.shape[0]
    dl = w2.shape[1]
    bw = (w_in.shape[1] - 3 * aw - 128 - 4 * dl) // 3
    n_heads = aw // HEAD_DIM
    sh1, sc1, g1 = (mod[:, None, i * d:(i + 1) * d] for i in range(3))
    shc = jnp.broadcast_to(mod_c[None, None, :d], (b, 1, d))
    scc = jnp.broadcast_to(mod_c[None, None, d:2 * d], (b, 1, d))

    c_ra, c_gd = bw, bw + aw
    c_ka = c_gd + 128
    c_va = c_ka + aw
    c_wd = c_va + aw
    c_ad = c_wd + 2 * dl
    c_kb = c_ad + 2 * dl
    c_vb = c_kb + bw
    cols = lambda a, lo, hi: a[..., lo:hi]
    pad = jnp.zeros((d, 128), F32)
    w_p = jnp.concatenate([cols(w_in, c_ra, c_gd), cols(w_in, c_ka, c_va), cols(w_in, c_va, c_wd),
                           cols(w_in, c_gd, c_ka), cols(w_in, c_wd, c_ad), cols(w_in, c_ad, c_kb), pad,
                           cols(w_in, 0, c_ra), cols(w_in, c_kb, c_vb), cols(w_in, c_vb, c_vb + bw)],
                          axis=1).astype(BF16)

    def shift_vec(mu):
        o = lambda c: c - c_ra
        return jnp.concatenate([mu[o(c_ra):o(c_gd)], mu[o(c_ka):o(c_va)], mu[o(c_va):o(c_wd)], mu[o(c_gd):o(c_ka)],
                                mu[o(c_wd):o(c_ad)], mu[o(c_ad):o(c_kb)], jnp.zeros((128,), F32)]).reshape(1, -1)

    blockdiag = lambda m: jnp.concatenate(
        [jnp.concatenate([m[0], jnp.zeros_like(m[0])], axis=1),
         jnp.concatenate([jnp.zeros_like(m[1]), m[1]], axis=1)], axis=0)
    head = jnp.arange(aw) // HEAD_DIM
    consts = {
        "mu_prev": shift_vec(mu_prev), "mu_next": shift_vec(mu_next),
        "w0": w0.reshape(1, 2 * aw), "w2": blockdiag(w2), "a0": a0.reshape(1, 2 * aw), "a2": blockdiag(a2),
        "g2": g2.astype(BF16), "key_k": key_k.reshape(1, aw), "key_a": key_a.reshape(1, aw),
        "r_k": r_k.reshape(1, aw), "seg": (head[:, None] == head[None, :]).astype(BF16),
    }

    g_row = norm_g.reshape(1, d)
    p = _in_proj(x, g_row, sh1, sc1, w_p)
    pc = _in_proj(ctx, g_row, shc, scc, w_p)
    z_m, bonus, gate = _rwkv_terms(p, consts)
    z_c, _, _ = _rwkv_terms(pc, consts)

    yf, yb = _wkv_scan(_to_scan_layout(z_c), _to_scan_layout(z_m))

    qb = (4 * aw) // 128
    o_b = _neighbourhood_attention(p, pc, _na_bias_table(rpb, t // GRID_W), qb, qb + bw // 128, qb + 2 * bw // 128)
    return _mix_out(yf, yb, bonus, gate, o_b, x, g1, lnx_g.reshape(1, aw), lnx_b.reshape(1, aw), consts["seg"],
                    w_out[:aw].astype(BF16), w_out[aw:].astype(BF16))


def _odd_layer(x, mod, norm_g, pw1_w, pw1_b, dw_w, dw_b, ln_g, ln_b, pw2_w, pw2_b):
    b, t, d = x.shape
    sh1, sc1, g1 = (mod[:, None, i * d:(i + 1) * d] for i in range(3))
    u = _glu_proj(x, norm_g.reshape(1, d), sh1, sc1, pw1_w.astype(BF16), pw1_b.reshape(1, -1))
    dw = jnp.concatenate([dw_w, jnp.zeros((1, d), F32)], axis=0)
    return _conv_module(u, dw, dw_b.reshape(1, d), ln_g.reshape(1, d), ln_b.reshape(1, d), pw2_w.astype(BF16),
                        pw2_b.reshape(1, d), x, g1)


def kernel(x, c, ctx, c_ctx, ada_w, ada_b, norm_mix_g, norm_ffn_g, final_norm_g, mix_w_in, shift_mu_prev, shift_mu_next, decay_w0, decay_w2, iclr_a0, iclr_a2, gate_g2, key_k, key_a, bonus_r_k, lnx_g, lnx_b, na_rpb, mix_w_out, conv_pw1_w, conv_pw1_b, conv_dw_w, conv_dw_b, conv_ln_g, conv_ln_b, conv_pw2_w, conv_pw2_b, router_w, router_b, expert_w1, expert_b1, expert_w2, expert_b2):
    b, t, d = x.shape
    depth = ada_w.shape[0]
    rows = -(-(b + 1) // 8) * 8
    c_all = jnp.concatenate([c, c_ctx[None, :], jnp.zeros((rows - b - 1, d), F32)], axis=0)
    mod_all = _modulation(c_all, ada_w, ada_b)
    for l in range(depth):
        mod = mod_all[l, :b]
        i = l // 2
        if l % 2 == 0:
            x = _even_layer(x, ctx, mod, mod_all[l, b], norm_mix_g[l], mix_w_in[i], shift_mu_prev[i], shift_mu_next[i],
                            decay_w0[i], decay_w2[i], iclr_a0[i], iclr_a2[i], gate_g2[i], key_k[i], key_a[i],
                            bonus_r_k[i].reshape(-1), lnx_g[i], lnx_b[i], na_rpb[i], mix_w_out[i])
        else:
            x = _odd_layer(x, mod, norm_mix_g[l], conv_pw1_w[i], conv_pw1_b[i], conv_dw_w[i], conv_dw_b[i],
                           conv_ln_g[i], conv_ln_b[i], conv_pw2_w[i], conv_pw2_b[i])
        sh2, sc2, g2 = (mod[:, None, j * d:(j + 1) * d] for j in range(3, 6))
        x = _moe_layer(x, norm_ffn_g[l].reshape(1, d), sh2, sc2, g2, router_w[l], router_b[l], expert_w1, expert_w2,
                       l, expert_b1[l], expert_b2[l], final_norm_g, final=(l == depth - 1))
    return x
```

```python
import functools

import jax
import jax.numpy as jnp
import numpy as np
from jax import lax
from jax.experimental import pallas as pl
from jax.experimental.pallas import tpu as pltpu

F32 = jnp.float32
BF16 = jnp.bfloat16
HIGHEST = lax.Precision.HIGHEST

HEAD_DIM = 64
GRID_W = 64
NA_ROWS = 8
NA_COLS = 16
CONV_WIDTH = 31
N_EXPERTS = 32
TOP_K = 4
SWIGLU_ALPHA = 1.702
SWIGLU_LIMIT = 7.0
RMS_EPS = 1e-6
LN_EPS = 1e-5
GN_EPS = 64e-5
NEG_BIG = -1e30

VMEM_LIMIT_BYTES = 52 * 1024 * 1024
TOKEN_TILE = 256
SCAN_BLOCK = 16
EXPERT_ROWS = 512
HALO = 16
COPY_TILE = 512
ISSUE_GROUP = 4
NA_GROUP = 4


def _params(*sem):
    return pltpu.CompilerParams(dimension_semantics=sem, vmem_limit_bytes=VMEM_LIMIT_BYTES)


def _adaln(x, g, sh, sc):
    y = x * lax.rsqrt(jnp.mean(x * x, axis=-1, keepdims=True) + RMS_EPS)
    return (y * g) * (1.0 + sc) + sh


def _sigmoid(x):
    return 1.0 / (1.0 + jnp.exp(-x))


def _split_dot(x, m):
    hi = x.astype(BF16)
    r1 = x - hi.astype(F32)
    mid = r1.astype(BF16)
    lo = (r1 - mid.astype(F32)).astype(BF16)
    dot = functools.partial(jnp.dot, preferred_element_type=F32)
    return dot(hi, m) + dot(mid, m) + dot(lo, m)


def _dot3(x, w):
    xh = x.astype(BF16)
    xl = (x - xh.astype(F32)).astype(BF16)
    wh = w.astype(BF16)
    wl = (w - wh.astype(F32)).astype(BF16)
    dot = functools.partial(jnp.dot, preferred_element_type=F32)
    return dot(xh, wh) + (dot(xl, wh) + dot(xh, wl))


def _mod_kernel(c_ref, w_ref, b_ref, o_ref):
    c = c_ref[...]
    s = c * _sigmoid(c)
    o_ref[0] = jnp.dot(s, w_ref[0], preferred_element_type=F32, precision=HIGHEST) + b_ref[0]


def _modulation(c_all, ada_w, ada_b):
    depth, d, n = ada_w.shape
    rows = c_all.shape[0]
    tn = 1536
    return pl.pallas_call(
        _mod_kernel,
        grid=(depth, n // tn),
        in_specs=[pl.BlockSpec((rows, d), lambda l, j: (0, 0)),
                  pl.BlockSpec((1, d, tn), lambda l, j: (l, 0, j)),
                  pl.BlockSpec((1, 1, tn), lambda l, j: (l, 0, j))],
        out_specs=pl.BlockSpec((1, rows, tn), lambda l, j: (l, 0, j)),
        out_shape=jax.ShapeDtypeStruct((depth, rows, n), F32),
        compiler_params=_params("arbitrary", "arbitrary"),
        name="modulation",
    )(c_all, ada_w, ada_b.reshape(depth, 1, n))


def _proj_kernel(x_ref, g_ref, sh_ref, sc_ref, w_ref, o_ref):
    h = _adaln(x_ref[0], g_ref[...], sh_ref[0], sc_ref[0]).astype(BF16)
    o_ref[0] = jnp.dot(h, w_ref[...], preferred_element_type=F32)


def _in_proj(x, g, sh, sc, w):
    b, t, d = x.shape
    n = w.shape[1]
    tt = min(TOKEN_TILE, t)
    return pl.pallas_call(
        _proj_kernel,
        grid=(b, t // tt),
        in_specs=[pl.BlockSpec((1, tt, d), lambda i, j: (i, j, 0)),
                  pl.BlockSpec((1, d), lambda i, j: (0, 0)),
                  pl.BlockSpec((1, 1, d), lambda i, j: (i, 0, 0)),
                  pl.BlockSpec((1, 1, d), lambda i, j: (i, 0, 0)),
                  pl.BlockSpec((d, n), lambda i, j: (0, 0))],
        out_specs=pl.BlockSpec((1, tt, n), lambda i, j: (i, j, 0)),
        out_shape=jax.ShapeDtypeStruct((b, t, n), F32),
        compiler_params=_params("arbitrary", "arbitrary"),
        name="in_proj",
    )(x, g, sh, sc, w)


def _terms_kernel(p_ref, pp_ref, pn_ref, mup_ref, mun_ref, w0_ref, w2_ref, a0_ref, a2_ref, g2_ref,
                  kk_ref, ka_ref, rk_ref, seg_ref, z_ref, bonus_ref, gate_ref, *, n_tiles):
    t = pl.program_id(1)
    p = p_ref[0]
    tt, aw = p.shape[0], kk_ref.shape[1]
    prev_row = jnp.where(t > 0, pp_ref[0, 7:8, :], 0.0)
    next_row = jnp.where(t < n_tiles - 1, pn_ref[0, 0:1, :], 0.0)
    rows = lax.broadcasted_iota(jnp.int32, p.shape, 0)
    prev = jnp.where(rows == 0, prev_row, pltpu.roll(p, 1, axis=0))
    nxt = jnp.where(rows == tt - 1, next_row, pltpu.roll(p, tt - 1, axis=0))
    s = p + mup_ref[...] * (prev - p) + mun_ref[...] * (nxt - p)
    r, k, v = s[:, :aw], s[:, aw:2 * aw], s[:, 2 * aw:3 * aw]
    lora = s[:, 3 * aw:]
    g_in, wd, ad = lora[:, 0:128], lora[:, 128:256], lora[:, 256:384]
    dotf = _dot3
    zw = -(w0_ref[...] + dotf(jnp.tanh(wd), w2_ref[...]))
    softplus = jnp.maximum(zw, 0.0) + jnp.log(1.0 + jnp.exp(-jnp.abs(zw)))
    decay = jnp.exp(-jnp.exp(-softplus - 0.5))
    a = _sigmoid(a0_ref[...] + dotf(ad, a2_ref[...]))
    seg = seg_ref[...]
    kk = k * kk_ref[...]
    kk = kk / jnp.maximum(jnp.sqrt(_split_dot(kk * kk, seg)), 1e-12)
    z_ref[0, 0] = kk
    z_ref[1, 0] = v
    z_ref[2, 0] = r
    kd_sum = jnp.zeros_like(k)
    for d in range(2):
        a_d = a[:, d * aw:(d + 1) * aw]
        k_dir = k * (1.0 + (a_d - 1.0) * ka_ref[...])
        z_ref[3 + 3 * d, 0] = decay[:, d * aw:(d + 1) * aw]
        z_ref[4 + 3 * d, 0] = k_dir
        z_ref[5 + 3 * d, 0] = kk * a_d
        kd_sum = kd_sum + k_dir
    bonus_ref[0] = _split_dot(r * kd_sum * rk_ref[...], seg) * v
    gate_ref[0] = jnp.dot(_sigmoid(g_in).astype(BF16), g2_ref[...], preferred_element_type=F32)


def _rwkv_terms(p, consts):
    b, t, _ = p.shape
    aw = consts["key_k"].shape[1]
    sw = 4 * aw
    tt = min(TOKEN_TILE, t)
    n_tiles = t // tt
    hb = tt // 8
    full = lambda a: pl.BlockSpec(a.shape, lambda i, j: (0,) * a.ndim)
    names = ("mu_prev", "mu_next", "w0", "w2", "a0", "a2", "g2", "key_k", "key_a", "r_k", "seg")
    cs = [consts[n] for n in names]
    out3 = jax.ShapeDtypeStruct((b, t, aw), F32)
    return pl.pallas_call(
        functools.partial(_terms_kernel, n_tiles=n_tiles),
        grid=(b, n_tiles),
        in_specs=[pl.BlockSpec((1, tt, sw), lambda i, j: (i, j, 0)),
                  pl.BlockSpec((1, 8, sw), lambda i, j: (i, jnp.maximum(j * hb - 1, 0), 0)),
                  pl.BlockSpec((1, 8, sw), lambda i, j: (i, jnp.minimum((j + 1) * hb, t // 8 - 1), 0))]
                 + [full(a) for a in cs],
        out_specs=[pl.BlockSpec((9, 1, tt, aw), lambda i, j: (0, i, j, 0)),
                   pl.BlockSpec((1, tt, aw), lambda i, j: (i, j, 0)),
                   pl.BlockSpec((1, tt, aw), lambda i, j: (i, j, 0))],
        out_shape=[jax.ShapeDtypeStruct((9, b, t, aw), F32), out3, out3],
        compiler_params=_params("arbitrary", "arbitrary"),
        name="rwkv_terms",
    )(p, p, p, *cs)


def _scan_kernel(csf_ref, cdf_ref, csb_ref, cdb_ref, zsf_ref, zdf_ref, zsb_ref, zdb_ref, yf_ref, yb_ref, s_ref,
                 *, tb, nc):
    g = pl.program_id(0)

    @pl.when(g == 0)
    def _():
        s_ref[...] = jnp.zeros_like(s_ref)

    n = s_ref.shape[1]

    def run(dirs):
        def step(tf, carry):
            tidx = (tf, tb - 1 - tf)
            vecs = []
            for d, (zs, zd, _) in enumerate(dirs):
                ti = tidx[d]
                kk, r = zs[ti, 0], zs[ti, 2]
                w, kd, bb = zd[ti, 0], zd[ti, 1], zd[ti, 2]
                bbr = jnp.sum(bb * r, axis=0, keepdims=True)
                kr = jnp.sum(kd * r, axis=0, keepdims=True)
                vecs.append((kk, w * r, w, bb, kd, bbr, kr))

            def row(i, c):
                for d, (zs, _, y_ref) in enumerate(dirs):
                    kk, wr, w, bb, kd, bbr, kr = vecs[d]
                    ti = tidx[d]
                    si = s_ref[d, i]
                    sa = -jnp.sum(si * kk, axis=0, keepdims=True)
                    vi = zs[ti, 1, pl.ds(i, 1), :]
                    s_ref[d, i] = si * w + sa * bb + vi * kd
                    if y_ref is not None:
                        y0 = jnp.sum(si * wr, axis=0, keepdims=True)
                        y_ref[ti, pl.ds(i, 1), :] = y0 + sa * bbr + vi * kr
                return c

            lax.fori_loop(0, n, row, 0, unroll=16)
            return carry

        lax.fori_loop(0, tb, step, 0)

    @pl.when(g < nc)
    def _():
        run(((csf_ref, cdf_ref, None), (csb_ref, cdb_ref, None)))

    @pl.when(g >= nc)
    def _():
        run(((zsf_ref, zdf_ref, yf_ref), (zsb_ref, zdb_ref, yb_ref)))


def _wkv_scan(zc, zm):
    n, lanes = zm.shape[2:]
    tb = SCAN_BLOCK
    nc, nm = zc.shape[0] // tb, zm.shape[0] // tb
    cf = lambda g: jnp.minimum(g, nc - 1)
    cb = lambda g: jnp.maximum(nc - 1 - g, 0)
    mf = lambda g: jnp.maximum(g - nc, 0)
    mb = lambda g: jnp.minimum(nm - 1, nm - 1 + nc - g)
    blk = (tb, 3, n, lanes)
    spec = lambda t_of, part: pl.BlockSpec(blk, lambda g: (t_of(g), part, 0, 0))
    y_shape = jax.ShapeDtypeStruct((nm * tb, n, lanes), F32)
    return pl.pallas_call(
        functools.partial(_scan_kernel, tb=tb, nc=nc),
        grid=(nc + nm,),
        in_specs=[spec(cf, 0), spec(cf, 1), spec(cb, 0), spec(cb, 2),
                  spec(mf, 0), spec(mf, 1), spec(mb, 0), spec(mb, 2)],
        out_specs=[pl.BlockSpec((tb, n, lanes), lambda g: (mf(g), 0, 0)),
                   pl.BlockSpec((tb, n, lanes), lambda g: (mb(g), 0, 0))],
        out_shape=[y_shape, y_shape],
        scratch_shapes=[pltpu.VMEM((2, n, n, lanes), F32)],
        compiler_params=_params("arbitrary"),
        name="wkv_scan",
    )(zc, zc, zc, zc, zm, zm, zm, zm)


def _to_scan_kernel(x_ref, o_ref):
    n_comp, nb, tt, aw = x_ref.shape
    n_head = aw // HEAD_DIM
    low = lax.broadcasted_iota(jnp.int32, (nb, 128), 1) < HEAD_DIM

    def comp(c, carry):
        for tp in range(tt // 2):
            a = x_ref[c, :, 2 * tp, :]
            b = x_ref[c, :, 2 * tp + 1, :]
            pieces = []
            for h in range(n_head):
                ls = slice((h // 2) * 128, (h // 2 + 1) * 128)
                am, bm = a[:, ls], b[:, ls]
                if h % 2 == 0:
                    pieces.append(jnp.where(low, am, pltpu.roll(bm, HEAD_DIM, axis=1)))
                else:
                    pieces.append(jnp.where(low, pltpu.roll(am, HEAD_DIM, axis=1), bm))
            r2 = jnp.concatenate(pieces, axis=0).T
            o_ref[2 * tp, c] = r2[:HEAD_DIM]
            o_ref[2 * tp + 1, c] = r2[HEAD_DIM:]
        return carry

    lax.fori_loop(0, n_comp, comp, 0)


def _to_scan_layout(z):
    n_comp, b, t, aw = z.shape
    tt = SCAN_BLOCK
    lanes = (aw // HEAD_DIM) * b
    return pl.pallas_call(
        _to_scan_kernel,
        grid=(t // tt,),
        in_specs=[pl.BlockSpec((n_comp, b, tt, aw), lambda i: (0, 0, i, 0))],
        out_specs=pl.BlockSpec((tt, n_comp, HEAD_DIM, lanes), lambda i: (i, 0, 0, 0)),
        out_shape=jax.ShapeDtypeStruct((t, n_comp, HEAD_DIM, lanes), F32),
        compiler_params=_params("arbitrary"),
        name="to_scan_layout",
    )(z)


def _from_scan_tile(yf_ref, yb_ref, o_ref):
    tt = yf_ref.shape[0]
    nb, _, aw = o_ref.shape
    n_head = aw // HEAD_DIM
    low = lax.broadcasted_iota(jnp.int32, (nb, 128), 1) < HEAD_DIM
    for tp in range(tt // 2):
        s = jnp.concatenate([yf_ref[2 * tp] + yb_ref[2 * tp], yf_ref[2 * tp + 1] + yb_ref[2 * tp + 1]], axis=0)
        r2 = s.T
        for m in range(n_head // 2):
            pe = r2[(2 * m) * nb:(2 * m + 1) * nb]
            po = r2[(2 * m + 1) * nb:(2 * m + 2) * nb]
            ls = slice(m * 128, (m + 1) * 128)
            o_ref[:, 2 * tp, ls] = jnp.where(low, pe, pltpu.roll(po, HEAD_DIM, axis=1))
            o_ref[:, 2 * tp + 1, ls] = jnp.where(low, pltpu.roll(pe, HEAD_DIM, axis=1), po)


def _na_kernel(q_ref, k_ref, v_ref, kc_ref, vc_ref, bias_ref, o_ref, kb_ref, vb_ref, kcb_ref, vcb_ref, *, rows):
    kh = NA_ROWS
    dn = (((1,), (1,)), ((), ()))
    kb_ref[...] = k_ref[0].astype(BF16)
    vb_ref[...] = v_ref[0].astype(BF16)
    kcb_ref[...] = kc_ref[0].astype(BF16)
    vcb_ref[...] = vc_ref[0].astype(BF16)
    nq = NA_GROUP * GRID_W
    nk = (kh + NA_GROUP - 1) * GRID_W
    n_groups = rows // NA_GROUP
    head_of_lane = lax.broadcasted_iota(jnp.int32, (nq, 2 * HEAD_DIM), 1) // HEAD_DIM

    def group(g, c):
        u = _na_union_start(g, rows)
        pat = jnp.where(g > 0, 1, 0) + jnp.where(g == n_groups - 1, 1, 0)
        q0 = pl.multiple_of(g * nq, nq)
        k0 = pl.multiple_of(u * GRID_W, GRID_W)
        q2 = q_ref[0, pl.ds(q0, nq), :] * (HEAD_DIM ** -0.5)
        kl = kb_ref[pl.ds(k0, nk), :]
        vl = vb_ref[pl.ds(k0, nk), :]
        out = jnp.zeros((nq, 2 * HEAD_DIM), F32)
        for hh in range(2):
            q = jnp.where(head_of_lane == hh, q2, 0.0).astype(BF16)
            s_loc = lax.dot_general(q, kl, dn, preferred_element_type=F32) + bias_ref[pat, hh]
            s_ctx = lax.dot_general(q, kcb_ref[...], dn, preferred_element_type=F32)
            m = jnp.maximum(jnp.max(s_loc, axis=-1, keepdims=True), jnp.max(s_ctx, axis=-1, keepdims=True))
            e_loc = jnp.exp(s_loc - m)
            e_ctx = jnp.exp(s_ctx - m)
            den = jnp.sum(e_loc, axis=-1, keepdims=True) + jnp.sum(e_ctx, axis=-1, keepdims=True)
            o = (jnp.dot(e_loc.astype(BF16), vl, preferred_element_type=F32)
                 + jnp.dot(e_ctx.astype(BF16), vcb_ref[...], preferred_element_type=F32))
            out = jnp.where(head_of_lane == hh, o / den, out)
        o_ref[0, pl.ds(q0, nq), :] = out.astype(o_ref.dtype)
        return c

    lax.fori_loop(0, n_groups, group, 0)


def _na_union_start(g, rows):
    lo = NA_GROUP * g - NA_ROWS // 2
    hi = rows - (NA_ROWS + NA_GROUP - 1)
    if isinstance(g, int):
        return min(max(lo, 0), hi)
    return jnp.clip(lo, 0, hi)


def _neighbourhood_attention(p, pc, bias, col_q, col_k, col_v):
    b, t, _ = p.shape
    l = pc.shape[1]
    rows = t // GRID_W
    n_pairs = bias.shape[1] // 2
    return pl.pallas_call(
        functools.partial(_na_kernel, rows=rows),
        grid=(n_pairs, b),
        in_specs=[pl.BlockSpec((1, t, 128), lambda h, i: (i, 0, col_q + h)),
                  pl.BlockSpec((1, t, 128), lambda h, i: (i, 0, col_k + h)),
                  pl.BlockSpec((1, t, 128), lambda h, i: (i, 0, col_v + h)),
                  pl.BlockSpec((1, l, 128), lambda h, i: (i, 0, col_k + h)),
                  pl.BlockSpec((1, l, 128), lambda h, i: (i, 0, col_v + h)),
                  pl.BlockSpec((bias.shape[0], 2) + bias.shape[2:], lambda h, i: (0, h, 0, 0))],
        out_specs=pl.BlockSpec((1, t, 128), lambda h, i: (i, 0, h)),
        out_shape=jax.ShapeDtypeStruct((b, t, n_pairs * 128), BF16),
        scratch_shapes=[pltpu.VMEM((t, 128), BF16), pltpu.VMEM((t, 128), BF16),
                        pltpu.VMEM((l, 128), BF16), pltpu.VMEM((l, 128), BF16)],
        compiler_params=_params("arbitrary", "arbitrary"),
        name="na_attention",
    )(p, p, p, pc, pc, bias)


def _na_bias_table(rpb, rows):
    h = rpb.shape[0]
    n_groups = rows // NA_GROUP
    assert rows % NA_GROUP == 0 and rows >= NA_ROWS + 2 * NA_GROUP - 1
    col = np.arange(GRID_W)
    c_start = np.clip(col - NA_COLS // 2, 0, GRID_W - NA_COLS)
    col_ok = (col[None, :] >= c_start[:, None]) & (col[None, :] < c_start[:, None] + NA_COLS)
    dc = np.clip(col[None, :] - col[:, None], 1 - NA_COLS, NA_COLS - 1) + NA_COLS - 1
    pick = (dc.reshape(1, -1) == np.arange(2 * NA_COLS - 1)[:, None]).astype(np.float32)
    t = jnp.einsum("hrc,cx->hrx", rpb, pick, precision=HIGHEST)
    t = jnp.where(col_ok.reshape(-1), t, NEG_BIG).reshape(h, 2 * NA_ROWS - 1, GRID_W, GRID_W)
    masked = jnp.full((h, GRID_W, GRID_W), NEG_BIG, F32)
    n_union = NA_ROWS + NA_GROUP - 1
    pats = []
    for g in (0, 1, n_groups - 1):
        u = _na_union_start(g, rows)
        blocks = []
        for ri in range(NA_GROUP):
            r = g * NA_GROUP + ri
            r_start = min(max(r - NA_ROWS // 2, 0), rows - NA_ROWS)
            row_blocks = []
            for kr in range(n_union):
                key_row = u + kr
                inside = r_start <= key_row < r_start + NA_ROWS
                row_blocks.append(t[:, key_row - r + NA_ROWS - 1] if inside else masked)
            blocks.append(jnp.stack(row_blocks, axis=2))
        pats.append(jnp.stack(blocks, axis=1))
    tab = jnp.stack(pats, axis=0)
    return tab.reshape(3, h, NA_GROUP * GRID_W, n_union * GRID_W)


def _mix_out_kernel(yf_ref, yb_ref, bonus_ref, gate_ref, ob_ref, x_ref, g1_ref, lg_ref, lb_ref, seg_ref, wa_ref,
                    wb_ref, o_ref, y_ref):
    _from_scan_tile(yf_ref, yb_ref, y_ref)
    nb, tt, aw = y_ref.shape
    rows = nb * tt
    y = y_ref[...].reshape(rows, aw)
    seg = seg_ref[...]
    inv = 1.0 / HEAD_DIM
    mu = _split_dot(y, seg) * inv
    yc = y - mu
    var = _split_dot(yc * yc, seg) * inv
    yn = (yc * lax.rsqrt(var + GN_EPS)) * lg_ref[...] + lb_ref[...]
    o_a = ((yn + bonus_ref[...].reshape(rows, aw)) * gate_ref[...].reshape(rows, aw)).astype(BF16)
    o_b = ob_ref[...].reshape(rows, ob_ref.shape[2])
    out = (jnp.dot(o_a, wa_ref[...], preferred_element_type=F32)
           + jnp.dot(o_b, wb_ref[...], preferred_element_type=F32))
    o_ref[...] = x_ref[...] + g1_ref[...] * out.reshape(nb, tt, out.shape[1])


def _mix_out(yf, yb, bonus, gate, o_b, x, g1, lnx_g, lnx_b, seg, w_a, w_b):
    b, t, d = x.shape
    aw = bonus.shape[2]
    tt = SCAN_BLOCK
    tok = lambda w: pl.BlockSpec((b, tt, w), lambda i: (0, i, 0))
    full = lambda a: pl.BlockSpec(a.shape, lambda i: (0,) * a.ndim)
    scan = pl.BlockSpec((tt,) + yf.shape[1:], lambda i: (i, 0, 0))
    return pl.pallas_call(
        _mix_out_kernel,
        grid=(t // tt,),
        in_specs=[scan, scan, tok(aw), tok(aw), tok(o_b.shape[2]), tok(d), full(g1),
                  full(lnx_g), full(lnx_b), full(seg), full(w_a), full(w_b)],
        out_specs=tok(d),
        out_shape=jax.ShapeDtypeStruct((b, t, d), F32),
        scratch_shapes=[pltpu.VMEM((b, tt, aw), F32)],
        compiler_params=_params("arbitrary"),
        name="mix_out",
    )(yf, yb, bonus, gate, o_b, x, g1, lnx_g, lnx_b, seg, w_a, w_b)


def _glu_kernel(x_ref, g_ref, sh_ref, sc_ref, w_ref, b_ref, o_ref):
    h = _adaln(x_ref[0], g_ref[...], sh_ref[0], sc_ref[0]).astype(BF16)
    u = jnp.dot(h, w_ref[...], preferred_element_type=F32) + b_ref[...]
    d = u.shape[1] // 2
    o_ref[0] = u[:, :d] * _sigmoid(u[:, d:])


def _glu_proj(x, g, sh, sc, w, bias):
    b, t, d = x.shape
    n = w.shape[1]
    tt = min(TOKEN_TILE, t)
    return pl.pallas_call(
        _glu_kernel,
        grid=(b, t // tt),
        in_specs=[pl.BlockSpec((1, tt, d), lambda i, j: (i, j, 0)),
                  pl.BlockSpec((1, d), lambda i, j: (0, 0)),
                  pl.BlockSpec((1, 1, d), lambda i, j: (i, 0, 0)),
                  pl.BlockSpec((1, 1, d), lambda i, j: (i, 0, 0)),
                  pl.BlockSpec((d, n), lambda i, j: (0, 0)),
                  pl.BlockSpec((1, n), lambda i, j: (0, 0))],
        out_specs=pl.BlockSpec((1, tt, n // 2), lambda i, j: (i, j, 0)),
        out_shape=jax.ShapeDtypeStruct((b, t, n // 2), F32),
        compiler_params=_params("arbitrary", "arbitrary"),
        name="glu_proj",
    )(x, g, sh, sc, w, bias)


def _conv_kernel(u_ref, up_ref, un_ref, dw_ref, dwb_ref, lg_ref, lb_ref, w2_ref, b2_ref, x_ref, g1_ref, o_ref,
                 win_ref, acc_ref, *, n_tiles):
    t = pl.program_id(1)
    tt, d = u_ref.shape[1], u_ref.shape[2]
    half = CONV_WIDTH // 2
    win_ref[0:HALO, :] = jnp.where(t > 0, up_ref[0], 0.0)
    win_ref[HALO:HALO + tt, :] = u_ref[0]
    win_ref[HALO + tt:2 * HALO + tt, :] = jnp.where(t < n_tiles - 1, un_ref[0], 0.0)
    rc = 64
    first = HALO - half

    for base in range(0, tt, rc):
        for lc in range(d // 128):
            ls = slice(lc * 128, (lc + 1) * 128)
            out = None
            for s in range(8):
                acc = None
                for a in range((first + CONV_WIDTH - 1 - s) // 8 + 1):
                    k = 8 * a + s - first
                    if k < 0:
                        continue
                    term = dw_ref[k:k + 1, ls] * win_ref[base + 8 * a:base + 8 * a + rc + 8, ls]
                    acc = term if acc is None else acc + term
                part = acc[s:s + rc]
                out = part if out is None else out + part
            acc_ref[base:base + rc, ls] = out
    u = acc_ref[...] + dwb_ref[...]
    mu = jnp.mean(u, axis=-1, keepdims=True)
    uc = u - mu
    var = jnp.mean(uc * uc, axis=-1, keepdims=True)
    un = (uc * lax.rsqrt(var + LN_EPS)) * lg_ref[...] + lb_ref[...]
    act = (un * _sigmoid(un)).astype(BF16)
    out = jnp.dot(act, w2_ref[...], preferred_element_type=F32) + b2_ref[...]
    o_ref[0] = x_ref[0] + g1_ref[0] * out


def _conv_module(u, dw, dwb, ln_g, ln_b, w2, b2, x, g1):
    b, t, d = x.shape
    tt = min(TOKEN_TILE, t)
    n_tiles = t // tt
    hb = tt // HALO
    tok = pl.BlockSpec((1, tt, d), lambda i, j: (i, j, 0))
    full = lambda a: pl.BlockSpec(a.shape, lambda i, j: (0,) * a.ndim)
    return pl.pallas_call(
        functools.partial(_conv_kernel, n_tiles=n_tiles),
        grid=(b, n_tiles),
        in_specs=[tok,
                  pl.BlockSpec((1, HALO, d), lambda i, j: (i, jnp.maximum(j * hb - 1, 0), 0)),
                  pl.BlockSpec((1, HALO, d), lambda i, j: (i, jnp.minimum((j + 1) * hb, t // HALO - 1), 0)),
                  full(dw), full(dwb), full(ln_g), full(ln_b), full(w2), full(b2), tok,
                  pl.BlockSpec((1, 1, d), lambda i, j: (i, 0, 0))],
        out_specs=tok,
        out_shape=jax.ShapeDtypeStruct((b, t, d), F32),
        scratch_shapes=[pltpu.VMEM((tt + 2 * HALO, d), F32), pltpu.VMEM((tt, d), F32)],
        compiler_params=_params("arbitrary", "arbitrary"),
        name="conv_module",
    )(u, u, u, dw, dwb, ln_g, ln_b, w2, b2, x, g1)


def _route_kernel(x_ref, g_ref, sh_ref, sc_ref, wr_ref, br_ref, hp_ref, route_ref, cnt_ref, carry_ref):
    @pl.when((pl.program_id(0) == 0) & (pl.program_id(1) == 0))
    def _():
        carry_ref[...] = jnp.zeros_like(carry_ref)

    h = _adaln(x_ref[0], g_ref[...], sh_ref[0], sc_ref[0])
    tt, d = h.shape
    hi = lax.bitcast_convert_type(h[:, :d // 2].astype(BF16).astype(F32), jnp.uint32)
    lo = lax.bitcast_convert_type(h[:, d // 2:].astype(BF16).astype(F32), jnp.uint32)
    packed = (hi & jnp.uint32(0xFFFF0000)) | (lo >> 16)
    n_ch = d // 2 // 128
    for c in range(n_ch):
        hp_ref[0, pl.ds(c, tt, stride=n_ch), :] = packed[:, c * 128:(c + 1) * 128]

    logits = _dot3(h, wr_ref[...]) + br_ref[...]
    ne = logits.shape[1]
    lane = lax.broadcasted_iota(jnp.int32, (tt, ne), 1).astype(F32)
    work = logits
    mask = jnp.zeros((tt, ne), F32)
    picks, es = [], []
    den = jnp.zeros((tt, 1), F32)
    for k in range(TOP_K):
        m = jnp.max(work, axis=-1, keepdims=True)
        idx = jnp.min(jnp.where(work == m, lane, float(ne)), axis=-1, keepdims=True)
        pick = lane == idx
        if k == 0:
            top = m
        e = jnp.exp(m - top)
        den = den + e
        picks.append((pick, idx))
        es.append(e)
        mask = jnp.where(pick, 1.0, mask)
        work = jnp.where(pick, -jnp.inf, work)

    ri = lax.broadcasted_iota(jnp.int32, (tt, tt), 0)
    ci = lax.broadcasted_iota(jnp.int32, (tt, tt), 1)
    lower = jnp.where(ci < ri, 1.0, 0.0).astype(BF16)
    rank = jnp.dot(lower, mask.astype(BF16), preferred_element_type=F32) + carry_ref[...]
    carry_ref[...] = carry_ref[...] + jnp.sum(mask, axis=0, keepdims=True)
    cnt_ref[...] = carry_ref[...]

    out_lane = lax.broadcasted_iota(jnp.int32, (tt, 128), 1)
    route = jnp.zeros((tt, 128), F32)
    for k in range(TOP_K):
        pick, idx = picks[k]
        rk = jnp.sum(jnp.where(pick, rank, 0.0), axis=-1, keepdims=True)
        route = jnp.where(out_lane == k, idx, route)
        route = jnp.where(out_lane == TOP_K + k, rk, route)
        route = jnp.where(out_lane == 2 * TOP_K + k, es[k] / den, route)
    route_ref[0] = route


def _route(x, g, sh, sc, w_r, b_r):
    b, t, d = x.shape
    ne = w_r.shape[1]
    tt = min(TOKEN_TILE, t)
    return pl.pallas_call(
        _route_kernel,
        grid=(b, t // tt),
        in_specs=[pl.BlockSpec((1, tt, d), lambda i, j: (i, j, 0)),
                  pl.BlockSpec((1, d), lambda i, j: (0, 0)),
                  pl.BlockSpec((1, 1, d), lambda i, j: (i, 0, 0)),
                  pl.BlockSpec((1, 1, d), lambda i, j: (i, 0, 0)),
                  pl.BlockSpec((d, ne), lambda i, j: (0, 0)),
                  pl.BlockSpec((1, ne), lambda i, j: (0, 0))],
        out_specs=[pl.BlockSpec((1, tt * (d // 256), 128), lambda i, j: (i, j, 0)),
                   pl.BlockSpec((1, tt, 128), lambda i, j: (i, j, 0)),
                   pl.BlockSpec((1, ne), lambda i, j: (0, 0))],
        out_shape=[jax.ShapeDtypeStruct((b, t * (d // 256), 128), jnp.uint32),
                   jax.ShapeDtypeStruct((b, t, 128), F32),
                   jax.ShapeDtypeStruct((1, ne), F32)],
        scratch_shapes=[pltpu.VMEM((1, ne), F32)],
        compiler_params=_params("arbitrary", "arbitrary"),
        name="moe_route",
    )(x, g, sh, sc, w_r, b_r)


def _dispatch_kernel(pos_hbm, hp_ref, xs_in, xs_out, idx_ref, sem_idx, sem_rows, *, tile, n_tiles, rc):
    del xs_in
    n_idx = tile * TOP_K

    def idx_copy(i, slot):
        return pltpu.make_async_copy(pos_hbm.at[i], idx_ref.at[pl.ds(slot * n_idx, n_idx)], sem_idx.at[slot])

    i = pl.program_id(0)
    slot = i % 2

    @pl.when(i == 0)
    def _():
        idx_copy(0, 0).start()

    idx_copy(i, slot).wait()

    @pl.when(i + 1 < n_tiles)
    def _():
        idx_copy(i + 1, 1 - slot).start()

    ibase = slot * n_idx

    def issue(jg, c2):
        j0 = jg * ISSUE_GROUP
        rows = [idx_ref[ibase + j0 * TOP_K + q] for q in range(ISSUE_GROUP * TOP_K)]
        for q, row in enumerate(rows):
            src = hp_ref.at[pl.ds(pl.multiple_of((j0 + q // TOP_K) * rc, rc), rc)]
            copy = pltpu.make_async_copy(src, xs_out.at[pl.ds(pl.multiple_of(row, rc), rc)], sem_rows)
            copy.start(priority=q % 2)
        return c2

    lax.fori_loop(0, tile // ISSUE_GROUP, issue, 0)
    for _ in range(TOP_K):
        pltpu.make_async_copy(hp_ref, xs_out.at[pl.ds(0, tile * rc)], sem_rows).wait()


def _dispatch(pos_flat, hp, n_rows, rc):
    n, w = hp.shape[0] // rc, hp.shape[1]
    tile = COPY_TILE
    xs0 = jnp.zeros((n_rows * rc, w), jnp.uint32)
    any_spec = pl.BlockSpec(memory_space=pl.ANY)
    return pl.pallas_call(
        functools.partial(_dispatch_kernel, tile=tile, n_tiles=n // tile, rc=rc),
        grid=(n // tile,),
        in_specs=[any_spec, pl.BlockSpec((tile * rc, w), lambda i: (i, 0)), any_spec],
        out_specs=any_spec,
        out_shape=jax.ShapeDtypeStruct((n_rows * rc, w), jnp.uint32),
        scratch_shapes=[pltpu.SMEM((2 * tile * TOP_K,), jnp.int32),
                        pltpu.SemaphoreType.DMA((2,)), pltpu.SemaphoreType.DMA],
        input_output_aliases={2: 0},
        compiler_params=_params("arbitrary"),
        name="moe_dispatch",
    )(pos_flat.reshape(n // tile, tile * TOP_K), hp, xs0)


def _expert_weights_kernel(w1_ref, p_ref, w2_ref, g_ref, l_ref, w2b_ref):
    w = w1_ref[0, 0].astype(BF16)
    n = w.shape[1]
    for c in range(n // 256):
        res = jnp.dot(w[:, c * 256:(c + 1) * 256], p_ref[...], preferred_element_type=F32)
        g_ref[0, :, c * 128:(c + 1) * 128] = res[:, :128].astype(BF16)
        l_ref[0, :, c * 128:(c + 1) * 128] = res[:, 128:].astype(BF16)
    w2b_ref[0] = w2_ref[0, 0].astype(BF16)


def _expert_weights(w1, w2, layer):
    _, ne, d, f2 = w1.shape
    f = w2.shape[2]
    n_steps = 2
    r = np.arange(256)[:, None]
    c = np.arange(256)[None, :]
    sel = jnp.asarray(np.where(c < 128, r == 2 * c, r == 2 * (c - 128) + 1), BF16)
    out1 = jax.ShapeDtypeStruct((ne, d, f2 // 2), BF16)
    return pl.pallas_call(
        _expert_weights_kernel,
        grid=(ne, n_steps),
        in_specs=[pl.BlockSpec((1, 1, d // n_steps, f2), lambda e, i: (layer, e, i, 0)),
                  pl.BlockSpec((256, 256), lambda e, i: (0, 0)),
                  pl.BlockSpec((1, 1, f // n_steps, d), lambda e, i: (layer, e, i, 0))],
        out_specs=[pl.BlockSpec((1, d // n_steps, f2 // 2), lambda e, i: (e, i, 0))] * 2
                  + [pl.BlockSpec((1, f // n_steps, d), lambda e, i: (e, i, 0))],
        out_shape=[out1, out1, jax.ShapeDtypeStruct((ne, f, d), BF16)],
        compiler_params=_params("arbitrary", "arbitrary"),
        name="expert_weights",
    )(w1, sel, w2)


def _expert_kernel(be_ref, nb_ref, xs_ref, w1g_ref, w1l_ref, b1g_ref, b1l_ref, w2_ref, b2_ref, ys_ref, *, bm):
    del be_ref
    rc = xs_ref.shape[0] // bm
    oc = ys_ref.shape[0] // bm

    @pl.when(pl.program_id(0) < nb_ref[0])
    def _():
        u = jnp.concatenate([xs_ref[pl.ds(c, bm, stride=rc), :] for c in range(rc)], axis=1)
        half = u.shape[1]
        xa = lax.bitcast_convert_type(u & jnp.uint32(0xFFFF0000), F32).astype(BF16)
        xb = lax.bitcast_convert_type(u << 16, F32).astype(BF16)
        dot = functools.partial(jnp.dot, preferred_element_type=F32)
        ug = dot(xa, w1g_ref[0, :half, :]) + dot(xb, w1g_ref[0, half:, :]) + b1g_ref[0]
        ul = dot(xa, w1l_ref[0, :half, :]) + dot(xb, w1l_ref[0, half:, :]) + b1l_ref[0]
        glu = jnp.minimum(ug, SWIGLU_LIMIT)
        lin = jnp.clip(ul, -SWIGLU_LIMIT, SWIGLU_LIMIT)
        act = (glu * _sigmoid(SWIGLU_ALPHA * glu)) * (lin + 1.0)
        y = dot(act.astype(BF16), w2_ref[0]) + b2_ref[0]
        for c in range(oc):
            ys_ref[pl.ds(c, bm, stride=oc), :] = y[:, c * 128:(c + 1) * 128]

    @pl.when(pl.program_id(0) >= nb_ref[0])
    def _():
        ys_ref[...] = jnp.zeros_like(ys_ref)


def _expert_ffn(block_e, n_used, xs, w1g, w1l, b1g, b1l, w2, b2, rc):
    n_rows = xs.shape[0] // rc
    ne, d, f = w1g.shape
    oc = d // 128
    bm = EXPERT_ROWS
    n_blocks = n_rows // bm
    wspec = lambda s: pl.BlockSpec((1,) + s, lambda i, be, nb: (be[i], 0, 0))
    return pl.pallas_call(
        functools.partial(_expert_kernel, bm=bm),
        grid_spec=pltpu.PrefetchScalarGridSpec(
            num_scalar_prefetch=2,
            grid=(n_blocks,),
            in_specs=[pl.BlockSpec((bm * rc, 128), lambda i, be, nb: (i, 0)),
                      wspec((d, f)), wspec((d, f)), wspec((1, f)), wspec((1, f)), wspec((f, d)), wspec((1, d))],
            out_specs=pl.BlockSpec((bm * oc, 128), lambda i, be, nb: (i, 0)),
        ),
        out_shape=jax.ShapeDtypeStruct((n_rows * oc, 128), F32),
        compiler_params=_params("arbitrary"),
        name="moe_experts",
    )(block_e, n_used, xs, w1g, w1l, b1g, b1l, w2, b2)


def _combine_kernel(pos_hbm, ys_hbm, x_ref, route_ref, g2_ref, fg_ref, o_ref, buf_ref, idx_ref, sem_idx, sem_rows,
                    *, tile, n_tiles, final):
    i = pl.program_id(0)
    n_idx = tile * TOP_K
    slot = i % 2
    oc = x_ref.shape[1] // 128

    def idx_copy(t, s):
        src = pos_hbm.at[pl.ds(pl.multiple_of(t * n_idx, n_idx), n_idx)]
        return pltpu.make_async_copy(src, idx_ref.at[pl.ds(s * n_idx, n_idx)], sem_idx.at[s])

    def gather(s):
        ibase = s * n_idx

        def issue(jg, c):
            j0 = jg * ISSUE_GROUP
            rows = [idx_ref[ibase + j0 * TOP_K + q] for q in range(ISSUE_GROUP * TOP_K)]
            for q, row in enumerate(rows):
                dst_row = pl.multiple_of((j0 + q // TOP_K) * oc, oc)
                copy = pltpu.make_async_copy(ys_hbm.at[pl.ds(pl.multiple_of(row, oc), oc)],
                                             buf_ref.at[s, q % TOP_K, pl.ds(dst_row, oc)], sem_rows.at[s])
                copy.start(priority=q % 2)
            return c
        lax.fori_loop(0, tile // ISSUE_GROUP, issue, 0)

    @pl.when(i == 0)
    def _():
        idx_copy(0, 0).start()
        idx_copy(0, 0).wait()
        gather(0)
        if n_tiles > 1:
            idx_copy(1, 1).start()

    for k in range(TOP_K):
        pltpu.make_async_copy(ys_hbm.at[pl.ds(0, tile * oc)], buf_ref.at[slot, k], sem_rows.at[slot]).wait()

    @pl.when(i + 1 < n_tiles)
    def _():
        idx_copy(i + 1, 1 - slot).wait()
        gather(1 - slot)

    @pl.when(i + 2 < n_tiles)
    def _():
        idx_copy(i + 2, slot).start()

    route = route_ref[...]
    chunks = []
    for c in range(oc):
        acc = jnp.zeros((tile, 128), F32)
        for k in range(TOP_K):
            acc = acc + buf_ref[slot, k, pl.ds(c, tile, stride=oc), :] * route[:, 2 * TOP_K + k:2 * TOP_K + k + 1]
        chunks.append(acc)
    x = x_ref[...] + g2_ref[0] * jnp.concatenate(chunks, axis=1)
    if final:
        x = (x * lax.rsqrt(jnp.mean(x * x, axis=-1, keepdims=True) + RMS_EPS)) * fg_ref[...]
    o_ref[...] = x


def _combine(pos_flat, ys, x2, route2, g2, final_g, tiles_per_batch, final):
    n, d = x2.shape
    tile = COPY_TILE
    return pl.pallas_call(
        functools.partial(_combine_kernel, tile=tile, n_tiles=n // tile, final=final),
        grid=(n // tile,),
        in_specs=[pl.BlockSpec(memory_space=pl.ANY),
                  pl.BlockSpec(memory_space=pl.ANY),
                  pl.BlockSpec((tile, d), lambda i: (i, 0)),
                  pl.BlockSpec((tile, 128), lambda i: (i, 0)),
                  pl.BlockSpec((1, 1, d), lambda i: (i // tiles_per_batch, 0, 0)),
                  pl.BlockSpec((1, d), lambda i: (0, 0))],
        out_specs=pl.BlockSpec((tile, d), lambda i: (i, 0)),
        out_shape=jax.ShapeDtypeStruct((n, d), F32),
        scratch_shapes=[pltpu.VMEM((2, TOP_K, tile * (d // 128), 128), F32),
                        pltpu.SMEM((2 * tile * TOP_K,), jnp.int32),
                        pltpu.SemaphoreType.DMA((2,)), pltpu.SemaphoreType.DMA((2,))],
        compiler_params=_params("arbitrary"),
        name="moe_combine",
    )(pos_flat, ys, x2, route2, g2, final_g)


def _moe_layer(x, g, sh, sc, gate2, w_r, b_r, w1_all, w2_all, layer, b1, b2, final_g, final):
    b, t, d = x.shape
    n = b * t
    ne = w_r.shape[1]
    bm = EXPERT_ROWS
    hp, route, counts = _route(x, g, sh, sc, w_r, b_r.reshape(1, ne))
    route2 = route.reshape(n, 128)

    counts = counts[0].astype(jnp.int32)
    padded = (counts + bm - 1) // bm * bm
    pad_end = jnp.cumsum(padded)
    pad_start = pad_end - padded
    n_blocks = -(-(n * TOP_K + ne * (bm - 1)) // bm)
    e_idx = route2[:, :TOP_K].astype(jnp.int32)
    experts = jnp.arange(ne, dtype=jnp.int32)
    start_of = jnp.sum(jnp.where(e_idx[:, :, None] == experts, pad_start, 0), axis=-1)
    pos_flat = (start_of + route2[:, TOP_K:2 * TOP_K].astype(jnp.int32)).reshape(-1)
    block_row = jnp.arange(n_blocks, dtype=jnp.int32) * bm
    block_e = jnp.minimum(jnp.sum((pad_end[None, :] <= block_row[:, None]).astype(jnp.int32), axis=-1), ne - 1)
    n_used = (pad_end[-1:] // bm).astype(jnp.int32)

    rc = d // 256
    xs = _dispatch(pos_flat * rc, hp.reshape(n * rc, 128), n_blocks * bm, rc)
    f = w2_all.shape[2]
    w1g, w1l, w2b = _expert_weights(w1_all, w2_all, layer)
    b1g = b1[:, 0::2].reshape(ne, 1, f)
    b1l = b1[:, 1::2].reshape(ne, 1, f)
    ys = _expert_ffn(block_e, n_used, xs, w1g, w1l, b1g, b1l, w2b, b2.reshape(ne, 1, d), rc)
    out = _combine(pos_flat * (d // 128), ys, x.reshape(n, d), route2, gate2, final_g.reshape(1, d),
                   t // COPY_TILE, final)
    return out.reshape(b, t, d)


def _even_layer(x, ctx, mod, mod_c, norm_g, w_in, mu_prev, mu_next, w0, w2, a0, a2, g2, key_k, key_a, r_k,
                lnx_g, lnx_b, rpb, w_out):
    b, t, d = x.shape
    l = ctx.shape[1]
    aw = key_k.shape[0]
    dl = w2.shape[1]
    bw = (w_in.shape[1] - 3 * aw - 128 - 4 * dl) // 3
    n_heads = aw // HEAD_DIM
    sh1, sc1, g1 = (mod[:, None, i * d:(i + 1) * d] for i in range(3))
    shc = jnp.broadcast_to(mod_c[None, None, :d], (b, 1, d))
    scc = jnp.broadcast_to(mod_c[None, None, d:2 * d], (b, 1, d))

    c_ra, c_gd = bw, bw + aw
    c_ka = c_gd + 128
    c_va = c_ka + aw
    c_wd = c_va + aw
    c_ad = c_wd + 2 * dl
    c_kb = c_ad + 2 * dl
    c_vb = c_kb + bw
    cols = lambda a, lo, hi: a[..., lo:hi]
    pad = jnp.zeros((d, 128), F32)
    w_p = jnp.concatenate([cols(w_in, c_ra, c_gd), cols(w_in, c_ka, c_va), cols(w_in, c_va, c_wd),
                           cols(w_in, c_gd, c_ka), cols(w_in, c_wd, c_ad), cols(w_in, c_ad, c_kb), pad,
                           cols(w_in, 0, c_ra), cols(w_in, c_kb, c_vb), cols(w_in, c_vb, c_vb + bw)],
                          axis=1).astype(BF16)

    def shift_vec(mu):
        o = lambda c: c - c_ra
        return jnp.concatenate([mu[o(c_ra):o(c_gd)], mu[o(c_ka):o(c_va)], mu[o(c_va):o(c_wd)], mu[o(c_gd):o(c_ka)],
                                mu[o(c_wd):o(c_ad)], mu[o(c_ad):o(c_kb)], jnp.zeros((128,), F32)]).reshape(1, -1)

    blockdiag = lambda m: jnp.concatenate(
        [jnp.concatenate([m[0], jnp.zeros_like(m[0])], axis=1),
         jnp.concatenate([jnp.zeros_like(m[1]), m[1]], axis=1)], axis=0)
    head = jnp.arange(aw) // HEAD_DIM
    consts = {
        "mu_prev": shift_vec(mu_prev), "mu_next": shift_vec(mu_next),
        "w0": w0.reshape(1, 2 * aw), "w2": blockdiag(w2), "a0": a0.reshape(1, 2 * aw), "a2": blockdiag(a2),
        "g2": g2.astype(BF16), "key_k": key_k.reshape(1, aw), "key_a": key_a.reshape(1, aw),
        "r_k": r_k.reshape(1, aw), "seg": (head[:, None] == head[None, :]).astype(BF16),
    }

    g_row = norm_g.reshape(1, d)
    p = _in_proj(x, g_row, sh1, sc1, w_p)
    pc = _in_proj(ctx, g_row, shc, scc, w_p)
    z_m, bonus, gate = _rwkv_terms(p, consts)
    z_c, _, _ = _rwkv_terms(pc, consts)

    yf, yb = _wkv_scan(_to_scan_layout(z_c), _to_scan_layout(z_m))

    qb = (4 * aw) // 128
    o_b = _neighbourhood_attention(p, pc, _na_bias_table(rpb, t // GRID_W), qb, qb + bw // 128, qb + 2 * bw // 128)
    return _mix_out(yf, yb, bonus, gate, o_b, x, g1, lnx_g.reshape(1, aw), lnx_b.reshape(1, aw), consts["seg"],
                    w_out[:aw].astype(BF16), w_out[aw:].astype(BF16))


def _odd_layer(x, mod, norm_g, pw1_w, pw1_b, dw_w, dw_b, ln_g, ln_b, pw2_w, pw2_b):
    b, t, d = x.shape
    sh1, sc1, g1 = (mod[:, None, i * d:(i + 1) * d] for i in range(3))
    u = _glu_proj(x, norm_g.reshape(1, d), sh1, sc1, pw1_w.astype(BF16), pw1_b.reshape(1, -1))
    dw = jnp.concatenate([dw_w, jnp.zeros((1, d), F32)], axis=0)
    return _conv_module(u, dw, dw_b.reshape(1, d), ln_g.reshape(1, d), ln_b.reshape(1, d), pw2_w.astype(BF16),
                        pw2_b.reshape(1, d), x, g1)


def kernel(x, c, ctx, c_ctx, ada_w, ada_b, norm_mix_g, norm_ffn_g, final_norm_g, mix_w_in, shift_mu_prev, shift_mu_next, decay_w0, decay_w2, iclr_a0, iclr_a2, gate_g2, key_k, key_a, bonus_r_k, lnx_g, lnx_b, na_rpb, mix_w_out, conv_pw1_w, conv_pw1_b, conv_dw_w, conv_dw_b, conv_ln_g, conv_ln_b, conv_pw2_w, conv_pw2_b, router_w, router_b, expert_w1, expert_b1, expert_w2, expert_b2):
    b, t, d = x.shape
    depth = ada_w.shape[0]
    rows = -(-(b + 1) // 8) * 8
    c_all = jnp.concatenate([c, c_ctx[None, :], jnp.zeros((rows - b - 1, d), F32)], axis=0)
    mod_all = _modulation(c_all, ada_w, ada_b)
    for l in range(depth):
        mod = mod_all[l, :b]
        i = l // 2
        if l % 2 == 0:
            x = _even_layer(x, ctx, mod, mod_all[l, b], norm_mix_g[l], mix_w_in[i], shift_mu_prev[i], shift_mu_next[i],
                            decay_w0[i], decay_w2[i], iclr_a0[i], iclr_a2[i], gate_g2[i], key_k[i], key_a[i],
                            bonus_r_k[i].reshape(-1), lnx_g[i], lnx_b[i], na_rpb[i], mix_w_out[i])
        else:
            x = _odd_layer(x, mod, norm_mix_g[l], conv_pw1_w[i], conv_pw1_b[i], conv_dw_w[i], conv_dw_b[i],
                           conv_ln_g[i], conv_ln_b[i], conv_pw2_w[i], conv_pw2_b[i])
        sh2, sc2, g2 = (mod[:, None, j * d:(j + 1) * d] for j in range(3, 6))
        x = _moe_layer(x, norm_ffn_g[l].reshape(1, d), sh2, sc2, g2, router_w[l], router_b[l], expert_w1, expert_w2,
                       l, expert_b1[l], expert_b2[l], final_norm_g, final=(l == depth - 1))
    return x
```

```python
import functools

import jax
import jax.numpy as jnp
import numpy as np
from jax import lax
from jax.experimental import pallas as pl
from jax.experimental.pallas import tpu as pltpu

F32 = jnp.float32
BF16 = jnp.bfloat16
HIGHEST = lax.Precision.HIGHEST

HEAD_DIM = 64
GRID_W = 64
NA_ROWS = 8
NA_COLS = 16
CONV_WIDTH = 31
N_EXPERTS = 32
TOP_K = 4
SWIGLU_ALPHA = 1.702
SWIGLU_LIMIT = 7.0
RMS_EPS = 1e-6
LN_EPS = 1e-5
GN_EPS = 64e-5
NEG_BIG = -1e30

VMEM_LIMIT_BYTES = 52 * 1024 * 1024
TOKEN_TILE = 256
SCAN_BLOCK = 16
EXPERT_ROWS = 512
HALO = 16
COPY_TILE = 512
ISSUE_GROUP = 4
NA_GROUP = 4


def _params(*sem):
    return pltpu.CompilerParams(dimension_semantics=sem, vmem_limit_bytes=VMEM_LIMIT_BYTES)


def _adaln(x, g, sh, sc):
    y = x * lax.rsqrt(jnp.mean(x * x, axis=-1, keepdims=True) + RMS_EPS)
    return (y * g) * (1.0 + sc) + sh


def _sigmoid(x):
    return 1.0 / (1.0 + jnp.exp(-x))


def _split_dot(x, m):
    hi = x.astype(BF16)
    r1 = x - hi.astype(F32)
    mid = r1.astype(BF16)
    lo = (r1 - mid.astype(F32)).astype(BF16)
    dot = functools.partial(jnp.dot, preferred_element_type=F32)
    return dot(hi, m) + dot(mid, m) + dot(lo, m)


def _dot3(x, w):
    xh = x.astype(BF16)
    xl = (x - xh.astype(F32)).astype(BF16)
    wh = w.astype(BF16)
    wl = (w - wh.astype(F32)).astype(BF16)
    dot = functools.partial(jnp.dot, preferred_element_type=F32)
    return dot(xh, wh) + (dot(xl, wh) + dot(xh, wl))


def _mod_kernel(c_ref, w_ref, b_ref, o_ref):
    c = c_ref[...]
    s = c * _sigmoid(c)
    o_ref[0] = jnp.dot(s, w_ref[0], preferred_element_type=F32, precision=HIGHEST) + b_ref[0]


def _modulation(c_all, ada_w, ada_b):
    depth, d, n = ada_w.shape
    rows = c_all.shape[0]
    tn = 1536
    return pl.pallas_call(
        _mod_kernel,
        grid=(depth, n // tn),
        in_specs=[pl.BlockSpec((rows, d), lambda l, j: (0, 0)),
                  pl.BlockSpec((1, d, tn), lambda l, j: (l, 0, j)),
                  pl.BlockSpec((1, 1, tn), lambda l, j: (l, 0, j))],
        out_specs=pl.BlockSpec((1, rows, tn), lambda l, j: (l, 0, j)),
        out_shape=jax.ShapeDtypeStruct((depth, rows, n), F32),
        compiler_params=_params("arbitrary", "arbitrary"),
        name="modulation",
    )(c_all, ada_w, ada_b.reshape(depth, 1, n))


def _proj_kernel(x_ref, g_ref, sh_ref, sc_ref, w_ref, o_ref):
    h = _adaln(x_ref[0], g_ref[...], sh_ref[0], sc_ref[0]).astype(BF16)
    o_ref[0] = jnp.dot(h, w_ref[...], preferred_element_type=F32)


def _in_proj(x, g, sh, sc, w):
    b, t, d = x.shape
    n = w.shape[1]
    tt = min(TOKEN_TILE, t)
    return pl.pallas_call(
        _proj_kernel,
        grid=(b, t // tt),
        in_specs=[pl.BlockSpec((1, tt, d), lambda i, j: (i, j, 0)),
                  pl.BlockSpec((1, d), lambda i, j: (0, 0)),
                  pl.BlockSpec((1, 1, d), lambda i, j: (i, 0, 0)),
                  pl.BlockSpec((1, 1, d), lambda i, j: (i, 0, 0)),
                  pl.BlockSpec((d, n), lambda i, j: (0, 0))],
        out_specs=pl.BlockSpec((1, tt, n), lambda i, j: (i, j, 0)),
        out_shape=jax.ShapeDtypeStruct((b, t, n), F32),
        compiler_params=_params("arbitrary", "arbitrary"),
        name="in_proj",
    )(x, g, sh, sc, w)


def _terms_kernel(p_ref, pp_ref, pn_ref, mup_ref, mun_ref, w0_ref, w2_ref, a0_ref, a2_ref, g2_ref,
                  kk_ref, ka_ref, rk_ref, seg_ref, z_ref, bonus_ref, gate_ref, *, n_tiles):
    t = pl.program_id(1)
    p = p_ref[0]
    tt, aw = p.shape[0], kk_ref.shape[1]
    prev_row = jnp.where(t > 0, pp_ref[0, 7:8, :], 0.0)
    next_row = jnp.where(t < n_tiles - 1, pn_ref[0, 0:1, :], 0.0)
    rows = lax.broadcasted_iota(jnp.int32, p.shape, 0)
    prev = jnp.where(rows == 0, prev_row, pltpu.roll(p, 1, axis=0))
    nxt = jnp.where(rows == tt - 1, next_row, pltpu.roll(p, tt - 1, axis=0))
    s = p + mup_ref[...] * (prev - p) + mun_ref[...] * (nxt - p)
    r, k, v = s[:, :aw], s[:, aw:2 * aw], s[:, 2 * aw:3 * aw]
    lora = s[:, 3 * aw:]
    g_in, wd, ad = lora[:, 0:128], lora[:, 128:256], lora[:, 256:384]
    dotf = _dot3
    zw = -(w0_ref[...] + dotf(jnp.tanh(wd), w2_ref[...]))
    softplus = jnp.maximum(zw, 0.0) + jnp.log(1.0 + jnp.exp(-jnp.abs(zw)))
    decay = jnp.exp(-jnp.exp(-softplus - 0.5))
    a = _sigmoid(a0_ref[...] + dotf(ad, a2_ref[...]))
    seg = seg_ref[...]
    kk = k * kk_ref[...]
    kk = kk / jnp.maximum(jnp.sqrt(_split_dot(kk * kk, seg)), 1e-12)
    z_ref[0, 0] = kk
    z_ref[1, 0] = v
    z_ref[2, 0] = r
    kd_sum = jnp.zeros_like(k)
    for d in range(2):
        a_d = a[:, d * aw:(d + 1) * aw]
        k_dir = k * (1.0 + (a_d - 1.0) * ka_ref[...])
        z_ref[3 + 3 * d, 0] = decay[:, d * aw:(d + 1) * aw]
        z_ref[4 + 3 * d, 0] = k_dir
        z_ref[5 + 3 * d, 0] = kk * a_d
        kd_sum = kd_sum + k_dir
    bonus_ref[0] = _split_dot(r * kd_sum * rk_ref[...], seg) * v
    gate_ref[0] = jnp.dot(_sigmoid(g_in).astype(BF16), g2_ref[...], preferred_element_type=F32)


def _rwkv_terms(p, consts):
    b, t, _ = p.shape
    aw = consts["key_k"].shape[1]
    sw = 4 * aw
    tt = min(TOKEN_TILE, t)
    n_tiles = t // tt
    hb = tt // 8
    full = lambda a: pl.BlockSpec(a.shape, lambda i, j: (0,) * a.ndim)
    names = ("mu_prev", "mu_next", "w0", "w2", "a0", "a2", "g2", "key_k", "key_a", "r_k", "seg")
    cs = [consts[n] for n in names]
    out3 = jax.ShapeDtypeStruct((b, t, aw), F32)
    return pl.pallas_call(
        functools.partial(_terms_kernel, n_tiles=n_tiles),
        grid=(b, n_tiles),
        in_specs=[pl.BlockSpec((1, tt, sw), lambda i, j: (i, j, 0)),
                  pl.BlockSpec((1, 8, sw), lambda i, j: (i, jnp.maximum(j * hb - 1, 0), 0)),
                  pl.BlockSpec((1, 8, sw), lambda i, j: (i, jnp.minimum((j + 1) * hb, t // 8 - 1), 0))]
                 + [full(a) for a in cs],
        out_specs=[pl.BlockSpec((9, 1, tt, aw), lambda i, j: (0, i, j, 0)),
                   pl.BlockSpec((1, tt, aw), lambda i, j: (i, j, 0)),
                   pl.BlockSpec((1, tt, aw), lambda i, j: (i, j, 0))],
        out_shape=[jax.ShapeDtypeStruct((9, b, t, aw), F32), out3, out3],
        compiler_params=_params("arbitrary", "arbitrary"),
        name="rwkv_terms",
    )(p, p, p, *cs)


def _scan_kernel(csf_ref, cdf_ref, csb_ref, cdb_ref, zsf_ref, zdf_ref, zsb_ref, zdb_ref, yf_ref, yb_ref, s_ref,
                 *, tb, nc):
    g = pl.program_id(0)

    @pl.when(g == 0)
    def _():
        s_ref[...] = jnp.zeros_like(s_ref)

    n = s_ref.shape[1]

    def run(dirs):
        def step(tf, carry):
            tidx = (tf, tb - 1 - tf)
            vecs = []
            for d, (zs, zd, _) in enumerate(dirs):
                ti = tidx[d]
                kk, r = zs[ti, 0], zs[ti, 2]
                w, kd, bb = zd[ti, 0], zd[ti, 1], zd[ti, 2]
                bbr = jnp.sum(bb * r, axis=0, keepdims=True)
                kr = jnp.sum(kd * r, axis=0, keepdims=True)
                vecs.append((kk, w * r, w, bb, kd, bbr, kr))

            def row(i, c):
                for d, (zs, _, y_ref) in enumerate(dirs):
                    kk, wr, w, bb, kd, bbr, kr = vecs[d]
                    ti = tidx[d]
                    si = s_ref[d, i]
                    sa = -jnp.sum(si * kk, axis=0, keepdims=True)
                    vi = zs[ti, 1, pl.ds(i, 1), :]
                    s_ref[d, i] = si * w + sa * bb + vi * kd
                    if y_ref is not None:
                        y0 = jnp.sum(si * wr, axis=0, keepdims=True)
                        y_ref[ti, pl.ds(i, 1), :] = y0 + sa * bbr + vi * kr
                return c

            lax.fori_loop(0, n, row, 0, unroll=16)
            return carry

        lax.fori_loop(0, tb, step, 0)

    @pl.when(g < nc)
    def _():
        run(((csf_ref, cdf_ref, None), (csb_ref, cdb_ref, None)))

    @pl.when(g >= nc)
    def _():
        run(((zsf_ref, zdf_ref, yf_ref), (zsb_ref, zdb_ref, yb_ref)))


def _wkv_scan(zc, zm):
    n, lanes = zm.shape[2:]
    tb = SCAN_BLOCK
    nc, nm = zc.shape[0] // tb, zm.shape[0] // tb
    cf = lambda g: jnp.minimum(g, nc - 1)
    cb = lambda g: jnp.maximum(nc - 1 - g, 0)
    mf = lambda g: jnp.maximum(g - nc, 0)
    mb = lambda g: jnp.minimum(nm - 1, nm - 1 + nc - g)
    blk = (tb, 3, n, lanes)
    spec = lambda t_of, part: pl.BlockSpec(blk, lambda g: (t_of(g), part, 0, 0))
    y_shape = jax.ShapeDtypeStruct((nm * tb, n, lanes), F32)
    return pl.pallas_call(
        functools.partial(_scan_kernel, tb=tb, nc=nc),
        grid=(nc + nm,),
        in_specs=[spec(cf, 0), spec(cf, 1), spec(cb, 0), spec(cb, 2),
                  spec(mf, 0), spec(mf, 1), spec(mb, 0), spec(mb, 2)],
        out_specs=[pl.BlockSpec((tb, n, lanes), lambda g: (mf(g), 0, 0)),
                   pl.BlockSpec((tb, n, lanes), lambda g: (mb(g), 0, 0))],
        out_shape=[y_shape, y_shape],
        scratch_shapes=[pltpu.VMEM((2, n, n, lanes), F32)],
        compiler_params=_params("arbitrary"),
        name="wkv_scan",
    )(zc, zc, zc, zc, zm, zm, zm, zm)


def _to_scan_kernel(x_ref, o_ref):
    n_comp, nb, tt, aw = x_ref.shape
    n_head = aw // HEAD_DIM
    low = lax.broadcasted_iota(jnp.int32, (nb, 128), 1) < HEAD_DIM

    def comp(c, carry):
        for tp in range(tt // 2):
            a = x_ref[c, :, 2 * tp, :]
            b = x_ref[c, :, 2 * tp + 1, :]
            pieces = []
            for h in range(n_head):
                ls = slice((h // 2) * 128, (h // 2 + 1) * 128)
                am, bm = a[:, ls], b[:, ls]
                if h % 2 == 0:
                    pieces.append(jnp.where(low, am, pltpu.roll(bm, HEAD_DIM, axis=1)))
                else:
                    pieces.append(jnp.where(low, pltpu.roll(am, HEAD_DIM, axis=1), bm))
            r2 = jnp.concatenate(pieces, axis=0).T
            o_ref[2 * tp, c] = r2[:HEAD_DIM]
            o_ref[2 * tp + 1, c] = r2[HEAD_DIM:]
        return carry

    lax.fori_loop(0, n_comp, comp, 0)


def _to_scan_layout(z):
    n_comp, b, t, aw = z.shape
    tt = SCAN_BLOCK
    lanes = (aw // HEAD_DIM) * b
    return pl.pallas_call(
        _to_scan_kernel,
        grid=(t // tt,),
        in_specs=[pl.BlockSpec((n_comp, b, tt, aw), lambda i: (0, 0, i, 0))],
        out_specs=pl.BlockSpec((tt, n_comp, HEAD_DIM, lanes), lambda i: (i, 0, 0, 0)),
        out_shape=jax.ShapeDtypeStruct((t, n_comp, HEAD_DIM, lanes), F32),
        compiler_params=_params("arbitrary"),
        name="to_scan_layout",
    )(z)


def _from_scan_tile(yf_ref, yb_ref, o_ref):
    tt = yf_ref.shape[0]
    nb, _, aw = o_ref.shape
    n_head = aw // HEAD_DIM
    low = lax.broadcasted_iota(jnp.int32, (nb, 128), 1) < HEAD_DIM
    for tp in range(tt // 2):
        s = jnp.concatenate([yf_ref[2 * tp] + yb_ref[2 * tp], yf_ref[2 * tp + 1] + yb_ref[2 * tp + 1]], axis=0)
        r2 = s.T
        for m in range(n_head // 2):
            pe = r2[(2 * m) * nb:(2 * m + 1) * nb]
            po = r2[(2 * m + 1) * nb:(2 * m + 2) * nb]
            ls = slice(m * 128, (m + 1) * 128)
            o_ref[:, 2 * tp, ls] = jnp.where(low, pe, pltpu.roll(po, HEAD_DIM, axis=1))
            o_ref[:, 2 * tp + 1, ls] = jnp.where(low, pltpu.roll(pe, HEAD_DIM, axis=1), po)


def _na_kernel(q_ref, k_ref, v_ref, kc_ref, vc_ref, bias_ref, o_ref, kb_ref, vb_ref, kcb_ref, vcb_ref, *, rows):
    kh = NA_ROWS
    dn = (((1,), (1,)), ((), ()))
    kb_ref[...] = k_ref[0].astype(BF16)
    vb_ref[...] = v_ref[0].astype(BF16)
    kcb_ref[...] = kc_ref[0].astype(BF16)
    vcb_ref[...] = vc_ref[0].astype(BF16)
    nq = NA_GROUP * GRID_W
    nk = (kh + NA_GROUP - 1) * GRID_W
    n_groups = rows // NA_GROUP
    head_of_lane = lax.broadcasted_iota(jnp.int32, (nq, 2 * HEAD_DIM), 1) // HEAD_DIM

    def group(g, c):
        u = _na_union_start(g, rows)
        pat = jnp.where(g > 0, 1, 0) + jnp.where(g == n_groups - 1, 1, 0)
        q0 = pl.multiple_of(g * nq, nq)
        k0 = pl.multiple_of(u * GRID_W, GRID_W)
        q2 = q_ref[0, pl.ds(q0, nq), :] * (HEAD_DIM ** -0.5)
        kl = kb_ref[pl.ds(k0, nk), :]
        vl = vb_ref[pl.ds(k0, nk), :]
        out = jnp.zeros((nq, 2 * HEAD_DIM), F32)
        for hh in range(2):
            q = jnp.where(head_of_lane == hh, q2, 0.0).astype(BF16)
            s_loc = lax.dot_general(q, kl, dn, preferred_element_type=F32) + bias_ref[pat, hh]
            s_ctx = lax.dot_general(q, kcb_ref[...], dn, preferred_element_type=F32)
            m = jnp.maximum(jnp.max(s_loc, axis=-1, keepdims=True), jnp.max(s_ctx, axis=-1, keepdims=True))
            e_loc = jnp.exp(s_loc - m)
            e_ctx = jnp.exp(s_ctx - m)
            den = jnp.sum(e_loc, axis=-1, keepdims=True) + jnp.sum(e_ctx, axis=-1, keepdims=True)
            o = (jnp.dot(e_loc.astype(BF16), vl, preferred_element_type=F32)
                 + jnp.dot(e_ctx.astype(BF16), vcb_ref[...], preferred_element_type=F32))
            out = jnp.where(head_of_lane == hh, o / den, out)
        o_ref[0, pl.ds(q0, nq), :] = out.astype(o_ref.dtype)
        return c

    lax.fori_loop(0, n_groups, group, 0)


def _na_union_start(g, rows):
    lo = NA_GROUP * g - NA_ROWS // 2
    hi = rows - (NA_ROWS + NA_GROUP - 1)
    if isinstance(g, int):
        return min(max(lo, 0), hi)
    return jnp.clip(lo, 0, hi)


def _neighbourhood_attention(p, pc, bias, col_q, col_k, col_v):
    b, t, _ = p.shape
    l = pc.shape[1]
    rows = t // GRID_W
    n_pairs = bias.shape[1] // 2
    return pl.pallas_call(
        functools.partial(_na_kernel, rows=rows),
        grid=(n_pairs, b),
        in_specs=[pl.BlockSpec((1, t, 128), lambda h, i: (i, 0, col_q + h)),
                  pl.BlockSpec((1, t, 128), lambda h, i: (i, 0, col_k + h)),
                  pl.BlockSpec((1, t, 128), lambda h, i: (i, 0, col_v + h)),
                  pl.BlockSpec((1, l, 128), lambda h, i: (i, 0, col_k + h)),
                  pl.BlockSpec((1, l, 128), lambda h, i: (i, 0, col_v + h)),
                  pl.BlockSpec((bias.shape[0], 2) + bias.shape[2:], lambda h, i: (0, h, 0, 0))],
        out_specs=pl.BlockSpec((1, t, 128), lambda h, i: (i, 0, h)),
        out_shape=jax.ShapeDtypeStruct((b, t, n_pairs * 128), BF16),
        scratch_shapes=[pltpu.VMEM((t, 128), BF16), pltpu.VMEM((t, 128), BF16),
                        pltpu.VMEM((l, 128), BF16), pltpu.VMEM((l, 128), BF16)],
        compiler_params=_params("arbitrary", "arbitrary"),
        name="na_attention",
    )(p, p, p, pc, pc, bias)


def _na_bias_table(rpb, rows):
    h = rpb.shape[0]
    n_groups = rows // NA_GROUP
    assert rows % NA_GROUP == 0 and rows >= NA_ROWS + 2 * NA_GROUP - 1
    col = np.arange(GRID_W)
    c_start = np.clip(col - NA_COLS // 2, 0, GRID_W - NA_COLS)
    col_ok = (col[None, :] >= c_start[:, None]) & (col[None, :] < c_start[:, None] + NA_COLS)
    dc = np.clip(col[None, :] - col[:, None], 1 - NA_COLS, NA_COLS - 1) + NA_COLS - 1
    pick = (dc.reshape(1, -1) == np.arange(2 * NA_COLS - 1)[:, None]).astype(np.float32)
    t = jnp.einsum("hrc,cx->hrx", rpb, pick, precision=HIGHEST)
    t = jnp.where(col_ok.reshape(-1), t, NEG_BIG).reshape(h, 2 * NA_ROWS - 1, GRID_W, GRID_W)
    masked = jnp.full((h, GRID_W, GRID_W), NEG_BIG, F32)
    n_union = NA_ROWS + NA_GROUP - 1
    pats = []
    for g in (0, 1, n_groups - 1):
        u = _na_union_start(g, rows)
        blocks = []
        for ri in range(NA_GROUP):
            r = g * NA_GROUP + ri
            r_start = min(max(r - NA_ROWS // 2, 0), rows - NA_ROWS)
            row_blocks = []
            for kr in range(n_union):
                key_row = u + kr
                inside = r_start <= key_row < r_start + NA_ROWS
                row_blocks.append(t[:, key_row - r + NA_ROWS - 1] if inside else masked)
            blocks.append(jnp.stack(row_blocks, axis=2))
        pats.append(jnp.stack(blocks, axis=1))
    tab = jnp.stack(pats, axis=0)
    return tab.reshape(3, h, NA_GROUP * GRID_W, n_union * GRID_W)


def _mix_out_kernel(yf_ref, yb_ref, bonus_ref, gate_ref, ob_ref, x_ref, g1_ref, lg_ref, lb_ref, seg_ref, wa_ref,
                    wb_ref, o_ref, y_ref):
    _from_scan_tile(yf_ref, yb_ref, y_ref)
    nb, tt, aw = y_ref.shape
    rows = nb * tt
    y = y_ref[...].reshape(rows, aw)
    seg = seg_ref[...]
    inv = 1.0 / HEAD_DIM
    mu = _split_dot(y, seg) * inv
    yc = y - mu
    var = _split_dot(yc * yc, seg) * inv
    yn = (yc * lax.rsqrt(var + GN_EPS)) * lg_ref[...] + lb_ref[...]
    o_a = ((yn + bonus_ref[...].reshape(rows, aw)) * gate_ref[...].reshape(rows, aw)).astype(BF16)
    o_b = ob_ref[...].reshape(rows, ob_ref.shape[2])
    out = (jnp.dot(o_a, wa_ref[...], preferred_element_type=F32)
           + jnp.dot(o_b, wb_ref[...], preferred_element_type=F32))
    o_ref[...] = x_ref[...] + g1_ref[...] * out.reshape(nb, tt, out.shape[1])


def _mix_out(yf, yb, bonus, gate, o_b, x, g1, lnx_g, lnx_b, seg, w_a, w_b):
    b, t, d = x.shape
    aw = bonus.shape[2]
    tt = SCAN_BLOCK
    tok = lambda w: pl.BlockSpec((b, tt, w), lambda i: (0, i, 0))
    full = lambda a: pl.BlockSpec(a.shape, lambda i: (0,) * a.ndim)
    scan = pl.BlockSpec((tt,) + yf.shape[1:], lambda i: (i, 0, 0))
    return pl.pallas_call(
        _mix_out_kernel,
        grid=(t // tt,),
        in_specs=[scan, scan, tok(aw), tok(aw), tok(o_b.shape[2]), tok(d), full(g1),
                  full(lnx_g), full(lnx_b), full(seg), full(w_a), full(w_b)],
        out_specs=tok(d),
        out_shape=jax.ShapeDtypeStruct((b, t, d), F32),
        scratch_shapes=[pltpu.VMEM((b, tt, aw), F32)],
        compiler_params=_params("arbitrary"),
        name="mix_out",
    )(yf, yb, bonus, gate, o_b, x, g1, lnx_g, lnx_b, seg, w_a, w_b)


def _glu_kernel(x_ref, g_ref, sh_ref, sc_ref, w_ref, b_ref, o_ref):
    h = _adaln(x_ref[0], g_ref[...], sh_ref[0], sc_ref[0]).astype(BF16)
    u = jnp.dot(h, w_ref[...], preferred_element_type=F32) + b_ref[...]
    d = u.shape[1] // 2
    o_ref[0] = u[:, :d] * _sigmoid(u[:, d:])


def _glu_proj(x, g, sh, sc, w, bias):
    b, t, d = x.shape
    n = w.shape[1]
    tt = min(TOKEN_TILE, t)
    return pl.pallas_call(
        _glu_kernel,
        grid=(b, t // tt),
        in_specs=[pl.BlockSpec((1, tt, d), lambda i, j: (i, j, 0)),
                  pl.BlockSpec((1, d), lambda i, j: (0, 0)),
                  pl.BlockSpec((1, 1, d), lambda i, j: (i, 0, 0)),
                  pl.BlockSpec((1, 1, d), lambda i, j: (i, 0, 0)),
                  pl.BlockSpec((d, n), lambda i, j: (0, 0)),
                  pl.BlockSpec((1, n), lambda i, j: (0, 0))],
        out_specs=pl.BlockSpec((1, tt, n // 2), lambda i, j: (i, j, 0)),
        out_shape=jax.ShapeDtypeStruct((b, t, n // 2), F32),
        compiler_params=_params("arbitrary", "arbitrary"),
        name="glu_proj",
    )(x, g, sh, sc, w, bias)


def _conv_kernel(u_ref, up_ref, un_ref, dw_ref, dwb_ref, lg_ref, lb_ref, w2_ref, b2_ref, x_ref, g1_ref, o_ref,
                 win_ref, acc_ref, *, n_tiles):
    t = pl.program_id(1)
    tt, d = u_ref.shape[1], u_ref.shape[2]
    half = CONV_WIDTH // 2
    win_ref[0:HALO, :] = jnp.where(t > 0, up_ref[0], 0.0)
    win_ref[HALO:HALO + tt, :] = u_ref[0]
    win_ref[HALO + tt:2 * HALO + tt, :] = jnp.where(t < n_tiles - 1, un_ref[0], 0.0)
    rc = 64
    first = HALO - half

    for base in range(0, tt, rc):
        for lc in range(d // 128):
            ls = slice(lc * 128, (lc + 1) * 128)
            out = None
            for s in range(8):
                acc = None
                for a in range((first + CONV_WIDTH - 1 - s) // 8 + 1):
                    k = 8 * a + s - first
                    if k < 0:
                        continue
                    term = dw_ref[k:k + 1, ls] * win_ref[base + 8 * a:base + 8 * a + rc + 8, ls]
                    acc = term if acc is None else acc + term
                part = acc[s:s + rc]
                out = part if out is None else out + part
            acc_ref[base:base + rc, ls] = out
    u = acc_ref[...] + dwb_ref[...]
    mu = jnp.mean(u, axis=-1, keepdims=True)
    uc = u - mu
    var = jnp.mean(uc * uc, axis=-1, keepdims=True)
    un = (uc * lax.rsqrt(var + LN_EPS)) * lg_ref[...] + lb_ref[...]
    act = (un * _sigmoid(un)).astype(BF16)
    out = jnp.dot(act, w2_ref[...], preferred_element_type=F32) + b2_ref[...]
    o_ref[0] = x_ref[0] + g1_ref[0] * out


def _conv_module(u, dw, dwb, ln_g, ln_b, w2, b2, x, g1):
    b, t, d = x.shape
    tt = min(TOKEN_TILE, t)
    n_tiles = t // tt
    hb = tt // HALO
    tok = pl.BlockSpec((1, tt, d), lambda i, j: (i, j, 0))
    full = lambda a: pl.BlockSpec(a.shape, lambda i, j: (0,) * a.ndim)
    return pl.pallas_call(
        functools.partial(_conv_kernel, n_tiles=n_tiles),
        grid=(b, n_tiles),
        in_specs=[tok,
                  pl.BlockSpec((1, HALO, d), lambda i, j: (i, jnp.maximum(j * hb - 1, 0), 0)),
                  pl.BlockSpec((1, HALO, d), lambda i, j: (i, jnp.minimum((j + 1) * hb, t // HALO - 1), 0)),
                  full(dw), full(dwb), full(ln_g), full(ln_b), full(w2), full(b2), tok,
                  pl.BlockSpec((1, 1, d), lambda i, j: (i, 0, 0))],
        out_specs=tok,
        out_shape=jax.ShapeDtypeStruct((b, t, d), F32),
        scratch_shapes=[pltpu.VMEM((tt + 2 * HALO, d), F32), pltpu.VMEM((tt, d), F32)],
        compiler_params=_params("arbitrary", "arbitrary"),
        name="conv_module",
    )(u, u, u, dw, dwb, ln_g, ln_b, w2, b2, x, g1)


def _route_kernel(x_ref, g_ref, sh_ref, sc_ref, wr_ref, br_ref, hp_ref, route_ref, cnt_ref, carry_ref):
    @pl.when((pl.program_id(0) == 0) & (pl.program_id(1) == 0))
    def _():
        carry_ref[...] = jnp.zeros_like(carry_ref)

    h = _adaln(x_ref[0], g_ref[...], sh_ref[0], sc_ref[0])
    tt, d = h.shape
    hi = lax.bitcast_convert_type(h[:, :d // 2].astype(BF16).astype(F32), jnp.uint32)
    lo = lax.bitcast_convert_type(h[:, d // 2:].astype(BF16).astype(F32), jnp.uint32)
    packed = (hi & jnp.uint32(0xFFFF0000)) | (lo >> 16)
    n_ch = d // 2 // 128
    for c in range(n_ch):
        hp_ref[0, pl.ds(c, tt, stride=n_ch), :] = packed[:, c * 128:(c + 1) * 128]

    logits = _dot3(h, wr_ref[...]) + br_ref[...]
    ne = logits.shape[1]
    lane = lax.broadcasted_iota(jnp.int32, (tt, ne), 1).astype(F32)
    work = logits
    mask = jnp.zeros((tt, ne), F32)
    picks, es = [], []
    den = jnp.zeros((tt, 1), F32)
    for k in range(TOP_K):
        m = jnp.max(work, axis=-1, keepdims=True)
        idx = jnp.min(jnp.where(work == m, lane, float(ne)), axis=-1, keepdims=True)
        pick = lane == idx
        if k == 0:
            top = m
        e = jnp.exp(m - top)
        den = den + e
        picks.append((pick, idx))
        es.append(e)
        mask = jnp.where(pick, 1.0, mask)
        work = jnp.where(pick, -jnp.inf, work)

    ri = lax.broadcasted_iota(jnp.int32, (tt, tt), 0)
    ci = lax.broadcasted_iota(jnp.int32, (tt, tt), 1)
    lower = jnp.where(ci < ri, 1.0, 0.0).astype(BF16)
    rank = jnp.dot(lower, mask.astype(BF16), preferred_element_type=F32) + carry_ref[...]
    carry_ref[...] = carry_ref[...] + jnp.sum(mask, axis=0, keepdims=True)
    cnt_ref[...] = carry_ref[...]

    out_lane = lax.broadcasted_iota(jnp.int32, (tt, 128), 1)
    route = jnp.zeros((tt, 128), F32)
    for k in range(TOP_K):
        pick, idx = picks[k]
        rk = jnp.sum(jnp.where(pick, rank, 0.0), axis=-1, keepdims=True)
        route = jnp.where(out_lane == k, idx, route)
        route = jnp.where(out_lane == TOP_K + k, rk, route)
        route = jnp.where(out_lane == 2 * TOP_K + k, es[k] / den, route)
    route_ref[0] = route


def _route(x, g, sh, sc, w_r, b_r):
    b, t, d = x.shape
    ne = w_r.shape[1]
    tt = min(TOKEN_TILE, t)
    return pl.pallas_call(
        _route_kernel,
        grid=(b, t // tt),
        in_specs=[pl.BlockSpec((1, tt, d), lambda i, j: (i, j, 0)),
                  pl.BlockSpec((1, d), lambda i, j: (0, 0)),
                  pl.BlockSpec((1, 1, d), lambda i, j: (i, 0, 0)),
                  pl.BlockSpec((1, 1, d), lambda i, j: (i, 0, 0)),
                  pl.BlockSpec((d, ne), lambda i, j: (0, 0)),
                  pl.BlockSpec((1, ne), lambda i, j: (0, 0))],
        out_specs=[pl.BlockSpec((1, tt * (d // 256), 128), lambda i, j: (i, j, 0)),
                   pl.BlockSpec((1, tt, 128), lambda i, j: (i, j, 0)),
                   pl.BlockSpec((1, ne), lambda i, j: (0, 0))],
        out_shape=[jax.ShapeDtypeStruct((b, t * (d // 256), 128), jnp.uint32),
                   jax.ShapeDtypeStruct((b, t, 128), F32),
                   jax.ShapeDtypeStruct((1, ne), F32)],
        scratch_shapes=[pltpu.VMEM((1, ne), F32)],
        compiler_params=_params("arbitrary", "arbitrary"),
        name="moe_route",
    )(x, g, sh, sc, w_r, b_r)


def _dispatch_kernel(pad_end_ref, pos_hbm, hp_ref, xs_out, idx_ref, zero_ref, sem_idx, sem_rows, sem_zero,
                     *, tile, n_tiles, rc):
    n_idx = tile * TOP_K
    block_rows = zero_ref.shape[0]

    def idx_copy(i, slot):
        return pltpu.make_async_copy(pos_hbm.at[i], idx_ref.at[pl.ds(slot * n_idx, n_idx)], sem_idx.at[slot])

    i = pl.program_id(0)
    slot = i % 2

    @pl.when(i == 0)
    def _():
        idx_copy(0, 0).start()
        zero_ref[...] = jnp.zeros_like(zero_ref)
        n_exp = pad_end_ref.shape[0]
        last = xs_out.shape[0] - block_rows
        starts = [jnp.maximum(pad_end_ref[e] * rc - block_rows, 0) for e in range(n_exp)]
        starts += [jnp.minimum(pad_end_ref[n_exp - 1] * rc + j * block_rows, last) for j in range(n_exp)]
        for start in starts:
            copy = pltpu.make_async_copy(zero_ref, xs_out.at[pl.ds(pl.multiple_of(start, 8), block_rows)], sem_zero)
            copy.start()
            copy.wait()

    idx_copy(i, slot).wait()

    @pl.when(i + 1 < n_tiles)
    def _():
        idx_copy(i + 1, 1 - slot).start()

    ibase = slot * n_idx

    def issue(jg, c2):
        j0 = jg * ISSUE_GROUP
        rows = [idx_ref[ibase + j0 * TOP_K + q] for q in range(ISSUE_GROUP * TOP_K)]
        for q, row in enumerate(rows):
            src = hp_ref.at[pl.ds(pl.multiple_of((j0 + q // TOP_K) * rc, rc), rc)]
            copy = pltpu.make_async_copy(src, xs_out.at[pl.ds(pl.multiple_of(row, rc), rc)], sem_rows)
            copy.start(priority=q % 2)
        return c2

    lax.fori_loop(0, tile // ISSUE_GROUP, issue, 0)
    for _ in range(TOP_K):
        pltpu.make_async_copy(hp_ref, xs_out.at[pl.ds(0, tile * rc)], sem_rows).wait()


def _dispatch(pos_flat, hp, pad_end, n_rows, rc):
    n, w = hp.shape[0] // rc, hp.shape[1]
    tile = COPY_TILE
    any_spec = pl.BlockSpec(memory_space=pl.ANY)
    return pl.pallas_call(
        functools.partial(_dispatch_kernel, tile=tile, n_tiles=n // tile, rc=rc),
        grid=(n // tile,),
        in_specs=[pl.BlockSpec(memory_space=pltpu.SMEM), any_spec, pl.BlockSpec((tile * rc, w), lambda i: (i, 0))],
        out_specs=any_spec,
        out_shape=jax.ShapeDtypeStruct((n_rows * rc, w), jnp.uint32),
        scratch_shapes=[pltpu.SMEM((2 * tile * TOP_K,), jnp.int32),
                        pltpu.VMEM((EXPERT_ROWS * rc, w), jnp.uint32),
                        pltpu.SemaphoreType.DMA((2,)), pltpu.SemaphoreType.DMA, pltpu.SemaphoreType.DMA],
        compiler_params=_params("arbitrary"),
        name="moe_dispatch",
    )(pad_end, pos_flat.reshape(n // tile, tile * TOP_K), hp)


def _expert_weights_kernel(w1_ref, p_ref, w2_ref, g_ref, l_ref, w2b_ref):
    w = w1_ref[0, 0].astype(BF16)
    n = w.shape[1]
    for c in range(n // 256):
        res = jnp.dot(w[:, c * 256:(c + 1) * 256], p_ref[...], preferred_element_type=F32)
        g_ref[0, :, c * 128:(c + 1) * 128] = res[:, :128].astype(BF16)
        l_ref[0, :, c * 128:(c + 1) * 128] = res[:, 128:].astype(BF16)
    w2b_ref[0] = w2_ref[0, 0].astype(BF16)


def _expert_weights(w1, w2, layer):
    _, ne, d, f2 = w1.shape
    f = w2.shape[2]
    n_steps = 2
    r = np.arange(256)[:, None]
    c = np.arange(256)[None, :]
    sel = jnp.asarray(np.where(c < 128, r == 2 * c, r == 2 * (c - 128) + 1), BF16)
    out1 = jax.ShapeDtypeStruct((ne, d, f2 // 2), BF16)
    return pl.pallas_call(
        _expert_weights_kernel,
        grid=(ne, n_steps),
        in_specs=[pl.BlockSpec((1, 1, d // n_steps, f2), lambda e, i: (layer, e, i, 0)),
                  pl.BlockSpec((256, 256), lambda e, i: (0, 0)),
                  pl.BlockSpec((1, 1, f // n_steps, d), lambda e, i: (layer, e, i, 0))],
        out_specs=[pl.BlockSpec((1, d // n_steps, f2 // 2), lambda e, i: (e, i, 0))] * 2
                  + [pl.BlockSpec((1, f // n_steps, d), lambda e, i: (e, i, 0))],
        out_shape=[out1, out1, jax.ShapeDtypeStruct((ne, f, d), BF16)],
        compiler_params=_params("arbitrary", "arbitrary"),
        name="expert_weights",
    )(w1, sel, w2)


def _expert_kernel(be_ref, nb_ref, xs_ref, w1g_ref, w1l_ref, b1g_ref, b1l_ref, w2_ref, b2_ref, ys_ref, *, bm):
    del be_ref
    rc = xs_ref.shape[0] // bm
    oc = ys_ref.shape[0] // bm

    @pl.when(pl.program_id(0) < nb_ref[0])
    def _():
        u = jnp.concatenate([xs_ref[pl.ds(c, bm, stride=rc), :] for c in range(rc)], axis=1)
        half = u.shape[1]
        xa = lax.bitcast_convert_type(u & jnp.uint32(0xFFFF0000), F32).astype(BF16)
        xb = lax.bitcast_convert_type(u << 16, F32).astype(BF16)
        dot = functools.partial(jnp.dot, preferred_element_type=F32)
        ug = dot(xa, w1g_ref[0, :half, :]) + dot(xb, w1g_ref[0, half:, :]) + b1g_ref[0]
        ul = dot(xa, w1l_ref[0, :half, :]) + dot(xb, w1l_ref[0, half:, :]) + b1l_ref[0]
        glu = jnp.minimum(ug, SWIGLU_LIMIT)
        lin = jnp.clip(ul, -SWIGLU_LIMIT, SWIGLU_LIMIT)
        act = (glu * _sigmoid(SWIGLU_ALPHA * glu)) * (lin + 1.0)
        y = dot(act.astype(BF16), w2_ref[0]) + b2_ref[0]
        for c in range(oc):
            ys_ref[pl.ds(c, bm, stride=oc), :] = y[:, c * 128:(c + 1) * 128]

    @pl.when(pl.program_id(0) >= nb_ref[0])
    def _():
        ys_ref[...] = jnp.zeros_like(ys_ref)


def _expert_ffn(block_e, n_used, xs, w1g, w1l, b1g, b1l, w2, b2, rc):
    n_rows = xs.shape[0] // rc
    ne, d, f = w1g.shape
    oc = d // 128
    bm = EXPERT_ROWS
    n_blocks = n_rows // bm
    wspec = lambda s: pl.BlockSpec((1,) + s, lambda i, be, nb: (be[i], 0, 0))
    return pl.pallas_call(
        functools.partial(_expert_kernel, bm=bm),
        grid_spec=pltpu.PrefetchScalarGridSpec(
            num_scalar_prefetch=2,
            grid=(n_blocks,),
            in_specs=[pl.BlockSpec((bm * rc, 128), lambda i, be, nb: (jnp.minimum(i, nb[0] - 1), 0)),
                      wspec((d, f)), wspec((d, f)), wspec((1, f)), wspec((1, f)), wspec((f, d)), wspec((1, d))],
            out_specs=pl.BlockSpec((bm * oc, 128), lambda i, be, nb: (i, 0)),
        ),
        out_shape=jax.ShapeDtypeStruct((n_rows * oc, 128), F32),
        compiler_params=_params("arbitrary"),
        name="moe_experts",
    )(block_e, n_used, xs, w1g, w1l, b1g, b1l, w2, b2)


def _combine_kernel(pos_hbm, ys_hbm, x_ref, route_ref, g2_ref, fg_ref, o_ref, buf_ref, idx_ref, sem_idx, sem_rows,
                    *, tile, n_tiles, final):
    i = pl.program_id(0)
    n_idx = tile * TOP_K
    slot = i % 2
    oc = x_ref.shape[1] // 128

    def idx_copy(t, s):
        src = pos_hbm.at[pl.ds(pl.multiple_of(t * n_idx, n_idx), n_idx)]
        return pltpu.make_async_copy(src, idx_ref.at[pl.ds(s * n_idx, n_idx)], sem_idx.at[s])

    def gather(s):
        ibase = s * n_idx

        def issue(jg, c):
            j0 = jg * ISSUE_GROUP
            rows = [idx_ref[ibase + j0 * TOP_K + q] for q in range(ISSUE_GROUP * TOP_K)]
            for q, row in enumerate(rows):
                dst_row = pl.multiple_of((j0 + q // TOP_K) * oc, oc)
                copy = pltpu.make_async_copy(ys_hbm.at[pl.ds(pl.multiple_of(row, oc), oc)],
                                             buf_ref.at[s, q % TOP_K, pl.ds(dst_row, oc)], sem_rows.at[s])
                copy.start(priority=q % 2)
            return c
        lax.fori_loop(0, tile // ISSUE_GROUP, issue, 0)

    @pl.when(i == 0)
    def _():
        idx_copy(0, 0).start()
        idx_copy(0, 0).wait()
        gather(0)
        if n_tiles > 1:
            idx_copy(1, 1).start()

    for k in range(TOP_K):
        pltpu.make_async_copy(ys_hbm.at[pl.ds(0, tile * oc)], buf_ref.at[slot, k], sem_rows.at[slot]).wait()

    @pl.when(i + 1 < n_tiles)
    def _():
        idx_copy(i + 1, 1 - slot).wait()
        gather(1 - slot)

    @pl.when(i + 2 < n_tiles)
    def _():
        idx_copy(i + 2, slot).start()

    route = route_ref[...]
    chunks = []
    for c in range(oc):
        acc = jnp.zeros((tile, 128), F32)
        for k in range(TOP_K):
            acc = acc + buf_ref[slot, k, pl.ds(c, tile, stride=oc), :] * route[:, 2 * TOP_K + k:2 * TOP_K + k + 1]
        chunks.append(acc)
    x = x_ref[...] + g2_ref[0] * jnp.concatenate(chunks, axis=1)
    if final:
        x = (x * lax.rsqrt(jnp.mean(x * x, axis=-1, keepdims=True) + RMS_EPS)) * fg_ref[...]
    o_ref[...] = x


def _combine(pos_flat, ys, x2, route2, g2, final_g, tiles_per_batch, final):
    n, d = x2.shape
    tile = COPY_TILE
    return pl.pallas_call(
        functools.partial(_combine_kernel, tile=tile, n_tiles=n // tile, final=final),
        grid=(n // tile,),
        in_specs=[pl.BlockSpec(memory_space=pl.ANY),
                  pl.BlockSpec(memory_space=pl.ANY),
                  pl.BlockSpec((tile, d), lambda i: (i, 0)),
                  pl.BlockSpec((tile, 128), lambda i: (i, 0)),
                  pl.BlockSpec((1, 1, d), lambda i: (i // tiles_per_batch, 0, 0)),
                  pl.BlockSpec((1, d), lambda i: (0, 0))],
        out_specs=pl.BlockSpec((tile, d), lambda i: (i, 0)),
        out_shape=jax.ShapeDtypeStruct((n, d), F32),
        scratch_shapes=[pltpu.VMEM((2, TOP_K, tile * (d // 128), 128), F32),
                        pltpu.SMEM((2 * tile * TOP_K,), jnp.int32),
                        pltpu.SemaphoreType.DMA((2,)), pltpu.SemaphoreType.DMA((2,))],
        compiler_params=_params("arbitrary"),
        name="moe_combine",
    )(pos_flat, ys, x2, route2, g2, final_g)


def _moe_layer(x, g, sh, sc, gate2, w_r, b_r, w1_all, w2_all, layer, b1, b2, final_g, final):
    b, t, d = x.shape
    n = b * t
    ne = w_r.shape[1]
    bm = EXPERT_ROWS
    hp, route, counts = _route(x, g, sh, sc, w_r, b_r.reshape(1, ne))
    route2 = route.reshape(n, 128)

    counts = counts[0].astype(jnp.int32)
    padded = (counts + bm - 1) // bm * bm
    pad_end = jnp.cumsum(padded)
    pad_start = pad_end - padded
    n_blocks = -(-(n * TOP_K + ne * (bm - 1)) // bm)
    e_idx = route2[:, :TOP_K].astype(jnp.int32)
    experts = jnp.arange(ne, dtype=jnp.int32)
    start_of = jnp.sum(jnp.where(e_idx[:, :, None] == experts, pad_start, 0), axis=-1)
    pos_flat = (start_of + route2[:, TOP_K:2 * TOP_K].astype(jnp.int32)).reshape(-1)
    block_row = jnp.arange(n_blocks, dtype=jnp.int32) * bm
    block_e = jnp.minimum(jnp.sum((pad_end[None, :] <= block_row[:, None]).astype(jnp.int32), axis=-1), ne - 1)
    n_used = (pad_end[-1:] // bm).astype(jnp.int32)

    rc = d // 256
    xs = _dispatch(pos_flat * rc, hp.reshape(n * rc, 128), pad_end.astype(jnp.int32), n_blocks * bm, rc)
    f = w2_all.shape[2]
    w1g, w1l, w2b = _expert_weights(w1_all, w2_all, layer)
    b1g = b1[:, 0::2].reshape(ne, 1, f)
    b1l = b1[:, 1::2].reshape(ne, 1, f)
    ys = _expert_ffn(block_e, n_used, xs, w1g, w1l, b1g, b1l, w2b, b2.reshape(ne, 1, d), rc)
    out = _combine(pos_flat * (d // 128), ys, x.reshape(n, d), route2, gate2, final_g.reshape(1, d),
                   t // COPY_TILE, final)
    return out.reshape(b, t, d)


def _even_layer(x, ctx, mod, mod_c, norm_g, w_in, mu_prev, mu_next, w0, w2, a0, a2, g2, key_k, key_a, r_k,
                lnx_g, lnx_b, rpb, w_out):
    b, t, d = x.shape
    l = ctx.shape[1]
    aw = key_k.shape[0]
    dl = w2.shape[1]
    bw = (w_in.shape[1] - 3 * aw - 128 - 4 * dl) // 3
    n_heads = aw // HEAD_DIM
    sh1, sc1, g1 = (mod[:, None, i * d:(i + 1) * d] for i in range(3))
    shc = jnp.broadcast_to(mod_c[None, None, :d], (b, 1, d))
    scc = jnp.broadcast_to(mod_c[None, None, d:2 * d], (b, 1, d))

    c_ra, c_gd = bw, bw + aw
    c_ka = c_gd + 128
    c_va = c_ka + aw
    c_wd = c_va + aw
    c_ad = c_wd + 2 * dl
    c_kb = c_ad + 2 * dl
    c_vb = c_kb + bw
    cols = lambda a, lo, hi: a[..., lo:hi]
    pad = jnp.zeros((d, 128), F32)
    w_p = jnp.concatenate([cols(w_in, c_ra, c_gd), cols(w_in, c_ka, c_va), cols(w_in, c_va, c_wd),
                           cols(w_in, c_gd, c_ka), cols(w_in, c_wd, c_ad), cols(w_in, c_ad, c_kb), pad,
                           cols(w_in, 0, c_ra), cols(w_in, c_kb, c_vb), cols(w_in, c_vb, c_vb + bw)],
                          axis=1).astype(BF16)

    def shift_vec(mu):
        o = lambda c: c - c_ra
        return jnp.concatenate([mu[o(c_ra):o(c_gd)], mu[o(c_ka):o(c_va)], mu[o(c_va):o(c_wd)], mu[o(c_gd):o(c_ka)],
                                mu[o(c_wd):o(c_ad)], mu[o(c_ad):o(c_kb)], jnp.zeros((128,), F32)]).reshape(1, -1)

    blockdiag = lambda m: jnp.concatenate(
        [jnp.concatenate([m[0], jnp.zeros_like(m[0])], axis=1),
         jnp.concatenate([jnp.zeros_like(m[1]), m[1]], axis=1)], axis=0)
    head = jnp.arange(aw) // HEAD_DIM
    consts = {
        "mu_prev": shift_vec(mu_prev), "mu_next": shift_vec(mu_next),
        "w0": w0.reshape(1, 2 * aw), "w2": blockdiag(w2), "a0": a0.reshape(1, 2 * aw), "a2": blockdiag(a2),
        "g2": g2.astype(BF16), "key_k": key_k.reshape(1, aw), "key_a": key_a.reshape(1, aw),
        "r_k": r_k.reshape(1, aw), "seg": (head[:, None] == head[None, :]).astype(BF16),
    }

    g_row = norm_g.reshape(1, d)
    p = _in_proj(x, g_row, sh1, sc1, w_p)
    pc = _in_proj(ctx, g_row, shc, scc, w_p)
    z_m, bonus, gate = _rwkv_terms(p, consts)
    z_c, _, _ = _rwkv_terms(pc, consts)

    yf, yb = _wkv_scan(_to_scan_layout(z_c), _to_scan_layout(z_m))

    qb = (4 * aw) // 128
    o_b = _neighbourhood_attention(p, pc, _na_bias_table(rpb, t // GRID_W), qb, qb + bw // 128, qb + 2 * bw // 128)
    return _mix_out(yf, yb, bonus, gate, o_b, x, g1, lnx_g.reshape(1, aw), lnx_b.reshape(1, aw), consts["seg"],
                    w_out[:aw].astype(BF16), w_out[aw:].astype(BF16))


def _odd_layer(x, mod, norm_g, pw1_w, pw1_b, dw_w, dw_b, ln_g, ln_b, pw2_w, pw2_b):
    b, t, d = x.shape
    sh1, sc1, g1 = (mod[:, None, i * d:(i + 1) * d] for i in range(3))
    u = _glu_proj(x, norm_g.reshape(1, d), sh1, sc1, pw1_w.astype(BF16), pw1_b.reshape(1, -1))
    dw = jnp.concatenate([dw_w, jnp.zeros((1, d), F32)], axis=0)
    return _conv_module(u, dw, dw_b.reshape(1, d), ln_g.reshape(1, d), ln_b.reshape(1, d), pw2_w.astype(BF16),
                        pw2_b.reshape(1, d), x, g1)


def kernel(x, c, ctx, c_ctx, ada_w, ada_b, norm_mix_g, norm_ffn_g, final_norm_g, mix_w_in, shift_mu_prev, shift_mu_next, decay_w0, decay_w2, iclr_a0, iclr_a2, gate_g2, key_k, key_a, bonus_r_k, lnx_g, lnx_b, na_rpb, mix_w_out, conv_pw1_w, conv_pw1_b, conv_dw_w, conv_dw_b, conv_ln_g, conv_ln_b, conv_pw2_w, conv_pw2_b, router_w, router_b, expert_w1, expert_b1, expert_w2, expert_b2):
    b, t, d = x.shape
    depth = ada_w.shape[0]
    rows = -(-(b + 1) // 8) * 8
    c_all = jnp.concatenate([c, c_ctx[None, :], jnp.zeros((rows - b - 1, d), F32)], axis=0)
    mod_all = _modulation(c_all, ada_w, ada_b)
    for l in range(depth):
        mod = mod_all[l, :b]
        i = l // 2
        if l % 2 == 0:
            x = _even_layer(x, ctx, mod, mod_all[l, b], norm_mix_g[l], mix_w_in[i], shift_mu_prev[i], shift_mu_next[i],
                            decay_w0[i], decay_w2[i], iclr_a0[i], iclr_a2[i], gate_g2[i], key_k[i], key_a[i],
                            bonus_r_k[i].reshape(-1), lnx_g[i], lnx_b[i], na_rpb[i], mix_w_out[i])
        else:
            x = _odd_layer(x, mod, norm_mix_g[l], conv_pw1_w[i], conv_pw1_b[i], conv_dw_w[i], conv_dw_b[i],
                           conv_ln_g[i], conv_ln_b[i], conv_pw2_w[i], conv_pw2_b[i])
        sh2, sc2, g2 = (mod[:, None, j * d:(j + 1) * d] for j in range(3, 6))
        x = _moe_layer(x, norm_ffn_g[l].reshape(1, d), sh2, sc2, g2, router_w[l], router_b[l], expert_w1, expert_w2,
                       l, expert_b1[l], expert_b2[l], final_norm_g, final=(l == depth - 1))
    return x
```
